```python
import jax
import jax.numpy as jnp
from jax import lax
import numpy as np

D_MODEL = 1024
BATCH = 8
SEQ = 4096
DEPTH = 4

N_MIXERS = 3
N_NSA = (DEPTH + 2) // 3
N_SGU = (DEPTH + 1) // 3
N_POOL = DEPTH // 3
RMS_EPS = 1e-6
LN_EPS = 1e-5
NEG_INF = -1e30
FORCE_SCORE = 1e9

NSA_HEADS = 16
NSA_KV_GROUPS = 4
NSA_HPG = NSA_HEADS // NSA_KV_GROUPS
NSA_HEAD_DIM = D_MODEL // NSA_HEADS
CMP_BLOCK = 32
CMP_STRIDE = 16
CMP_HIDDEN = 4 * NSA_HEAD_DIM
SEL_BLOCK = 64
SEL_TOPN = 16
WINDOW = 512
NSA_QBLOCK = 64
NSA_IN = NSA_HEADS * NSA_HEAD_DIM + 6 * NSA_KV_GROUPS * NSA_HEAD_DIM + 3 * NSA_HEADS
ROPE_THETA = 500000.0
ROPE_DIM = NSA_HEAD_DIM // 4

SGU_GROUPS = 8
SGU_WIDTH = D_MODEL
SGU_GROUP_DIM = SGU_WIDTH // SGU_GROUPS
SGU_CHUNK = 128

POOL_WINDOWS = (2, 4, 8, 16)
POOL_GROUPS = len(POOL_WINDOWS)
POOL_WIDTH = D_MODEL
POOL_GROUP_DIM = POOL_WIDTH // POOL_GROUPS

FFN_HIDDEN = -(-8 * D_MODEL // (3 * 256)) * 256

kernel_name = 'hybrid_nsa_gmlp_pool_swiglu_trunk'


def rmsnorm(x, g):
    xf = x.astype(jnp.float32)
    y = xf * lax.rsqrt(jnp.mean(xf * xf, axis=-1, keepdims=True) + RMS_EPS)
    return (y * g.astype(jnp.float32)).astype(x.dtype)


def layernorm(x, g, b):
    xf = x.astype(jnp.float32)
    mu = jnp.mean(xf, axis=-1, keepdims=True)
    var = jnp.mean(jnp.square(xf - mu), axis=-1, keepdims=True)
    y = (xf - mu) * lax.rsqrt(var + LN_EPS) * g.astype(jnp.float32) + b.astype(jnp.float32)
    return y.astype(x.dtype)


def rope_tables(positions):
    inv_freq = ROPE_THETA ** (-jnp.arange(0, ROPE_DIM, 2, dtype=jnp.float32) / ROPE_DIM)
    ang = positions.astype(jnp.float32)[..., None] * inv_freq
    return jnp.cos(ang)[:, None], jnp.sin(ang)[:, None]


def partial_rope(t, cos, sin):
    half = ROPE_DIM // 2
    tf = t[..., :ROPE_DIM].astype(jnp.float32)
    t1, t2 = tf[..., :half], tf[..., half:]
    rot = jnp.concatenate([t1 * cos - t2 * sin, t2 * cos + t1 * sin], axis=-1)
    return jnp.concatenate([rot.astype(t.dtype), t[..., ROPE_DIM:]], axis=-1)


def nsa_mixer(h, cos, sin, w_in, cmp_pos_k, cmp_k_w1, cmp_k_w2, cmp_pos_v, cmp_v_w1, cmp_v_w2, w_out):
    B, S, _ = h.shape
    H, G, P, dh, QB = NSA_HEADS, NSA_KV_GROUPS, NSA_HPG, NSA_HEAD_DIM, NSA_QBLOCK
    splits = [H * dh + k * G * dh for k in range(7)]
    q, kc, vc, ks, vs, kw, vw, gl = jnp.split(h @ w_in, splits, axis=-1)

    def to_heads(t, n):
        return t.reshape(B, S, n, dh).transpose(0, 2, 1, 3)

    q = (partial_rope(to_heads(q, H), cos, sin) * (dh ** -0.5)).reshape(B, G, P, S, dh)
    kc, ks, kw = [partial_rope(to_heads(t, G), cos, sin) for t in (kc, ks, kw)]
    vc, vs, vw = [to_heads(t, G) for t in (vc, vs, vw)]
    gates = jax.nn.sigmoid(gl.astype(jnp.float32)).astype(h.dtype)
    gates = gates.reshape(B, S, 3, G, P).transpose(0, 2, 3, 4, 1)

    n_cmp = (S - CMP_BLOCK) // CMP_STRIDE + 1
    cmp_start = jnp.arange(n_cmp) * CMP_STRIDE
    cmp_end = cmp_start + CMP_BLOCK - 1
    tok_idx = cmp_start[:, None] + jnp.arange(CMP_BLOCK)[None, :]

    def compress(t, pos, w1, w2):
        blk = (t[:, :, tok_idx, :] + pos).reshape(B, G, n_cmp, CMP_BLOCK * dh)
        return jax.nn.gelu(blk @ w1) @ w2

    k_cmp = compress(kc, cmp_pos_k, cmp_k_w1, cmp_k_w2)
    v_cmp = compress(vc, cmp_pos_v, cmp_v_w1, cmp_v_w2)

    n_sel = S // SEL_BLOCK
    n_top = min(SEL_TOPN, n_sel)
    sel_ids = jnp.arange(n_sel)
    sel_start = sel_ids * SEL_BLOCK
    overlap = ((cmp_start[:, None] <= sel_start[None, :] + SEL_BLOCK - 1)
               & (cmp_end[:, None] >= sel_start[None, :])).astype(jnp.float32)
    ks_blk = ks.reshape(B, G, n_sel, SEL_BLOCK, dh)
    vs_blk = vs.reshape(B, G, n_sel, SEL_BLOCK, dh)
    b_idx = jnp.arange(B)[:, None, None, None]
    g_idx = jnp.arange(G)[None, :, None, None]

    kw_pad = jnp.pad(kw, ((0, 0), (0, 0), (WINDOW, 0), (0, 0)))
    vw_pad = jnp.pad(vw, ((0, 0), (0, 0), (WINDOW, 0), (0, 0)))

    def block_fn(bi):
        s = bi * QB
        qb = lax.dynamic_slice_in_dim(q, s, QB, axis=3)
        t_pos = s + jnp.arange(QB)

        sc = jnp.einsum('bgpqd,bgnd->bgpqn', qb, k_cmp).astype(jnp.float32)
        valid_c = cmp_end[None, :] <= t_pos[:, None]
        p_c = jax.nn.softmax(jnp.where(valid_c, sc, NEG_INF), axis=-1)
        p_c = jnp.where(valid_c.any(-1)[:, None], p_c, 0.0)
        o_c = jnp.einsum('bgpqn,bgnd->bgpqd', p_c.astype(v_cmp.dtype), v_cmp)

        imp = jnp.einsum('bgpqn,nm->bgqm', p_c, overlap)
        cur = (t_pos // SEL_BLOCK)[:, None]
        forced = (sel_ids[None, :] == 0) | (sel_ids[None, :] == cur) | (sel_ids[None, :] == cur - 1)
        imp = jnp.where(forced, FORCE_SCORE, imp)
        imp = jnp.where(sel_start[None, :] <= t_pos[:, None], imp, NEG_INF)
        _, sel_idx = lax.top_k(imp, n_top)
        k_sel = ks_blk[b_idx, g_idx, sel_idx]
        v_sel = vs_blk[b_idx, g_idx, sel_idx]
        key_pos = sel_idx[..., None] * SEL_BLOCK + jnp.arange(SEL_BLOCK)
        valid_s = key_pos <= t_pos[None, None, :, None, None]
        ss = jnp.einsum('bgpqd,bgqnld->bgpqnl', qb, k_sel).astype(jnp.float32)
        ss = jnp.where(valid_s[:, :, None], ss, NEG_INF)
        p_s = jax.nn.softmax(ss.reshape(B, G, P, QB, n_top * SEL_BLOCK), axis=-1)
        p_s = p_s.reshape(B, G, P, QB, n_top, SEL_BLOCK).astype(v_sel.dtype)
        o_s = jnp.einsum('bgpqnl,bgqnld->bgpqd', p_s, v_sel)

        kwb = lax.dynamic_slice_in_dim(kw_pad, s, WINDOW + QB, axis=2)
        vwb = lax.dynamic_slice_in_dim(vw_pad, s, WINDOW + QB, axis=2)
        kpos = s - WINDOW + jnp.arange(WINDOW + QB)
        rel = t_pos[:, None] - kpos[None, :]
        valid_w = (rel >= 0) & (rel < WINDOW) & (kpos[None, :] >= 0)
        sw = jnp.einsum('bgpqd,bgkd->bgpqk', qb, kwb).astype(jnp.float32)
        p_w = jax.nn.softmax(jnp.where(valid_w, sw, NEG_INF), axis=-1).astype(vwb.dtype)
        o_w = jnp.einsum('bgpqk,bgkd->bgpqd', p_w, vwb)

        gb = lax.dynamic_slice_in_dim(gates, s, QB, axis=4)[..., None]
        o = gb[:, 0] * o_c + gb[:, 1] * o_s + gb[:, 2] * o_w
        return o.astype(h.dtype)

    out = lax.map(block_fn, jnp.arange(S // QB))
    out = out.transpose(1, 0, 4, 2, 3, 5).reshape(B, S, H * dh)
    return out @ w_out


def sgu_mixer(h, w_in, ln_g, ln_b, w_s, b_s, w_out):
    B, S, _ = h.shape
    T = SGU_CHUNK
    u, v = jnp.split(jax.nn.gelu(h @ w_in), 2, axis=-1)
    v = layernorm(v, ln_g, ln_b).reshape(B, S // T, T, SGU_GROUPS, SGU_GROUP_DIM)
    ws = jnp.where(jnp.tril(jnp.ones((T, T), dtype=bool)), w_s, 0.0)
    sv = jnp.einsum('gts,bnsgc->bntgc', ws, v) + b_s.T[None, None, :, :, None]
    return (u * sv.reshape(B, S, SGU_WIDTH).astype(u.dtype)) @ w_out


def pool_mixer(h, w_in, w_grp, scale, w_out):
    B, S, _ = h.shape
    z = (h @ w_in).astype(jnp.float32)
    csum = jnp.pad(jnp.cumsum(z, axis=1), ((0, 0), (1, 0), (0, 0)))
    t = jnp.arange(S)
    diffs = []
    for gi, w in enumerate(POOL_WINDOWS):
        sl = slice(gi * POOL_GROUP_DIM, (gi + 1) * POOL_GROUP_DIM)
        lo = jnp.maximum(t + 1 - w, 0)
        win = csum[:, 1:, sl] - csum[:, lo, sl]
        cnt = (t + 1 - lo).astype(jnp.float32)[None, :, None]
        diffs.append(win / cnt - z[..., sl])
    d = jnp.stack(diffs, axis=2)
    y = jnp.einsum('bsgc,gce->bsge', d, w_grp.astype(jnp.float32)).reshape(B, S, POOL_WIDTH)
    y = y * scale.astype(jnp.float32)
    return y.astype(h.dtype) @ w_out


def swiglu(h, w_gate_up, w_down):
    g, u = jnp.split(h @ w_gate_up, 2, axis=-1)
    return (jax.nn.silu(g) * u) @ w_down


def setup_inputs(seed: int = 0) -> dict:
    key = jax.random.key(seed)
    keys = iter(jax.random.split(key, 32))

    def nrm(shape, scale):
        return scale * jax.random.normal(next(keys), shape, jnp.float32)

    D, dh, G = D_MODEL, NSA_HEAD_DIM, NSA_KV_GROUPS
    res = (2 * DEPTH) ** -0.5
    x = jax.random.normal(next(keys), (BATCH, SEQ, D), jnp.float32)
    positions = (jnp.arange(SEQ, dtype=jnp.int32)[None, :]
                 + jax.random.randint(next(keys), (BATCH, 1), 0, 1024, dtype=jnp.int32))
    return {
        'x': x,
        'positions': positions,
        'mix_norm': 1.0 + nrm((DEPTH, D), 0.05),
        'ffn_norm': 1.0 + nrm((DEPTH, D), 0.05),
        'final_norm': 1.0 + nrm((D,), 0.05),
        'nsa_w_in': nrm((N_NSA, D, NSA_IN), D ** -0.5),
        'nsa_cmp_pos_k': nrm((N_NSA, CMP_BLOCK, dh), 0.1),
        'nsa_cmp_k_w1': nrm((N_NSA, CMP_BLOCK * dh, CMP_HIDDEN), (CMP_BLOCK * dh) ** -0.5),
        'nsa_cmp_k_w2': nrm((N_NSA, CMP_HIDDEN, dh), CMP_HIDDEN ** -0.5),
        'nsa_cmp_pos_v': nrm((N_NSA, CMP_BLOCK, dh), 0.1),
        'nsa_cmp_v_w1': nrm((N_NSA, CMP_BLOCK * dh, CMP_HIDDEN), (CMP_BLOCK * dh) ** -0.5),
        'nsa_cmp_v_w2': nrm((N_NSA, CMP_HIDDEN, dh), CMP_HIDDEN ** -0.5),
        'nsa_w_out': nrm((N_NSA, NSA_HEADS * dh, D), res * (NSA_HEADS * dh) ** -0.5),
        'sgu_w_in': nrm((N_SGU, D, 2 * SGU_WIDTH), D ** -0.5),
        'sgu_ln_g': 1.0 + nrm((N_SGU, SGU_WIDTH), 0.05),
        'sgu_ln_b': nrm((N_SGU, SGU_WIDTH), 0.02),
        'sgu_w_s': nrm((N_SGU, SGU_GROUPS, SGU_CHUNK, SGU_CHUNK), SGU_CHUNK ** -0.5),
        'sgu_b_s': 1.0 + nrm((N_SGU, SGU_GROUPS, SGU_CHUNK), 0.1),
        'sgu_w_out': nrm((N_SGU, SGU_WIDTH, D), res * SGU_WIDTH ** -0.5),
        'pool_w_in': nrm((N_POOL, D, POOL_WIDTH), D ** -0.5),
        'pool_w_grp': nrm((N_POOL, POOL_GROUPS, POOL_GROUP_DIM, POOL_GROUP_DIM), POOL_GROUP_DIM ** -0.5),
        'pool_scale': 1.0 + nrm((N_POOL, POOL_WIDTH), 0.1),
        'pool_w_out': nrm((N_POOL, POOL_WIDTH, D), res * POOL_WIDTH ** -0.5),
        'ffn_w_gate_up': nrm((DEPTH, D, 2 * FFN_HIDDEN), D ** -0.5),
        'ffn_w_down': nrm((DEPTH, FFN_HIDDEN, D), res * FFN_HIDDEN ** -0.5),
    }


def reference(x, positions, mix_norm, ffn_norm, final_norm,
              nsa_w_in, nsa_cmp_pos_k, nsa_cmp_k_w1, nsa_cmp_k_w2,
              nsa_cmp_pos_v, nsa_cmp_v_w1, nsa_cmp_v_w2, nsa_w_out,
              sgu_w_in, sgu_ln_g, sgu_ln_b, sgu_w_s, sgu_b_s, sgu_w_out,
              pool_w_in, pool_w_grp, pool_scale, pool_w_out,
              ffn_w_gate_up, ffn_w_down):
    cos, sin = rope_tables(positions)
    for i in range(DEPTH):
        kind, j = i % N_MIXERS, i // N_MIXERS
        h = rmsnorm(x, mix_norm[i])
        if kind == 0:
            m = nsa_mixer(h, cos, sin, nsa_w_in[j], nsa_cmp_pos_k[j], nsa_cmp_k_w1[j], nsa_cmp_k_w2[j],
                          nsa_cmp_pos_v[j], nsa_cmp_v_w1[j], nsa_cmp_v_w2[j], nsa_w_out[j])
        elif kind == 1:
            m = sgu_mixer(h, sgu_w_in[j], sgu_ln_g[j], sgu_ln_b[j], sgu_w_s[j], sgu_b_s[j], sgu_w_out[j])
        else:
            m = pool_mixer(h, pool_w_in[j], pool_w_grp[j], pool_scale[j], pool_w_out[j])
        x = x + m
        x = x + swiglu(rmsnorm(x, ffn_norm[i]), ffn_w_gate_up[i], ffn_w_down[i])
    return rmsnorm(x, final_norm)
```

```python
import functools

import jax
import jax.numpy as jnp
import numpy as np
from jax import lax
from jax.experimental import pallas as pl
from jax.experimental.pallas import tpu as pltpu

RMS_EPS = 1e-6
LN_EPS = 1e-5
NEG_INF = -1e30
FORCE_SCORE = 1e9
PICKED = -3e38

N_MIXERS = 3
NSA_HEADS = 16
NSA_KV_GROUPS = 4
NSA_HPG = NSA_HEADS // NSA_KV_GROUPS
NSA_HEAD_DIM = 64
CMP_BLOCK = 32
CMP_STRIDE = 16
SEL_BLOCK = 64
SEL_TOPN = 16
WINDOW = 512
ROPE_THETA = 500000.0
ROPE_DIM = NSA_HEAD_DIM // 4
SGU_GROUPS = 8
SGU_CHUNK = 128
POOL_WINDOWS = (2, 4, 8, 16)

LANES = 128
VMEM_LIMIT = 56 * 1024 * 1024

BF16 = jnp.bfloat16
F32 = jnp.float32


def _const_spec(shape):
    nd = len(shape)
    return pl.BlockSpec(shape, lambda *_: (0,) * nd, pipeline_mode=pl.Buffered(1))


def _rmsnorm(x, g):
    ms = jnp.mean(x * x, axis=-1, keepdims=True)
    return x * lax.rsqrt(ms + RMS_EPS) * g


def _dot(a, b):
    return jnp.dot(a, b, preferred_element_type=F32)


def _dot_nt(a, b):
    return lax.dot_general(a, b, (((1,), (1,)), ((), ())), preferred_element_type=F32)


def _rope(t, c, s1, s2):
    return t * c + pltpu.roll(t, LANES - ROPE_DIM // 2, axis=1) * s1 + pltpu.roll(t, ROPE_DIM // 2, axis=1) * s2


def _nsa_in_kernel(x_ref, g_ref, w_ref, c_ref, s1_ref, s2_ref,
                   q_ref, kvs_ref, kvw_ref, kcv_ref, gate_ref, *, d_model, n_groups):
    xn = _rmsnorm(x_ref[...], g_ref[...]).astype(BF16)
    y = _dot(xn, w_ref[...])
    c, s1, s2 = c_ref[...], s1_ref[...], s2_ref[...]
    lane = lax.broadcasted_iota(jnp.int32, c.shape, 1)
    khalf = lane < NSA_HEAD_DIM
    ck, s1k, s2k = jnp.where(khalf, c, 1.0), jnp.where(khalf, s1, 0.0), jnp.where(khalf, s2, 0.0)
    scale = NSA_HEAD_DIM ** -0.5
    for j in range(d_model // LANES):
        sl = slice(j * LANES, (j + 1) * LANES)
        q_ref[:, sl] = (_rope(y[:, sl], c, s1, s2) * scale).astype(BF16)
    off = d_model
    for g in range(n_groups):
        sl = slice(off + g * LANES, off + (g + 1) * LANES)
        kvs_ref[g] = _rope(y[:, sl], ck, s1k, s2k).astype(BF16)
    off += n_groups * LANES
    for g in range(n_groups):
        sl = slice(off + g * LANES, off + (g + 1) * LANES)
        kvw_ref[g] = _rope(y[:, sl], ck, s1k, s2k).astype(BF16)
    off += n_groups * LANES
    nkc = n_groups * NSA_HEAD_DIM
    for j in range(nkc // LANES):
        sl = slice(off + j * LANES, off + (j + 1) * LANES)
        kcv_ref[:, j * LANES:(j + 1) * LANES] = _rope(y[:, sl], c, s1, s2)
    off += nkc
    kcv_ref[:, nkc:2 * nkc] = y[:, off:off + nkc]
    off += nkc
    gate_ref[...] = jax.nn.sigmoid(y[:, off:off + LANES])


def _nsa_in(x, g, w_perm, c_tab, s1_tab, s2_tab, *, tm=512):
    B, S, D = x.shape
    G = NSA_KV_GROUPS
    n_out = w_perm.shape[1]
    nkc = G * NSA_HEAD_DIM
    tok = lambda w: pl.BlockSpec((None, tm, w), lambda b, i: (b, i, 0))
    grp = pl.BlockSpec((None, G, tm, LANES), lambda b, i: (b, 0, i, 0))
    return pl.pallas_call(
        functools.partial(_nsa_in_kernel, d_model=D, n_groups=G),
        grid=(B, S // tm),
        in_specs=[tok(D), _const_spec((1, D)), _const_spec((D, n_out)), tok(LANES), tok(LANES), tok(LANES)],
        out_specs=[tok(D), grp, grp, tok(2 * nkc), tok(LANES)],
        out_shape=[jax.ShapeDtypeStruct((B, S, D), BF16),
                   jax.ShapeDtypeStruct((B, G, S, LANES), BF16),
                   jax.ShapeDtypeStruct((B, G, S, LANES), BF16),
                   jax.ShapeDtypeStruct((B, S, 2 * nkc), F32),
                   jax.ShapeDtypeStruct((B, S, LANES), F32)],
        compiler_params=pltpu.CompilerParams(
            dimension_semantics=("parallel", "parallel"), vmem_limit_bytes=VMEM_LIMIT),
        name="nsa_in",
    )(x, g, w_perm, c_tab, s1_tab, s2_tab)


def _nsa_cmp_kernel(xk_ref, xv_ref, pk_ref, pv_ref, wk1_ref, wk2_ref, wv1_ref, wv2_ref, o_ref):
    half = xk_ref.shape[-1]

    def hidden(x_ref, p_ref, w1_ref):
        x = x_ref[...]
        a = _dot((x + p_ref[:, :half]).astype(BF16), w1_ref[:half, :])
        b = _dot((x + p_ref[:, half:]).astype(BF16), w1_ref[half:, :])
        return jax.nn.gelu(a + pltpu.roll(b, b.shape[0] - 1, axis=0)).astype(BF16)

    hk = hidden(xk_ref, pk_ref, wk1_ref)
    hv = hidden(xv_ref, pv_ref, wv1_ref)
    o_ref[...] = (_dot(hk, wk2_ref[...]) + _dot(hv, wv2_ref[...])).astype(BF16)


def _nsa_cmp(xk, xv, pk, pv, wk1, wk2x, wv1, wv2x):
    B, G, NC, W = xk.shape
    blk = pl.BlockSpec((None, None, NC, W), lambda b, g: (b, g, 0, 0))
    return pl.pallas_call(
        _nsa_cmp_kernel,
        grid=(B, G),
        in_specs=[blk, blk, _const_spec(pk.shape), _const_spec(pv.shape), _const_spec(wk1.shape),
                  _const_spec(wk2x.shape), _const_spec(wv1.shape), _const_spec(wv2x.shape)],
        out_specs=pl.BlockSpec((None, None, NC, LANES), lambda b, g: (b, g, 0, 0)),
        out_shape=jax.ShapeDtypeStruct((B, G, NC, LANES), BF16),
        compiler_params=pltpu.CompilerParams(
            dimension_semantics=("parallel", "parallel"), vmem_limit_bytes=VMEM_LIMIT),
        name="nsa_cmp",
    )(xk, xv, pk, pv, wk1, wk2x, wv1, wv2x)


def _nsa_attn_kernel(q_ref, kvc_ref, kvs_ref, kvw_ref, gate_ref, ov_ref, o_ref, *, tq, tk):
    P, dh = NSA_HPG, NSA_HEAD_DIM
    g = pl.program_id(1)
    qs = pl.program_id(2) * tq
    rows = P * tq

    qf = q_ref[...].astype(F32)
    lane_q = lax.broadcasted_iota(jnp.int32, (tq, LANES), 1)
    heads = []
    for p in range(P):
        slab = qf[:, (p // 2) * LANES:(p // 2 + 1) * LANES]
        if p % 2:
            slab = pltpu.roll(slab, LANES - dh, axis=1)
        heads.append(jnp.where(lane_q < dh, slab, 0.0))
    q4 = jnp.concatenate(heads, axis=0).astype(BF16)

    row_i = lax.broadcasted_iota(jnp.int32, (rows, 1), 0)
    t_row = qs + jnp.bitwise_and(row_i, tq - 1)

    kvc = kvc_ref[...]
    nc = kvc.shape[0]
    sc = _dot_nt(q4, kvc)
    n_i = lax.broadcasted_iota(jnp.int32, (rows, nc), 1)
    valid_c = (n_i * CMP_STRIDE + (CMP_BLOCK - 1)) <= t_row
    sc = jnp.where(valid_c, sc, NEG_INF)
    mc = jnp.max(sc, axis=-1, keepdims=True)
    ec = jnp.where(valid_c, jnp.exp(sc - mc), 0.0)
    lc = jnp.sum(ec, axis=-1, keepdims=True)
    pc = ec / jnp.where(lc > 0.0, lc, 1.0)
    o_c = _dot(pc.astype(BF16), kvc)

    psum = pc[0:tq]
    for p in range(1, P):
        psum = psum + pc[p * tq:(p + 1) * tq]
    ov = ov_ref[...]
    hi = psum.astype(BF16)
    r1 = psum - hi.astype(F32)
    mid = r1.astype(BF16)
    lo = (r1 - mid.astype(F32)).astype(BF16)
    imp = _dot(hi, ov) + _dot(mid, ov) + _dot(lo, ov)

    n_sel = kvs_ref.shape[0] // SEL_BLOCK
    m_i = lax.broadcasted_iota(jnp.int32, (tq, LANES), 1)
    t_q = qs + lax.broadcasted_iota(jnp.int32, (tq, 1), 0)
    cur = jnp.right_shift(t_q, 6)
    forced = (m_i == 0) | (m_i == cur) | (m_i == cur - 1)
    imp = jnp.where(forced, FORCE_SCORE, imp)
    imp = jnp.where((m_i * SEL_BLOCK <= t_q) & (m_i < n_sel), imp, NEG_INF)
    m_f = m_i.astype(F32)
    sel = jnp.zeros((tq, LANES), F32)
    for _ in range(min(SEL_TOPN, n_sel)):
        mx = jnp.max(imp, axis=-1, keepdims=True)
        first = jnp.min(jnp.where(imp == mx, m_f, float(LANES)), axis=-1, keepdims=True)
        hit = m_f == first
        sel = jnp.where(hit, 1.0, sel)
        imp = jnp.where(hit, PICKED, imp)
    sel4 = jnp.concatenate([sel.astype(BF16)] * P, axis=0)

    blk_i = lax.broadcasted_iota(jnp.int32, (LANES, tk), 0)
    key_i = lax.broadcasted_iota(jnp.int32, (LANES, tk), 1)
    kpos_l = lax.broadcasted_iota(jnp.int32, (rows, tk), 1)

    def flash_step(kv, ok, carry):
        m, l, acc = carry
        s = jnp.where(ok, _dot_nt(q4, kv), NEG_INF)
        m_new = jnp.maximum(m, jnp.max(s, axis=-1, keepdims=True))
        alpha = jnp.exp(m - m_new)
        p = jnp.exp(s - m_new)
        l = alpha * l + jnp.sum(p, axis=-1, keepdims=True)
        acc = alpha * acc + _dot(p.astype(BF16), kv)
        return m_new, l, acc

    init = (jnp.full((rows, 1), NEG_INF, F32), jnp.zeros((rows, 1), F32), jnp.zeros((rows, LANES), F32))

    def sel_body(j, carry):
        k0 = pl.multiple_of(j * tk, tk)
        kv = kvs_ref[pl.ds(k0, tk), :]
        expand = (blk_i == jnp.right_shift(k0 + key_i, 6)).astype(BF16)
        chosen = _dot(sel4, expand)
        ok = (chosen > 0.5) & ((k0 + kpos_l) <= t_row)
        return flash_step(kv, ok, carry)

    n_kv = (qs + tq + tk - 1) // tk
    _, l_s, acc_s = lax.fori_loop(0, n_kv, sel_body, init)

    def win_body(j, carry):
        k0 = pl.multiple_of(j * tk, tk)
        kv = kvw_ref[pl.ds(k0, tk), :]
        rel = t_row - (k0 + kpos_l)
        ok = (rel >= 0) & (rel < WINDOW)
        return flash_step(kv, ok, carry)

    j_lo = jnp.maximum(qs - WINDOW + 1, 0) // tk
    _, l_w, acc_w = lax.fori_loop(j_lo, n_kv, win_body, init)

    gates = pltpu.roll(gate_ref[...], (LANES - P * g) % LANES, axis=1)
    o_s = acc_s / l_s
    o_w = acc_w / l_w
    outs = []
    for p in range(P):
        r = slice(p * tq, (p + 1) * tq)
        outs.append(gates[:, p:p + 1] * o_c[r]
                    + gates[:, NSA_HEADS + p:NSA_HEADS + p + 1] * o_s[r]
                    + gates[:, 2 * NSA_HEADS + p:2 * NSA_HEADS + p + 1] * o_w[r])
    for pair in range(P // 2):
        even = pltpu.roll(outs[2 * pair], LANES - dh, axis=1)
        o_ref[:, pair * LANES:(pair + 1) * LANES] = jnp.where(lane_q < dh, even, outs[2 * pair + 1]).astype(BF16)


def _nsa_attn(q, kvc, kvs, kvw, gates, ov, *, tq=128, tk=128):
    B, S, D = q.shape
    G, P, dh = NSA_KV_GROUPS, NSA_HPG, NSA_HEAD_DIM
    NC = kvc.shape[2]
    qblk = pl.BlockSpec((None, tq, P * dh), lambda b, g, i: (b, i, g))
    seq = pl.BlockSpec((None, None, S, LANES), lambda b, g, i: (b, g, 0, 0))
    return pl.pallas_call(
        functools.partial(_nsa_attn_kernel, tq=tq, tk=tk),
        grid=(B, G, S // tq),
        in_specs=[qblk,
                  pl.BlockSpec((None, None, NC, LANES), lambda b, g, i: (b, g, 0, 0)),
                  seq, seq,
                  pl.BlockSpec((None, tq, LANES), lambda b, g, i: (b, i, 0)),
                  _const_spec(ov.shape)],
        out_specs=qblk,
        out_shape=jax.ShapeDtypeStruct((B, S, D), BF16),
        compiler_params=pltpu.CompilerParams(
            dimension_semantics=("parallel", "parallel", "arbitrary"), vmem_limit_bytes=VMEM_LIMIT),
        name="nsa_attn",
    )(q, kvc, kvs, kvw, gates, ov)


def _sgu_kernel(x_ref, g_ref, w_ref, lng_ref, lnb_ref, ws_ref, bs_ref, o_ref, *, width):
    T = SGU_CHUNK
    tm = x_ref.shape[0]
    xn = _rmsnorm(x_ref[...], g_ref[...]).astype(BF16)
    y = jax.nn.gelu(_dot(xn, w_ref[...]))
    u, v = y[:, :width], y[:, width:]
    mu = jnp.mean(v, axis=-1, keepdims=True)
    vc = v - mu
    var = jnp.mean(vc * vc, axis=-1, keepdims=True)
    v = (vc * lax.rsqrt(var + LN_EPS) * lng_ref[...] + lnb_ref[...]).astype(BF16)
    r_i = lax.broadcasted_iota(jnp.int32, (T, T), 0)
    c_i = lax.broadcasted_iota(jnp.int32, (T, T), 1)
    gd = width // SGU_GROUPS
    for gi in range(SGU_GROUPS):
        ws = jnp.where(r_i >= c_i, ws_ref[gi], 0.0).astype(BF16)
        cols = slice(gi * gd, (gi + 1) * gd)
        for c in range(tm // T):
            rws = slice(c * T, (c + 1) * T)
            sv = _dot(ws, v[rws, cols]) + bs_ref[gi]
            o_ref[rws, cols] = (u[rws, cols] * sv).astype(BF16)


def _sgu(x2, g, w_in, ln_g, ln_b, w_s, bs_b, *, tm=256):
    N, D = x2.shape
    W = w_in.shape[1] // 2
    tok = lambda w: pl.BlockSpec((tm, w), lambda i: (i, 0))
    return pl.pallas_call(
        functools.partial(_sgu_kernel, width=W),
        grid=(N // tm,),
        in_specs=[tok(D), _const_spec((1, D)), _const_spec(w_in.shape), _const_spec((1, W)), _const_spec((1, W)),
                  _const_spec(w_s.shape), _const_spec(bs_b.shape)],
        out_specs=tok(W),
        out_shape=jax.ShapeDtypeStruct((N, W), BF16),
        compiler_params=pltpu.CompilerParams(dimension_semantics=("parallel",), vmem_limit_bytes=VMEM_LIMIT),
        name="sgu",
    )(x2, g, w_in, ln_g, ln_b, w_s, bs_b)


def _pool_kernel(x_ref, g_ref, w_ref, wg_ref, sc_ref, o_ref, zbuf, *, hist):
    tm = x_ref.shape[0]
    i = pl.program_id(1)

    @pl.when(i == 0)
    def _():
        zbuf[0:hist, :] = jnp.zeros((hist, zbuf.shape[1]), F32)

    xn = _rmsnorm(x_ref[...], g_ref[...]).astype(BF16)
    z = _dot(xn, w_ref[...])
    zbuf[hist:hist + tm, :] = z
    t = i * tm + lax.broadcasted_iota(jnp.int32, (tm, 1), 0)
    gd = z.shape[1] // len(POOL_WINDOWS)
    for gi, w in enumerate(POOL_WINDOWS):
        cols = slice(gi * gd, (gi + 1) * gd)
        win = z[:, cols]
        for k in range(1, w):
            win = win + zbuf[hist - k:hist - k + tm, cols]
        cnt = jnp.minimum(t + 1, w).astype(F32)
        d = win / cnt - z[:, cols]
        y = _dot(d.astype(BF16), wg_ref[gi]) * sc_ref[:, cols]
        o_ref[:, cols] = y.astype(BF16)
    zbuf[0:hist, :] = zbuf[tm:tm + hist, :]


def _pool(x, g, w_in, w_grp, scale, *, tm=256):
    B, S, D = x.shape
    PW = w_in.shape[1]
    hist = 16
    assert hist >= max(POOL_WINDOWS) - 1 and tm >= hist
    tok = lambda w: pl.BlockSpec((None, tm, w), lambda b, i: (b, i, 0))
    return pl.pallas_call(
        functools.partial(_pool_kernel, hist=hist),
        grid=(B, S // tm),
        in_specs=[tok(D), _const_spec((1, D)), _const_spec(w_in.shape), _const_spec(w_grp.shape),
                  _const_spec((1, PW))],
        out_specs=tok(PW),
        out_shape=jax.ShapeDtypeStruct((B, S, PW), BF16),
        scratch_shapes=[pltpu.VMEM((hist + tm, PW), F32)],
        compiler_params=pltpu.CompilerParams(
            dimension_semantics=("arbitrary", "arbitrary"), vmem_limit_bytes=VMEM_LIMIT),
        name="pool",
    )(x, g, w_in, w_grp, scale)


def _post_kernel(x_ref, u_ref, wo_ref, g_ref, wgu_ref, wd_ref, fg_ref, o_ref, *, hidden, final):
    x1 = x_ref[...] + _dot(u_ref[...], wo_ref[...])
    xn = _rmsnorm(x1, g_ref[...]).astype(BF16)
    gate = _dot(xn, wgu_ref[:, :hidden])
    up = _dot(xn, wgu_ref[:, hidden:])
    a = (jax.nn.silu(gate) * up).astype(BF16)
    x2 = x1 + _dot(a, wd_ref[...])
    if final:
        x2 = _rmsnorm(x2, fg_ref[...])
    o_ref[...] = x2


def _post(x2, u2, w_out, g, w_gu, w_down, fg, *, final, tm=256):
    N, D = x2.shape
    hidden = w_down.shape[0]
    tok = lambda w: pl.BlockSpec((tm, w), lambda i: (i, 0))
    return pl.pallas_call(
        functools.partial(_post_kernel, hidden=hidden, final=final),
        grid=(N // tm,),
        in_specs=[tok(D), tok(u2.shape[1]), _const_spec(w_out.shape), _const_spec((1, D)),
                  _const_spec(w_gu.shape), _const_spec(w_down.shape), _const_spec((1, D))],
        out_specs=tok(D),
        out_shape=jax.ShapeDtypeStruct((N, D), F32),
        compiler_params=pltpu.CompilerParams(dimension_semantics=("parallel",), vmem_limit_bytes=VMEM_LIMIT),
        name="post",
    )(x2, u2, w_out, g, w_gu, w_down, fg)


def _rope_tables(positions):
    half = ROPE_DIM // 2
    inv_freq = ROPE_THETA ** (-jnp.arange(0, ROPE_DIM, 2, dtype=F32) / ROPE_DIM)
    ang = positions.astype(F32)[..., None] * inv_freq
    cos, sin = jnp.cos(ang), jnp.sin(ang)
    z = lambda n: jnp.zeros(ang.shape[:-1] + (n,), F32)
    dh = NSA_HEAD_DIM
    c64 = jnp.concatenate([cos, cos, 1.0 + z(dh - ROPE_DIM)], axis=-1)
    s1 = jnp.concatenate([-sin, z(dh - half)], axis=-1)
    s2 = jnp.concatenate([z(half), sin, z(dh - ROPE_DIM)], axis=-1)
    rep = LANES // dh
    return tuple(jnp.tile(t, (1, 1, rep)) for t in (c64, s1, s2))


def _permute_nsa_w_in(w):
    H, G, dh = NSA_HEADS, NSA_KV_GROUPS, NSA_HEAD_DIM
    base = H * dh
    seg = lambda k: w[:, base + k * G * dh: base + (k + 1) * G * dh]
    kc, vc, ks, vs, kw, vw = (seg(k) for k in range(6))
    gl = w[:, base + 6 * G * dh:]
    inter = lambda a, b: jnp.concatenate(
        [a.reshape(-1, G, 1, dh), b.reshape(-1, G, 1, dh)], axis=2).reshape(-1, 2 * G * dh)
    glp = jnp.pad(gl, ((0, 0), (0, LANES - gl.shape[1])))
    return jnp.concatenate([w[:, :base], inter(ks, vs), inter(kw, vw), kc, vc, glp], axis=1).astype(BF16)


def _overlap_matrix(n_chunks, n_sel):
    n = np.arange(n_chunks)[:, None]
    m = np.arange(LANES)[None, :]
    start, end = n * CMP_STRIDE, n * CMP_STRIDE + CMP_BLOCK - 1
    ov = (start <= m * SEL_BLOCK + SEL_BLOCK - 1) & (end >= m * SEL_BLOCK) & (m < n_sel)
    return jnp.asarray(ov, BF16)


def _nsa_layer(x, tabs, g, w_in, pos_k, k_w1, k_w2, pos_v, v_w1, v_w2):
    B, S, D = x.shape
    G, dh = NSA_KV_GROUPS, NSA_HEAD_DIM
    q, kvs, kvw, kcv, gates = _nsa_in(x, g.reshape(1, D), _permute_nsa_w_in(w_in), *tabs)
    n_chunks = S // CMP_STRIDE
    chunks = lambda t: t.reshape(B, n_chunks, CMP_STRIDE, G, dh).transpose(0, 3, 1, 2, 4).reshape(
        B, G, n_chunks, CMP_STRIDE * dh)
    nkc = G * dh
    pad_r = lambda w2: jnp.pad(w2, ((0, 0), (0, LANES - dh))).astype(BF16)
    pad_l = lambda w2: jnp.pad(w2, ((0, 0), (LANES - dh, 0))).astype(BF16)
    kvc = _nsa_cmp(chunks(kcv[..., :nkc]), chunks(kcv[..., nkc:]),
                   pos_k.reshape(1, CMP_BLOCK * dh), pos_v.reshape(1, CMP_BLOCK * dh),
                   k_w1.astype(BF16), pad_r(k_w2), v_w1.astype(BF16), pad_l(v_w2))
    ov = _overlap_matrix(n_chunks, S // SEL_BLOCK)
    return _nsa_attn(q, kvc, kvs, kvw, gates, ov)


def kernel(x, positions, mix_norm, ffn_norm, final_norm, nsa_w_in, nsa_cmp_pos_k, nsa_cmp_k_w1, nsa_cmp_k_w2, nsa_cmp_pos_v, nsa_cmp_v_w1, nsa_cmp_v_w2, nsa_w_out, sgu_w_in, sgu_ln_g, sgu_ln_b, sgu_w_s, sgu_b_s, sgu_w_out, pool_w_in, pool_w_grp, pool_scale, pool_w_out, ffn_w_gate_up, ffn_w_down):
    B, S, D = x.shape
    depth = mix_norm.shape[0]
    tabs = _rope_tables(positions)
    row = lambda v: v.reshape(1, -1)
    for i in range(depth):
        kind, j = i % N_MIXERS, i // N_MIXERS
        if kind == 0:
            u = _nsa_layer(x, tabs, mix_norm[i], nsa_w_in[j], nsa_cmp_pos_k[j], nsa_cmp_k_w1[j], nsa_cmp_k_w2[j],
                           nsa_cmp_pos_v[j], nsa_cmp_v_w1[j], nsa_cmp_v_w2[j])
            w_out = nsa_w_out[j]
        elif kind == 1:
            bs_b = jnp.broadcast_to(sgu_b_s[j][:, :, None], sgu_b_s[j].shape + (LANES,))
            u = _sgu(x.reshape(B * S, D), row(mix_norm[i]), sgu_w_in[j].astype(BF16), row(sgu_ln_g[j]),
                     row(sgu_ln_b[j]), sgu_w_s[j], bs_b)
            w_out = sgu_w_out[j]
        else:
            u = _pool(x, row(mix_norm[i]), pool_w_in[j].astype(BF16), pool_w_grp[j].astype(BF16),
                      row(pool_scale[j]))
            w_out = pool_w_out[j]
        x = _post(x.reshape(B * S, D), u.reshape(B * S, -1), w_out.astype(BF16), row(ffn_norm[i]),
                  ffn_w_gate_up[i].astype(BF16), ffn_w_down[i].astype(BF16), row(final_norm),
                  final=(i == depth - 1)).reshape(B, S, D)
    return x
```

```python
import functools

import jax
import jax.numpy as jnp
import numpy as np
from jax import lax
from jax.experimental import pallas as pl
from jax.experimental.pallas import tpu as pltpu

RMS_EPS = 1e-6
LN_EPS = 1e-5
NEG_INF = -1e30
FORCE_SCORE = 1e9

N_MIXERS = 3
NSA_HEADS = 16
NSA_KV_GROUPS = 4
NSA_HPG = NSA_HEADS // NSA_KV_GROUPS
NSA_HEAD_DIM = 64
CMP_BLOCK = 32
CMP_STRIDE = 16
SEL_BLOCK = 64
SEL_TOPN = 16
WINDOW = 512
ROPE_THETA = 500000.0
ROPE_DIM = NSA_HEAD_DIM // 4
SGU_GROUPS = 8
SGU_CHUNK = 128
POOL_WINDOWS = (2, 4, 8, 16)

LANES = 128
VMEM_LIMIT = 56 * 1024 * 1024

BF16 = jnp.bfloat16
F32 = jnp.float32


def _const_spec(shape):
    nd = len(shape)
    return pl.BlockSpec(shape, lambda *_: (0,) * nd, pipeline_mode=pl.Buffered(1))


def _rmsnorm(x, g):
    ms = jnp.mean(x * x, axis=-1, keepdims=True)
    return x * lax.rsqrt(ms + RMS_EPS) * g


def _dot(a, b):
    return jnp.dot(a, b, preferred_element_type=F32)


VT_ROWS = NSA_HEAD_DIM + 16


def _rope(t, c, s1, s2):
    return t * c + pltpu.roll(t, LANES - ROPE_DIM // 2, axis=1) * s1 + pltpu.roll(t, ROPE_DIM // 2, axis=1) * s2


def _nsa_in_kernel(x_ref, g_ref, w_ref, c_ref, s1_ref, s2_ref,
                   q_ref, ks_ref, vst_ref, kw_ref, vwt_ref, kcv_ref, gate_ref, *, d_model, n_groups):
    tm = x_ref.shape[0]
    dh = NSA_HEAD_DIM
    xn = _rmsnorm(x_ref[...], g_ref[...]).astype(BF16)
    y = _dot(xn, w_ref[...])
    c, s1, s2 = c_ref[...], s1_ref[...], s2_ref[...]
    lane = lax.broadcasted_iota(jnp.int32, c.shape, 1)
    khalf = lane < dh
    ck, s1k, s2k = jnp.where(khalf, c, 1.0), jnp.where(khalf, s1, 0.0), jnp.where(khalf, s2, 0.0)
    scale = dh ** -0.5
    for j in range(d_model // LANES):
        sl = slice(j * LANES, (j + 1) * LANES)
        q_ref[:, sl] = (_rope(y[:, sl], c, s1, s2) * scale).astype(BF16)
    ones = jnp.ones((VT_ROWS - dh, LANES), F32)
    off = d_model
    for k_ref, vt_ref in ((ks_ref, vst_ref), (kw_ref, vwt_ref)):
        for g in range(n_groups):
            slab = _rope(y[:, off + g * LANES: off + (g + 1) * LANES], ck, s1k, s2k)
            k_ref[g] = slab.astype(BF16)
            for ch in range(tm // LANES):
                vt = slab[ch * LANES:(ch + 1) * LANES, :].T[dh:, :]
                vt_ref[g, ch] = jnp.concatenate([vt, ones], axis=0).astype(BF16)
        off += n_groups * LANES
    nkc = n_groups * dh
    for j in range(nkc // LANES):
        sl = slice(off + j * LANES, off + (j + 1) * LANES)
        kcv_ref[:, j * LANES:(j + 1) * LANES] = _rope(y[:, sl], c, s1, s2)
    off += nkc
    kcv_ref[:, nkc:2 * nkc] = y[:, off:off + nkc]
    off += nkc
    gate_ref[...] = jax.nn.sigmoid(y[:, off:off + LANES])


def _nsa_in(x, g, w_perm, c_tab, s1_tab, s2_tab, *, tm=512):
    B, S, D = x.shape
    G = NSA_KV_GROUPS
    n_out = w_perm.shape[1]
    nkc = G * NSA_HEAD_DIM
    tok = lambda w: pl.BlockSpec((None, tm, w), lambda b, i: (b, i, 0))
    grp = pl.BlockSpec((None, G, tm, LANES), lambda b, i: (b, 0, i, 0))
    grp_t = pl.BlockSpec((None, G, tm // LANES, VT_ROWS, LANES), lambda b, i: (b, 0, i, 0, 0))
    k_shape = jax.ShapeDtypeStruct((B, G, S, LANES), BF16)
    vt_shape = jax.ShapeDtypeStruct((B, G, S // LANES, VT_ROWS, LANES), BF16)
    return pl.pallas_call(
        functools.partial(_nsa_in_kernel, d_model=D, n_groups=G),
        grid=(B, S // tm),
        in_specs=[tok(D), _const_spec((1, D)), _const_spec((D, n_out)), tok(LANES), tok(LANES), tok(LANES)],
        out_specs=[tok(D), grp, grp_t, grp, grp_t, tok(2 * nkc), tok(LANES)],
        out_shape=[jax.ShapeDtypeStruct((B, S, D), BF16), k_shape, vt_shape, k_shape, vt_shape,
                   jax.ShapeDtypeStruct((B, S, 2 * nkc), F32),
                   jax.ShapeDtypeStruct((B, S, LANES), F32)],
        compiler_params=pltpu.CompilerParams(
            dimension_semantics=("parallel", "parallel"), vmem_limit_bytes=VMEM_LIMIT),
        name="nsa_in",
    )(x, g, w_perm, c_tab, s1_tab, s2_tab)


def _nsa_cmp_kernel(xk_ref, xv_ref, pk_ref, pv_ref, wk1_ref, wk2_ref, wv1_ref, wv2_ref, kc_ref, vct_ref):
    half = xk_ref.shape[-1]

    def hidden(x_ref, p_ref, w1_ref):
        x = x_ref[...]
        a = _dot((x + p_ref[:, :half]).astype(BF16), w1_ref[:half, :])
        b = _dot((x + p_ref[:, half:]).astype(BF16), w1_ref[half:, :])
        return jax.nn.gelu(a + pltpu.roll(b, b.shape[0] - 1, axis=0)).astype(BF16)

    kv = _dot(hidden(xk_ref, pk_ref, wk1_ref), wk2_ref[...]) + _dot(hidden(xv_ref, pv_ref, wv1_ref), wv2_ref[...])
    kc_ref[...] = kv.astype(BF16)
    vct_ref[...] = kv.T[NSA_HEAD_DIM:, :].astype(BF16)


def _nsa_cmp(xk, xv, pk, pv, wk1, wk2x, wv1, wv2x):
    B, G, NC, W = xk.shape
    blk = pl.BlockSpec((None, None, NC, W), lambda b, g: (b, g, 0, 0))
    return pl.pallas_call(
        _nsa_cmp_kernel,
        grid=(B, G),
        in_specs=[blk, blk, _const_spec(pk.shape), _const_spec(pv.shape), _const_spec(wk1.shape),
                  _const_spec(wk2x.shape), _const_spec(wv1.shape), _const_spec(wv2x.shape)],
        out_specs=[pl.BlockSpec((None, None, NC, LANES), lambda b, g: (b, g, 0, 0)),
                   pl.BlockSpec((None, None, NSA_HEAD_DIM, NC), lambda b, g: (b, g, 0, 0))],
        out_shape=[jax.ShapeDtypeStruct((B, G, NC, LANES), BF16),
                   jax.ShapeDtypeStruct((B, G, NSA_HEAD_DIM, NC), BF16)],
        compiler_params=pltpu.CompilerParams(
            dimension_semantics=("parallel", "parallel"), vmem_limit_bytes=VMEM_LIMIT),
        name="nsa_cmp",
    )(xk, xv, pk, pv, wk1, wk2x, wv1, wv2x)


MASK_NONE, MASK_CAUSAL, MASK_BAND = 0, 1, 2


def _nsa_attn_kernel(q_ref, kc_ref, vct_ref, ks_ref, vst_ref, kw_ref, vwt_ref, gate_ref, ovt_ref, o_ref,
                     q4t, selb, acc, m_ref, osel, gt, *, tq, n_sel):
    P, dh = NSA_HPG, NSA_HEAD_DIM
    g = pl.program_id(1)
    i = pl.program_id(2)
    qs = i * tq
    L = P * tq

    qt = q_ref[...].astype(F32).T
    zpad = jnp.zeros((LANES - dh, tq), F32)
    q4t[...] = jnp.concatenate(
        [jnp.concatenate([qt[p * dh:(p + 1) * dh], zpad], axis=0) for p in range(P)], axis=1).astype(BF16)

    t_lane = qs + jnp.bitwise_and(lax.broadcasted_iota(jnp.int32, (1, L), 1), tq - 1)

    kc = kc_ref[...]
    nc = kc.shape[0]
    sc = _dot(kc, q4t[...])
    n_i = lax.broadcasted_iota(jnp.int32, (nc, L), 0)
    valid_c = (n_i * CMP_STRIDE + (CMP_BLOCK - 1)) <= t_lane
    sc = jnp.where(valid_c, sc, NEG_INF)
    mc = jnp.max(sc, axis=0, keepdims=True)
    ec = jnp.where(valid_c, jnp.exp(sc - mc), 0.0)
    lc = jnp.sum(ec, axis=0, keepdims=True)
    pc = ec * (1.0 / jnp.where(lc > 0.0, lc, 1.0))
    o_c = _dot(vct_ref[...], pc.astype(BF16))

    psum = pc[:, 0:tq]
    for p in range(1, P):
        psum = psum + pc[:, p * tq:(p + 1) * tq]
    ovt = ovt_ref[...]
    hi = psum.astype(BF16)
    r1 = psum - hi.astype(F32)
    mid = r1.astype(BF16)
    lo = (r1 - mid.astype(F32)).astype(BF16)
    imp = (_dot(ovt, hi) + _dot(ovt, mid) + _dot(ovt, lo))[0:n_sel]

    m_i = lax.broadcasted_iota(jnp.int32, (n_sel, tq), 0)
    t_q = qs + lax.broadcasted_iota(jnp.int32, (1, tq), 1)
    cur = jnp.right_shift(t_q, 6)
    forced = (m_i == 0) | (m_i == cur) | (m_i == cur - 1)
    imp = jnp.where(forced, FORCE_SCORE, imp)
    imp = jnp.where(m_i * SEL_BLOCK <= t_q, imp, NEG_INF)
    sub8 = lax.broadcasted_iota(jnp.int32, (8, tq), 0)
    groups = [imp[8 * r:8 * r + 8] for r in range(n_sel // 8)]
    ranks = [jnp.zeros((8, tq), F32) for _ in groups]
    for mp in range(n_sel):
        row = imp[mp:mp + 1, :]
        for r, vr in enumerate(groups):
            if 8 * r > mp:
                beats = row >= vr
            elif 8 * r + 7 < mp:
                beats = row > vr
            else:
                beats = (row > vr) | ((row == vr) & (sub8 + 8 * r > mp))
            ranks[r] = ranks[r] + jnp.where(beats, 1.0, 0.0)
    bias = jnp.where(jnp.concatenate(ranks, axis=0) < float(min(SEL_TOPN, n_sel)), 0.0, NEG_INF)
    selb[...] = jnp.concatenate([bias] * P, axis=1)

    tk = tq
    sub_i = lax.broadcasted_iota(jnp.int32, (SEL_BLOCK, tq), 0)

    def tile(k_ref, vt_ref, k0, mode, biased):
        kt = k_ref[pl.ds(k0, tk), :]
        j0 = k0 // LANES
        vt = jnp.concatenate([vt_ref[j0 + c] for c in range(tk // LANES)], axis=1)
        scores = [_dot(kt, q4t[:, p * tq:(p + 1) * tq]) for p in range(P)]
        for p in range(P):
            cols = slice(p * tq, (p + 1) * tq)
            slabs = []
            for b in range(tk // SEL_BLOCK):
                sb = scores[p][b * SEL_BLOCK:(b + 1) * SEL_BLOCK]
                if biased:
                    sb = sb + selb[pl.ds(k0 // SEL_BLOCK + b, 1), cols]
                if mode != MASK_NONE:
                    kpos = k0 + b * SEL_BLOCK + sub_i
                    keep = (kpos <= t_q) if mode == MASK_CAUSAL else (t_q - kpos < WINDOW)
                    sb = jnp.where(keep, sb, NEG_INF)
                slabs.append(sb)
            top = slabs[0]
            for sb in slabs[1:]:
                top = jnp.maximum(top, sb)
            m_old = m_ref[0:1, cols]
            m_new = jnp.maximum(m_old, jnp.max(top, axis=0, keepdims=True))
            alpha = jnp.exp(m_old - m_new)
            pr = jnp.concatenate([jnp.exp(sb - m_new).astype(BF16) for sb in slabs], axis=0)
            acc[:, cols] = acc[:, cols] * alpha + _dot(vt, pr)
            m_ref[0:1, cols] = m_new

    def reset():
        m_ref[...] = jnp.full(m_ref.shape, NEG_INF, F32)
        acc[...] = jnp.zeros(acc.shape, F32)

    def normalized():
        return acc[0:dh, :] * (1.0 / acc[dh:dh + 1, :])

    reset()

    def sel_body(j, c):
        tile(ks_ref, vst_ref, pl.multiple_of(j * tk, tk), MASK_NONE, True)
        return c

    lax.fori_loop(0, i, sel_body, 0)
    tile(ks_ref, vst_ref, pl.multiple_of(qs, tk), MASK_CAUSAL, True)
    osel[...] = normalized()

    reset()

    @pl.when(qs >= WINDOW)
    def _():
        tile(kw_ref, vwt_ref, pl.multiple_of(qs - WINDOW, tk), MASK_BAND, False)

    def win_body(j, c):
        tile(kw_ref, vwt_ref, pl.multiple_of(j * tk, tk), MASK_NONE, False)
        return c

    lax.fori_loop(jnp.maximum(i - (WINDOW // tk - 1), 0), i, win_body, 0)
    tile(kw_ref, vwt_ref, pl.multiple_of(qs, tk), MASK_CAUSAL, False)
    o_w = normalized()

    gt[...] = gate_ref[...].T
    o_s = osel[...]
    outs = []
    for p in range(P):
        cols = slice(p * tq, (p + 1) * tq)
        hq = NSA_HPG * g + p
        outs.append(gt[pl.ds(hq, 1), :] * o_c[:, cols]
                    + gt[pl.ds(NSA_HEADS + hq, 1), :] * o_s[:, cols]
                    + gt[pl.ds(2 * NSA_HEADS + hq, 1), :] * o_w[:, cols])
    o_ref[...] = jnp.concatenate(outs, axis=0).T.astype(BF16)


def _nsa_attn(q, kc, vct, ks, vst, kw, vwt, gates, ovt, *, tq=256):
    B, S, D = q.shape
    G, P, dh = NSA_KV_GROUPS, NSA_HPG, NSA_HEAD_DIM
    NC = kc.shape[2]
    n_sel = S // SEL_BLOCK
    assert tq % LANES == 0 and tq & (tq - 1) == 0 and WINDOW % tq == 0 and n_sel % 8 == 0 and n_sel <= LANES
    L = P * tq
    qblk = pl.BlockSpec((None, tq, P * dh), lambda b, g, i: (b, i, g))
    seq = pl.BlockSpec((None, None, S, LANES), lambda b, g, i: (b, g, 0, 0))
    seq_t = pl.BlockSpec((None, None, S // LANES, VT_ROWS, LANES), lambda b, g, i: (b, g, 0, 0, 0))
    return pl.pallas_call(
        functools.partial(_nsa_attn_kernel, tq=tq, n_sel=n_sel),
        grid=(B, G, S // tq),
        in_specs=[qblk,
                  pl.BlockSpec((None, None, NC, LANES), lambda b, g, i: (b, g, 0, 0)),
                  pl.BlockSpec((None, None, dh, NC), lambda b, g, i: (b, g, 0, 0)),
                  seq, seq_t, seq, seq_t,
                  pl.BlockSpec((None, tq, LANES), lambda b, g, i: (b, i, 0)),
                  _const_spec(ovt.shape)],
        out_specs=qblk,
        out_shape=jax.ShapeDtypeStruct((B, S, D), BF16),
        scratch_shapes=[pltpu.VMEM((LANES, L), BF16),
                        pltpu.VMEM((n_sel, L), F32),
                        pltpu.VMEM((VT_ROWS, L), F32),
                        pltpu.VMEM((8, L), F32),
                        pltpu.VMEM((dh, L), F32),
                        pltpu.VMEM((LANES, tq), F32)],
        compiler_params=pltpu.CompilerParams(
            dimension_semantics=("parallel", "parallel", "arbitrary"), vmem_limit_bytes=VMEM_LIMIT),
        name="nsa_attn",
    )(q, kc, vct, ks, vst, kw, vwt, gates, ovt)


def _sgu_kernel(x_ref, g_ref, w_ref, lng_ref, lnb_ref, ws_ref, bs_ref, o_ref, *, width):
    T = SGU_CHUNK
    tm = x_ref.shape[0]
    xn = _rmsnorm(x_ref[...], g_ref[...]).astype(BF16)
    y = jax.nn.gelu(_dot(xn, w_ref[...]))
    u, v = y[:, :width], y[:, width:]
    mu = jnp.mean(v, axis=-1, keepdims=True)
    vc = v - mu
    var = jnp.mean(vc * vc, axis=-1, keepdims=True)
    v = (vc * lax.rsqrt(var + LN_EPS) * lng_ref[...] + lnb_ref[...]).astype(BF16)
    r_i = lax.broadcasted_iota(jnp.int32, (T, T), 0)
    c_i = lax.broadcasted_iota(jnp.int32, (T, T), 1)
    gd = width // SGU_GROUPS
    for gi in range(SGU_GROUPS):
        ws = jnp.where(r_i >= c_i, ws_ref[gi], 0.0).astype(BF16)
        cols = slice(gi * gd, (gi + 1) * gd)
        for c in range(tm // T):
            rws = slice(c * T, (c + 1) * T)
            sv = _dot(ws, v[rws, cols]) + bs_ref[gi]
            o_ref[rws, cols] = (u[rws, cols] * sv).astype(BF16)


def _sgu(x2, g, w_in, ln_g, ln_b, w_s, bs_b, *, tm=256):
    N, D = x2.shape
    W = w_in.shape[1] // 2
    tok = lambda w: pl.BlockSpec((tm, w), lambda i: (i, 0))
    return pl.pallas_call(
        functools.partial(_sgu_kernel, width=W),
        grid=(N // tm,),
        in_specs=[tok(D), _const_spec((1, D)), _const_spec(w_in.shape), _const_spec((1, W)), _const_spec((1, W)),
                  _const_spec(w_s.shape), _const_spec(bs_b.shape)],
        out_specs=tok(W),
        out_shape=jax.ShapeDtypeStruct((N, W), BF16),
        compiler_params=pltpu.CompilerParams(dimension_semantics=("parallel",), vmem_limit_bytes=VMEM_LIMIT),
        name="sgu",
    )(x2, g, w_in, ln_g, ln_b, w_s, bs_b)


def _pool_kernel(x_ref, g_ref, w_ref, wg_ref, sc_ref, o_ref, zbuf, *, hist):
    tm = x_ref.shape[0]
    i = pl.program_id(1)

    @pl.when(i == 0)
    def _():
        zbuf[0:hist, :] = jnp.zeros((hist, zbuf.shape[1]), F32)

    xn = _rmsnorm(x_ref[...], g_ref[...]).astype(BF16)
    z = _dot(xn, w_ref[...])
    zbuf[hist:hist + tm, :] = z
    t = i * tm + lax.broadcasted_iota(jnp.int32, (tm, 1), 0)
    gd = z.shape[1] // len(POOL_WINDOWS)
    for gi, w in enumerate(POOL_WINDOWS):
        cols = slice(gi * gd, (gi + 1) * gd)
        win = z[:, cols]
        for k in range(1, w):
            win = win + zbuf[hist - k:hist - k + tm, cols]
        cnt = jnp.minimum(t + 1, w).astype(F32)
        d = win / cnt - z[:, cols]
        y = _dot(d.astype(BF16), wg_ref[gi]) * sc_ref[:, cols]
        o_ref[:, cols] = y.astype(BF16)
    zbuf[0:hist, :] = zbuf[tm:tm + hist, :]


def _pool(x, g, w_in, w_grp, scale, *, tm=256):
    B, S, D = x.shape
    PW = w_in.shape[1]
    hist = 16
    assert hist >= max(POOL_WINDOWS) - 1 and tm >= hist
    tok = lambda w: pl.BlockSpec((None, tm, w), lambda b, i: (b, i, 0))
    return pl.pallas_call(
        functools.partial(_pool_kernel, hist=hist),
        grid=(B, S // tm),
        in_specs=[tok(D), _const_spec((1, D)), _const_spec(w_in.shape), _const_spec(w_grp.shape),
                  _const_spec((1, PW))],
        out_specs=tok(PW),
        out_shape=jax.ShapeDtypeStruct((B, S, PW), BF16),
        scratch_shapes=[pltpu.VMEM((hist + tm, PW), F32)],
        compiler_params=pltpu.CompilerParams(
            dimension_semantics=("arbitrary", "arbitrary"), vmem_limit_bytes=VMEM_LIMIT),
        name="pool",
    )(x, g, w_in, w_grp, scale)


def _post_kernel(x_ref, u_ref, wo_ref, g_ref, wgu_ref, wd_ref, fg_ref, o_ref, *, hidden, final):
    x1 = x_ref[...] + _dot(u_ref[...], wo_ref[...])
    xn = _rmsnorm(x1, g_ref[...]).astype(BF16)
    gate = _dot(xn, wgu_ref[:, :hidden])
    up = _dot(xn, wgu_ref[:, hidden:])
    a = (jax.nn.silu(gate) * up).astype(BF16)
    x2 = x1 + _dot(a, wd_ref[...])
    if final:
        x2 = _rmsnorm(x2, fg_ref[...])
    o_ref[...] = x2


def _post(x2, u2, w_out, g, w_gu, w_down, fg, *, final, tm=256):
    N, D = x2.shape
    hidden = w_down.shape[0]
    tok = lambda w: pl.BlockSpec((tm, w), lambda i: (i, 0))
    return pl.pallas_call(
        functools.partial(_post_kernel, hidden=hidden, final=final),
        grid=(N // tm,),
        in_specs=[tok(D), tok(u2.shape[1]), _const_spec(w_out.shape), _const_spec((1, D)),
                  _const_spec(w_gu.shape), _const_spec(w_down.shape), _const_spec((1, D))],
        out_specs=tok(D),
        out_shape=jax.ShapeDtypeStruct((N, D), F32),
        compiler_params=pltpu.CompilerParams(dimension_semantics=("parallel",), vmem_limit_bytes=VMEM_LIMIT),
        name="post",
    )(x2, u2, w_out, g, w_gu, w_down, fg)


def _rope_tables(positions):
    half = ROPE_DIM // 2
    inv_freq = ROPE_THETA ** (-jnp.arange(0, ROPE_DIM, 2, dtype=F32) / ROPE_DIM)
    ang = positions.astype(F32)[..., None] * inv_freq
    cos, sin = jnp.cos(ang), jnp.sin(ang)
    z = lambda n: jnp.zeros(ang.shape[:-1] + (n,), F32)
    dh = NSA_HEAD_DIM
    c64 = jnp.concatenate([cos, cos, 1.0 + z(dh - ROPE_DIM)], axis=-1)
    s1 = jnp.concatenate([-sin, z(dh - half)], axis=-1)
    s2 = jnp.concatenate([z(half), sin, z(dh - ROPE_DIM)], axis=-1)
    rep = LANES // dh
    return tuple(jnp.tile(t, (1, 1, rep)) for t in (c64, s1, s2))


def _permute_nsa_w_in(w):
    H, G, dh = NSA_HEADS, NSA_KV_GROUPS, NSA_HEAD_DIM
    base = H * dh
    seg = lambda k: w[:, base + k * G * dh: base + (k + 1) * G * dh]
    kc, vc, ks, vs, kw, vw = (seg(k) for k in range(6))
    gl = w[:, base + 6 * G * dh:]
    inter = lambda a, b: jnp.concatenate(
        [a.reshape(-1, G, 1, dh), b.reshape(-1, G, 1, dh)], axis=2).reshape(-1, 2 * G * dh)
    glp = jnp.pad(gl, ((0, 0), (0, LANES - gl.shape[1])))
    return jnp.concatenate([w[:, :base], inter(ks, vs), inter(kw, vw), kc, vc, glp], axis=1).astype(BF16)


def _overlap_matrix_t(n_chunks, n_sel):
    n = np.arange(n_chunks)[None, :]
    m = np.arange(LANES)[:, None]
    start, end = n * CMP_STRIDE, n * CMP_STRIDE + CMP_BLOCK - 1
    ov = (start <= m * SEL_BLOCK + SEL_BLOCK - 1) & (end >= m * SEL_BLOCK) & (m < n_sel)
    return jnp.asarray(ov, BF16)


def _nsa_layer(x, tabs, g, w_in, pos_k, k_w1, k_w2, pos_v, v_w1, v_w2):
    B, S, D = x.shape
    G, dh = NSA_KV_GROUPS, NSA_HEAD_DIM
    q, ks, vst, kw, vwt, kcv, gates = _nsa_in(x, g.reshape(1, D), _permute_nsa_w_in(w_in), *tabs)
    n_chunks = S // CMP_STRIDE
    chunks = lambda t: t.reshape(B, n_chunks, CMP_STRIDE, G, dh).transpose(0, 3, 1, 2, 4).reshape(
        B, G, n_chunks, CMP_STRIDE * dh)
    nkc = G * dh
    pad_r = lambda w2: jnp.pad(w2, ((0, 0), (0, LANES - dh))).astype(BF16)
    pad_l = lambda w2: jnp.pad(w2, ((0, 0), (LANES - dh, 0))).astype(BF16)
    kc, vct = _nsa_cmp(chunks(kcv[..., :nkc]), chunks(kcv[..., nkc:]),
                       pos_k.reshape(1, CMP_BLOCK * dh), pos_v.reshape(1, CMP_BLOCK * dh),
                       k_w1.astype(BF16), pad_r(k_w2), v_w1.astype(BF16), pad_l(v_w2))
    ovt = _overlap_matrix_t(n_chunks, S // SEL_BLOCK)
    return _nsa_attn(q, kc, vct, ks, vst, kw, vwt, gates, ovt)


def kernel(x, positions, mix_norm, ffn_norm, final_norm, nsa_w_in, nsa_cmp_pos_k, nsa_cmp_k_w1, nsa_cmp_k_w2, nsa_cmp_pos_v, nsa_cmp_v_w1, nsa_cmp_v_w2, nsa_w_out, sgu_w_in, sgu_ln_g, sgu_ln_b, sgu_w_s, sgu_b_s, sgu_w_out, pool_w_in, pool_w_grp, pool_scale, pool_w_out, ffn_w_gate_up, ffn_w_down):
    B, S, D = x.shape
    depth = mix_norm.shape[0]
    tabs = _rope_tables(positions)
    row = lambda v: v.reshape(1, -1)
    for i in range(depth):
        kind, j = i % N_MIXERS, i // N_MIXERS
        if kind == 0:
            u = _nsa_layer(x, tabs, mix_norm[i], nsa_w_in[j], nsa_cmp_pos_k[j], nsa_cmp_k_w1[j], nsa_cmp_k_w2[j],
                           nsa_cmp_pos_v[j], nsa_cmp_v_w1[j], nsa_cmp_v_w2[j])
            w_out = nsa_w_out[j]
        elif kind == 1:
            gd = sgu_w_in.shape[-1] // 2 // SGU_GROUPS
            bs_b = jnp.broadcast_to(sgu_b_s[j][:, :, None], sgu_b_s[j].shape + (gd,))
            u = _sgu(x.reshape(B * S, D), row(mix_norm[i]), sgu_w_in[j].astype(BF16), row(sgu_ln_g[j]),
                     row(sgu_ln_b[j]), sgu_w_s[j], bs_b)
            w_out = sgu_w_out[j]
        else:
            u = _pool(x, row(mix_norm[i]), pool_w_in[j].astype(BF16), pool_w_grp[j].astype(BF16),
                      row(pool_scale[j]))
            w_out = pool_w_out[j]
        x = _post(x.reshape(B * S, D), u.reshape(B * S, -1), w_out.astype(BF16), row(ffn_norm[i]),
                  ffn_w_gate_up[i].astype(BF16), ffn_w_down[i].astype(BF16), row(final_norm),
                  final=(i == depth - 1)).reshape(B, S, D)
    return x
```

```python
import functools

import jax
import jax.numpy as jnp
import numpy as np
from jax import lax
from jax.experimental import pallas as pl
from jax.experimental.pallas import tpu as pltpu

RMS_EPS = 1e-6
LN_EPS = 1e-5
NEG_INF = -1e30
FORCE_SCORE = 1e9
LOG2_E = 1.4426950408889634

N_MIXERS = 3
NSA_HEADS = 16
NSA_KV_GROUPS = 4
NSA_HPG = NSA_HEADS // NSA_KV_GROUPS
NSA_HEAD_DIM = 64
CMP_BLOCK = 32
CMP_STRIDE = 16
SEL_BLOCK = 64
SEL_TOPN = 16
WINDOW = 512
ROPE_THETA = 500000.0
ROPE_DIM = NSA_HEAD_DIM // 4
SGU_GROUPS = 8
SGU_CHUNK = 128
POOL_WINDOWS = (2, 4, 8, 16)

LANES = 128
VMEM_LIMIT = 56 * 1024 * 1024

BF16 = jnp.bfloat16
F32 = jnp.float32


def _const_spec(shape):
    nd = len(shape)
    return pl.BlockSpec(shape, lambda *_: (0,) * nd, pipeline_mode=pl.Buffered(1))


def _rmsnorm(x, g):
    ms = jnp.mean(x * x, axis=-1, keepdims=True)
    return x * lax.rsqrt(ms + RMS_EPS) * g


def _dot(a, b):
    return jnp.dot(a, b, preferred_element_type=F32)


VT_ROWS = NSA_HEAD_DIM + 16


def _rope(t, c, s1, s2):
    return t * c + pltpu.roll(t, LANES - ROPE_DIM // 2, axis=1) * s1 + pltpu.roll(t, ROPE_DIM // 2, axis=1) * s2


def _nsa_in_kernel(x_ref, g_ref, w_ref, c_ref, s1_ref, s2_ref,
                   q_ref, ks_ref, vst_ref, kw_ref, vwt_ref, kcv_ref, gate_ref, *, d_model, n_groups):
    tm = x_ref.shape[0]
    dh = NSA_HEAD_DIM
    xn = _rmsnorm(x_ref[...], g_ref[...]).astype(BF16)
    y = _dot(xn, w_ref[...])
    c, s1, s2 = c_ref[...], s1_ref[...], s2_ref[...]
    lane = lax.broadcasted_iota(jnp.int32, c.shape, 1)
    khalf = lane < dh
    ck, s1k, s2k = jnp.where(khalf, c, 1.0), jnp.where(khalf, s1, 0.0), jnp.where(khalf, s2, 0.0)
    scale = dh ** -0.5 * LOG2_E
    for j in range(d_model // LANES):
        sl = slice(j * LANES, (j + 1) * LANES)
        q_ref[:, sl] = (_rope(y[:, sl], c, s1, s2) * scale).astype(BF16)
    ones = jnp.ones((VT_ROWS - dh, LANES), F32)
    tok = pl.program_id(1) * tm + lax.broadcasted_iota(jnp.int32, c.shape, 0)
    blk_onehot = jnp.where(lane - dh == tok // SEL_BLOCK, 1.0, 0.0)
    off = d_model
    for k_ref, vt_ref, tag_blocks in ((ks_ref, vst_ref, True), (kw_ref, vwt_ref, False)):
        for g in range(n_groups):
            slab = _rope(y[:, off + g * LANES: off + (g + 1) * LANES], ck, s1k, s2k)
            k_ref[g] = (jnp.where(khalf, slab, blk_onehot) if tag_blocks else slab).astype(BF16)
            for ch in range(tm // LANES):
                vt = slab[ch * LANES:(ch + 1) * LANES, :].T[dh:, :]
                vt_ref[g, ch] = jnp.concatenate([vt, ones], axis=0).astype(BF16)
        off += n_groups * LANES
    nkc = n_groups * dh
    for j in range(nkc // LANES):
        sl = slice(off + j * LANES, off + (j + 1) * LANES)
        kcv_ref[:, j * LANES:(j + 1) * LANES] = _rope(y[:, sl], c, s1, s2)
    off += nkc
    kcv_ref[:, nkc:2 * nkc] = y[:, off:off + nkc]
    off += nkc
    gate_ref[...] = jax.nn.sigmoid(y[:, off:off + LANES])


def _nsa_in(x, g, w_perm, c_tab, s1_tab, s2_tab, *, tm=512):
    B, S, D = x.shape
    G = NSA_KV_GROUPS
    n_out = w_perm.shape[1]
    nkc = G * NSA_HEAD_DIM
    tok = lambda w: pl.BlockSpec((None, tm, w), lambda b, i: (b, i, 0))
    grp = pl.BlockSpec((None, G, tm, LANES), lambda b, i: (b, 0, i, 0))
    grp_t = pl.BlockSpec((None, G, tm // LANES, VT_ROWS, LANES), lambda b, i: (b, 0, i, 0, 0))
    k_shape = jax.ShapeDtypeStruct((B, G, S, LANES), BF16)
    vt_shape = jax.ShapeDtypeStruct((B, G, S // LANES, VT_ROWS, LANES), BF16)
    return pl.pallas_call(
        functools.partial(_nsa_in_kernel, d_model=D, n_groups=G),
        grid=(B, S // tm),
        in_specs=[tok(D), _const_spec((1, D)), _const_spec((D, n_out)), tok(LANES), tok(LANES), tok(LANES)],
        out_specs=[tok(D), grp, grp_t, grp, grp_t, tok(2 * nkc), tok(LANES)],
        out_shape=[jax.ShapeDtypeStruct((B, S, D), BF16), k_shape, vt_shape, k_shape, vt_shape,
                   jax.ShapeDtypeStruct((B, S, 2 * nkc), F32),
                   jax.ShapeDtypeStruct((B, S, LANES), F32)],
        compiler_params=pltpu.CompilerParams(
            dimension_semantics=("parallel", "parallel"), vmem_limit_bytes=VMEM_LIMIT),
        name="nsa_in",
    )(x, g, w_perm, c_tab, s1_tab, s2_tab)


def _nsa_cmp_kernel(xk_ref, xv_ref, pk_ref, pv_ref, wk1_ref, wk2_ref, wv1_ref, wv2_ref, kc_ref, vct_ref):
    half = xk_ref.shape[-1]

    def hidden(x_ref, p_ref, w1_ref):
        x = x_ref[...]
        a = _dot((x + p_ref[:, :half]).astype(BF16), w1_ref[:half, :])
        b = _dot((x + p_ref[:, half:]).astype(BF16), w1_ref[half:, :])
        return jax.nn.gelu(a + pltpu.roll(b, b.shape[0] - 1, axis=0)).astype(BF16)

    kv = _dot(hidden(xk_ref, pk_ref, wk1_ref), wk2_ref[...]) + _dot(hidden(xv_ref, pv_ref, wv1_ref), wv2_ref[...])
    kc_ref[...] = kv.astype(BF16)
    vct_ref[...] = kv.T[NSA_HEAD_DIM:, :].astype(BF16)


def _nsa_cmp(xk, xv, pk, pv, wk1, wk2x, wv1, wv2x):
    B, G, NC, W = xk.shape
    blk = pl.BlockSpec((None, None, NC, W), lambda b, g: (b, g, 0, 0))
    return pl.pallas_call(
        _nsa_cmp_kernel,
        grid=(B, G),
        in_specs=[blk, blk, _const_spec(pk.shape), _const_spec(pv.shape), _const_spec(wk1.shape),
                  _const_spec(wk2x.shape), _const_spec(wv1.shape), _const_spec(wv2x.shape)],
        out_specs=[pl.BlockSpec((None, None, NC, LANES), lambda b, g: (b, g, 0, 0)),
                   pl.BlockSpec((None, None, NSA_HEAD_DIM, NC), lambda b, g: (b, g, 0, 0))],
        out_shape=[jax.ShapeDtypeStruct((B, G, NC, LANES), BF16),
                   jax.ShapeDtypeStruct((B, G, NSA_HEAD_DIM, NC), BF16)],
        compiler_params=pltpu.CompilerParams(
            dimension_semantics=("parallel", "parallel"), vmem_limit_bytes=VMEM_LIMIT),
        name="nsa_cmp",
    )(xk, xv, pk, pv, wk1, wk2x, wv1, wv2x)


def _nsa_attn_kernel(q_ref, kc_ref, vct_ref, ks_ref, vst_ref, kw_ref, vwt_ref, gate_ref, ovt_ref, o_ref,
                     q4w, q4s, acc_s, m_s, acc_w, m_w, gt, *, tq, n_sel):
    P, dh = NSA_HPG, NSA_HEAD_DIM
    g = pl.program_id(1)
    i = pl.program_id(2)
    qs = i * tq
    head_cols = [slice(p * tq, (p + 1) * tq) for p in range(P)]

    qt = q_ref[...].astype(F32).T
    zpad = jnp.zeros((LANES - dh, tq), F32)
    for p in range(P):
        qp = qt[p * dh:(p + 1) * dh].astype(BF16)
        q4w[0:dh, head_cols[p]] = qp
        q4w[dh:, head_cols[p]] = zpad.astype(BF16)
        q4s[0:dh, head_cols[p]] = qp

    t_q = qs + lax.broadcasted_iota(jnp.int32, (1, tq), 1)

    kc = kc_ref[...]
    nc = kc.shape[0]
    n_i = lax.broadcasted_iota(jnp.int32, (nc, tq), 0)
    cbias = jnp.where(n_i * CMP_STRIDE + (CMP_BLOCK - 1) <= t_q, 0.0, NEG_INF)
    has_c = t_q >= CMP_BLOCK - 1
    vct = vct_ref[...]
    sc_all = [_dot(kc, q4w[:, cols]) for cols in head_cols]
    pcs = []
    for sc in sc_all:
        sc = sc + cbias
        ec = jnp.exp2(sc - jnp.max(sc, axis=0, keepdims=True))
        pcs.append(ec * jnp.where(has_c, 1.0 / jnp.sum(ec, axis=0, keepdims=True), 0.0))
    o_c = [_dot(vct, pc.astype(BF16)) for pc in pcs]
    psum = (pcs[0] + pcs[1]) + (pcs[2] + pcs[3])

    ovt = ovt_ref[...]
    hi = psum.astype(BF16)
    r1 = psum - hi.astype(F32)
    mid = r1.astype(BF16)
    lo = (r1 - mid.astype(F32)).astype(BF16)
    imp = (_dot(ovt, hi) + _dot(ovt, mid) + _dot(ovt, lo))[0:n_sel]

    m_i = lax.broadcasted_iota(jnp.int32, (n_sel, tq), 0)
    cur = jnp.right_shift(t_q, 6)
    forced = (m_i == 0) | (m_i == cur) | (m_i == cur - 1)
    imp = jnp.where(forced, FORCE_SCORE, imp)
    imp = jnp.where(m_i * SEL_BLOCK <= t_q, imp, NEG_INF)
    sub8 = lax.broadcasted_iota(jnp.int32, (8, tq), 0)
    groups = [imp[8 * r:8 * r + 8] for r in range(n_sel // 8)]
    ranks = [jnp.zeros((8, tq), F32) for _ in groups]
    for mp in range(n_sel):
        row = imp[mp:mp + 1, :]
        for r, vr in enumerate(groups):
            if 8 * r > mp:
                beats = row >= vr
            elif 8 * r + 7 < mp:
                beats = row > vr
            else:
                beats = (row > vr) | ((row == vr) & (sub8 + 8 * r > mp))
            ranks[r] = ranks[r] + jnp.where(beats, 1.0, 0.0)
    sel_bias = jnp.where(jnp.concatenate(ranks, axis=0) < float(min(SEL_TOPN, n_sel)), 0.0, NEG_INF)
    if n_sel < LANES - dh:
        sel_bias = jnp.concatenate([sel_bias, jnp.zeros((LANES - dh - n_sel, tq), F32)], axis=0)
    for cols in head_cols:
        q4s[dh:, cols] = sel_bias.astype(BF16)

    tk = tq
    n_pre = WINDOW // tk
    sub_i = lax.broadcasted_iota(jnp.int32, (SEL_BLOCK, tq), 0)

    def scores_of(k_ref, k0, q4):
        kt = k_ref[pl.ds(k0, tk), :]
        return [_dot(kt, q4[:, cols]) for cols in head_cols]

    def absorb(scores, vt_ref, k0, acc, m_ref, keep=None, shift=None):
        j0 = k0 // LANES
        vt = jnp.concatenate([vt_ref[j0 + c] for c in range(tk // LANES)], axis=1)
        for p, cols in enumerate(head_cols):
            slabs = []
            for b in range(tk // SEL_BLOCK):
                sb = scores[p][b * SEL_BLOCK:(b + 1) * SEL_BLOCK]
                if shift is not None:
                    sb = sb + shift
                if keep is not None:
                    sb = jnp.where(keep(k0 + b * SEL_BLOCK + sub_i), sb, NEG_INF)
                slabs.append(sb)
            top = slabs[0]
            for sb in slabs[1:]:
                top = jnp.maximum(top, sb)
            m_old = m_ref[0:1, cols]
            m_new = jnp.maximum(m_old, jnp.max(top, axis=0, keepdims=True))
            alpha = jnp.exp2(m_old - m_new)
            pr = jnp.concatenate([jnp.exp2(sb - m_new).astype(BF16) for sb in slabs], axis=0)
            acc[:, cols] = acc[:, cols] * alpha + _dot(vt, pr)
            m_ref[0:1, cols] = m_new

    for m_ref, acc in ((m_s, acc_s), (m_w, acc_w)):
        m_ref[...] = jnp.full(m_ref.shape, NEG_INF, F32)
        acc[...] = jnp.zeros(acc.shape, F32)

    def sel_tiles(k0s):
        sc = [scores_of(ks_ref, k0, q4s) for k0 in k0s]
        for k0, s in zip(k0s, sc):
            absorb(s, vst_ref, k0, acc_s, m_s)

    unroll = 4

    def main_body(a, c):
        sel_tiles([pl.multiple_of((a * unroll + u) * tk, tk) for u in range(unroll)])
        return c

    lax.fori_loop(0, i // unroll, main_body, 0)
    done = (i // unroll) * unroll
    rem = i - done

    @pl.when(jnp.bitwise_and(rem, 2) != 0)
    def _():
        sel_tiles([pl.multiple_of((done + u) * tk, tk) for u in range(2)])

    @pl.when(jnp.bitwise_and(rem, 1) != 0)
    def _():
        sel_tiles([pl.multiple_of((done + jnp.bitwise_and(rem, 2)) * tk, tk)])

    k_diag = pl.multiple_of(qs, tk)
    s_diag = scores_of(ks_ref, k_diag, q4s)
    win = []
    for d in range(n_pre, -1, -1):
        j = i - d
        k0 = pl.multiple_of(jnp.maximum(j, 0) * tk, tk)
        win.append((d, j, k0, scores_of(kw_ref, k0, q4w)))
    causal = lambda kpos: kpos <= t_q
    absorb(s_diag, vst_ref, k_diag, acc_s, m_s, keep=causal)
    for d, j, k0, s_win in win:
        if d == n_pre:
            lowest = t_q - WINDOW + jnp.where(j >= 0, 0, 1 << 30)
            absorb(s_win, vwt_ref, k0, acc_w, m_w, keep=lambda kpos: kpos > lowest)
        elif d == 0:
            absorb(s_win, vwt_ref, k0, acc_w, m_w, keep=causal)
        else:
            absorb(s_win, vwt_ref, k0, acc_w, m_w, shift=jnp.where(j >= 0, 0.0, NEG_INF))

    o_s = acc_s[0:dh, :] * (1.0 / acc_s[dh:dh + 1, :])
    o_w = acc_w[0:dh, :] * (1.0 / acc_w[dh:dh + 1, :])

    gt[...] = gate_ref[...].T
    outs = []
    for p, cols in enumerate(head_cols):
        hq = NSA_HPG * g + p
        outs.append(gt[pl.ds(hq, 1), :] * o_c[p]
                    + gt[pl.ds(NSA_HEADS + hq, 1), :] * o_s[:, cols]
                    + gt[pl.ds(2 * NSA_HEADS + hq, 1), :] * o_w[:, cols])
    o_ref[...] = jnp.concatenate(outs, axis=0).T.astype(BF16)


def _nsa_attn(q, kc, vct, ks, vst, kw, vwt, gates, ovt, *, tq=256):
    B, S, D = q.shape
    G, P, dh = NSA_KV_GROUPS, NSA_HPG, NSA_HEAD_DIM
    NC = kc.shape[2]
    n_sel = S // SEL_BLOCK
    assert tq % LANES == 0 and tq & (tq - 1) == 0 and WINDOW % tq == 0 and n_sel % 16 == 0 and n_sel <= LANES - dh
    L = P * tq
    qblk = pl.BlockSpec((None, tq, P * dh), lambda b, g, i: (b, i, g))
    seq = pl.BlockSpec((None, None, S, LANES), lambda b, g, i: (b, g, 0, 0))
    seq_t = pl.BlockSpec((None, None, S // LANES, VT_ROWS, LANES), lambda b, g, i: (b, g, 0, 0, 0))
    return pl.pallas_call(
        functools.partial(_nsa_attn_kernel, tq=tq, n_sel=n_sel),
        grid=(B, G, S // tq),
        in_specs=[qblk,
                  pl.BlockSpec((None, None, NC, LANES), lambda b, g, i: (b, g, 0, 0)),
                  pl.BlockSpec((None, None, dh, NC), lambda b, g, i: (b, g, 0, 0)),
                  seq, seq_t, seq, seq_t,
                  pl.BlockSpec((None, tq, LANES), lambda b, g, i: (b, i, 0)),
                  _const_spec(ovt.shape)],
        out_specs=qblk,
        out_shape=jax.ShapeDtypeStruct((B, S, D), BF16),
        scratch_shapes=[pltpu.VMEM((LANES, L), BF16),
                        pltpu.VMEM((LANES, L), BF16),
                        pltpu.VMEM((VT_ROWS, L), F32),
                        pltpu.VMEM((8, L), F32),
                        pltpu.VMEM((VT_ROWS, L), F32),
                        pltpu.VMEM((8, L), F32),
                        pltpu.VMEM((LANES, tq), F32)],
        compiler_params=pltpu.CompilerParams(
            dimension_semantics=("parallel", "parallel", "arbitrary"), vmem_limit_bytes=VMEM_LIMIT),
        name="nsa_attn",
    )(q, kc, vct, ks, vst, kw, vwt, gates, ovt)


def _sgu_kernel(x_ref, g_ref, w_ref, lng_ref, lnb_ref, ws_ref, bs_ref, o_ref, *, width):
    T = SGU_CHUNK
    tm = x_ref.shape[0]
    xn = _rmsnorm(x_ref[...], g_ref[...]).astype(BF16)
    y = jax.nn.gelu(_dot(xn, w_ref[...]))
    u, v = y[:, :width], y[:, width:]
    mu = jnp.mean(v, axis=-1, keepdims=True)
    vc = v - mu
    var = jnp.mean(vc * vc, axis=-1, keepdims=True)
    v = (vc * lax.rsqrt(var + LN_EPS) * lng_ref[...] + lnb_ref[...]).astype(BF16)
    r_i = lax.broadcasted_iota(jnp.int32, (T, T), 0)
    c_i = lax.broadcasted_iota(jnp.int32, (T, T), 1)
    gd = width // SGU_GROUPS
    for gi in range(SGU_GROUPS):
        ws = jnp.where(r_i >= c_i, ws_ref[gi], 0.0).astype(BF16)
        cols = slice(gi * gd, (gi + 1) * gd)
        for c in range(tm // T):
            rws = slice(c * T, (c + 1) * T)
            sv = _dot(ws, v[rws, cols]) + bs_ref[gi]
            o_ref[rws, cols] = (u[rws, cols] * sv).astype(BF16)


def _sgu(x2, g, w_in, ln_g, ln_b, w_s, bs_b, *, tm=256):
    N, D = x2.shape
    W = w_in.shape[1] // 2
    tok = lambda w: pl.BlockSpec((tm, w), lambda i: (i, 0))
    return pl.pallas_call(
        functools.partial(_sgu_kernel, width=W),
        grid=(N // tm,),
        in_specs=[tok(D), _const_spec((1, D)), _const_spec(w_in.shape), _const_spec((1, W)), _const_spec((1, W)),
                  _const_spec(w_s.shape), _const_spec(bs_b.shape)],
        out_specs=tok(W),
        out_shape=jax.ShapeDtypeStruct((N, W), BF16),
        compiler_params=pltpu.CompilerParams(dimension_semantics=("parallel",), vmem_limit_bytes=VMEM_LIMIT),
        name="sgu",
    )(x2, g, w_in, ln_g, ln_b, w_s, bs_b)


def _pool_kernel(x_ref, g_ref, w_ref, wg_ref, sc_ref, o_ref, zbuf, *, hist):
    tm = x_ref.shape[0]
    i = pl.program_id(1)

    @pl.when(i == 0)
    def _():
        zbuf[0:hist, :] = jnp.zeros((hist, zbuf.shape[1]), F32)

    xn = _rmsnorm(x_ref[...], g_ref[...]).astype(BF16)
    z = _dot(xn, w_ref[...])
    zbuf[hist:hist + tm, :] = z
    t = i * tm + lax.broadcasted_iota(jnp.int32, (tm, 1), 0)
    gd = z.shape[1] // len(POOL_WINDOWS)
    for gi, w in enumerate(POOL_WINDOWS):
        cols = slice(gi * gd, (gi + 1) * gd)
        win = z[:, cols]
        for k in range(1, w):
            win = win + zbuf[hist - k:hist - k + tm, cols]
        cnt = jnp.minimum(t + 1, w).astype(F32)
        d = win / cnt - z[:, cols]
        y = _dot(d.astype(BF16), wg_ref[gi]) * sc_ref[:, cols]
        o_ref[:, cols] = y.astype(BF16)
    zbuf[0:hist, :] = zbuf[tm:tm + hist, :]


def _pool(x, g, w_in, w_grp, scale, *, tm=256):
    B, S, D = x.shape
    PW = w_in.shape[1]
    hist = 16
    assert hist >= max(POOL_WINDOWS) - 1 and tm >= hist
    tok = lambda w: pl.BlockSpec((None, tm, w), lambda b, i: (b, i, 0))
    return pl.pallas_call(
        functools.partial(_pool_kernel, hist=hist),
        grid=(B, S // tm),
        in_specs=[tok(D), _const_spec((1, D)), _const_spec(w_in.shape), _const_spec(w_grp.shape),
                  _const_spec((1, PW))],
        out_specs=tok(PW),
        out_shape=jax.ShapeDtypeStruct((B, S, PW), BF16),
        scratch_shapes=[pltpu.VMEM((hist + tm, PW), F32)],
        compiler_params=pltpu.CompilerParams(
            dimension_semantics=("arbitrary", "arbitrary"), vmem_limit_bytes=VMEM_LIMIT),
        name="pool",
    )(x, g, w_in, w_grp, scale)


def _post_kernel(x_ref, u_ref, wo_ref, g_ref, wgu_ref, wd_ref, fg_ref, o_ref, *, hidden, final):
    x1 = x_ref[...] + _dot(u_ref[...], wo_ref[...])
    xn = _rmsnorm(x1, g_ref[...]).astype(BF16)
    gate = _dot(xn, wgu_ref[:, :hidden])
    up = _dot(xn, wgu_ref[:, hidden:])
    a = (jax.nn.silu(gate) * up).astype(BF16)
    x2 = x1 + _dot(a, wd_ref[...])
    if final:
        x2 = _rmsnorm(x2, fg_ref[...])
    o_ref[...] = x2


def _post(x2, u2, w_out, g, w_gu, w_down, fg, *, final, tm=256):
    N, D = x2.shape
    hidden = w_down.shape[0]
    tok = lambda w: pl.BlockSpec((tm, w), lambda i: (i, 0))
    return pl.pallas_call(
        functools.partial(_post_kernel, hidden=hidden, final=final),
        grid=(N // tm,),
        in_specs=[tok(D), tok(u2.shape[1]), _const_spec(w_out.shape), _const_spec((1, D)),
                  _const_spec(w_gu.shape), _const_spec(w_down.shape), _const_spec((1, D))],
        out_specs=tok(D),
        out_shape=jax.ShapeDtypeStruct((N, D), F32),
        compiler_params=pltpu.CompilerParams(dimension_semantics=("parallel",), vmem_limit_bytes=VMEM_LIMIT),
        name="post",
    )(x2, u2, w_out, g, w_gu, w_down, fg)


def _rope_tables(positions):
    half = ROPE_DIM // 2
    inv_freq = ROPE_THETA ** (-jnp.arange(0, ROPE_DIM, 2, dtype=F32) / ROPE_DIM)
    ang = positions.astype(F32)[..., None] * inv_freq
    cos, sin = jnp.cos(ang), jnp.sin(ang)
    z = lambda n: jnp.zeros(ang.shape[:-1] + (n,), F32)
    dh = NSA_HEAD_DIM
    c64 = jnp.concatenate([cos, cos, 1.0 + z(dh - ROPE_DIM)], axis=-1)
    s1 = jnp.concatenate([-sin, z(dh - half)], axis=-1)
    s2 = jnp.concatenate([z(half), sin, z(dh - ROPE_DIM)], axis=-1)
    rep = LANES // dh
    return tuple(jnp.tile(t, (1, 1, rep)) for t in (c64, s1, s2))


def _permute_nsa_w_in(w):
    H, G, dh = NSA_HEADS, NSA_KV_GROUPS, NSA_HEAD_DIM
    base = H * dh
    seg = lambda k: w[:, base + k * G * dh: base + (k + 1) * G * dh]
    kc, vc, ks, vs, kw, vw = (seg(k) for k in range(6))
    gl = w[:, base + 6 * G * dh:]
    inter = lambda a, b: jnp.concatenate(
        [a.reshape(-1, G, 1, dh), b.reshape(-1, G, 1, dh)], axis=2).reshape(-1, 2 * G * dh)
    glp = jnp.pad(gl, ((0, 0), (0, LANES - gl.shape[1])))
    return jnp.concatenate([w[:, :base], inter(ks, vs), inter(kw, vw), kc, vc, glp], axis=1).astype(BF16)


def _overlap_matrix_t(n_chunks, n_sel):
    n = np.arange(n_chunks)[None, :]
    m = np.arange(LANES)[:, None]
    start, end = n * CMP_STRIDE, n * CMP_STRIDE + CMP_BLOCK - 1
    ov = (start <= m * SEL_BLOCK + SEL_BLOCK - 1) & (end >= m * SEL_BLOCK) & (m < n_sel)
    return jnp.asarray(ov, BF16)


def _nsa_layer(x, tabs, g, w_in, pos_k, k_w1, k_w2, pos_v, v_w1, v_w2):
    B, S, D = x.shape
    G, dh = NSA_KV_GROUPS, NSA_HEAD_DIM
    q, ks, vst, kw, vwt, kcv, gates = _nsa_in(x, g.reshape(1, D), _permute_nsa_w_in(w_in), *tabs)
    n_chunks = S // CMP_STRIDE
    chunks = lambda t: t.reshape(B, n_chunks, CMP_STRIDE, G, dh).transpose(0, 3, 1, 2, 4).reshape(
        B, G, n_chunks, CMP_STRIDE * dh)
    nkc = G * dh
    pad_r = lambda w2: jnp.pad(w2, ((0, 0), (0, LANES - dh))).astype(BF16)
    pad_l = lambda w2: jnp.pad(w2, ((0, 0), (LANES - dh, 0))).astype(BF16)
    kc, vct = _nsa_cmp(chunks(kcv[..., :nkc]), chunks(kcv[..., nkc:]),
                       pos_k.reshape(1, CMP_BLOCK * dh), pos_v.reshape(1, CMP_BLOCK * dh),
                       k_w1.astype(BF16), pad_r(k_w2), v_w1.astype(BF16), pad_l(v_w2))
    ovt = _overlap_matrix_t(n_chunks, S // SEL_BLOCK)
    return _nsa_attn(q, kc, vct, ks, vst, kw, vwt, gates, ovt)


def kernel(x, positions, mix_norm, ffn_norm, final_norm, nsa_w_in, nsa_cmp_pos_k, nsa_cmp_k_w1, nsa_cmp_k_w2, nsa_cmp_pos_v, nsa_cmp_v_w1, nsa_cmp_v_w2, nsa_w_out, sgu_w_in, sgu_ln_g, sgu_ln_b, sgu_w_s, sgu_b_s, sgu_w_out, pool_w_in, pool_w_grp, pool_scale, pool_w_out, ffn_w_gate_up, ffn_w_down):
    B, S, D = x.shape
    depth = mix_norm.shape[0]
    tabs = _rope_tables(positions)
    row = lambda v: v.reshape(1, -1)
    for i in range(depth):
        kind, j = i % N_MIXERS, i // N_MIXERS
        if kind == 0:
            u = _nsa_layer(x, tabs, mix_norm[i], nsa_w_in[j], nsa_cmp_pos_k[j], nsa_cmp_k_w1[j], nsa_cmp_k_w2[j],
                           nsa_cmp_pos_v[j], nsa_cmp_v_w1[j], nsa_cmp_v_w2[j])
            w_out = nsa_w_out[j]
        elif kind == 1:
            gd = sgu_w_in.shape[-1] // 2 // SGU_GROUPS
            bs_b = jnp.broadcast_to(sgu_b_s[j][:, :, None], sgu_b_s[j].shape + (gd,))
            u = _sgu(x.reshape(B * S, D), row(mix_norm[i]), sgu_w_in[j].astype(BF16), row(sgu_ln_g[j]),
                     row(sgu_ln_b[j]), sgu_w_s[j], bs_b)
            w_out = sgu_w_out[j]
        else:
            u = _pool(x, row(mix_norm[i]), pool_w_in[j].astype(BF16), pool_w_grp[j].astype(BF16),
                      row(pool_scale[j]))
            w_out = pool_w_out[j]
        x = _post(x.reshape(B * S, D), u.reshape(B * S, -1), w_out.astype(BF16), row(ffn_norm[i]),
                  ffn_w_gate_up[i].astype(BF16), ffn_w_down[i].astype(BF16), row(final_norm),
                  final=(i == depth - 1)).reshape(B, S, D)
    return x
```

```python
import functools

import jax
import jax.numpy as jnp
import numpy as np
from jax import lax
from jax.experimental import pallas as pl
from jax.experimental.pallas import tpu as pltpu

RMS_EPS = 1e-6
LN_EPS = 1e-5
NEG_INF = -1e30
FORCE_SCORE = 1e9
LOG2_E = 1.4426950408889634

N_MIXERS = 3
NSA_HEADS = 16
NSA_KV_GROUPS = 4
NSA_HPG = NSA_HEADS // NSA_KV_GROUPS
NSA_HEAD_DIM = 64
CMP_BLOCK = 32
CMP_STRIDE = 16
SEL_BLOCK = 64
SEL_TOPN = 16
WINDOW = 512
ROPE_THETA = 500000.0
ROPE_DIM = NSA_HEAD_DIM // 4
SGU_GROUPS = 8
SGU_CHUNK = 128
POOL_WINDOWS = (2, 4, 8, 16)

LANES = 128
VMEM_LIMIT = 56 * 1024 * 1024

BF16 = jnp.bfloat16
F32 = jnp.float32


def _const_spec(shape):
    nd = len(shape)
    return pl.BlockSpec(shape, lambda *_: (0,) * nd, pipeline_mode=pl.Buffered(1))


def _rmsnorm(x, g):
    ms = jnp.mean(x * x, axis=-1, keepdims=True)
    return x * lax.rsqrt(ms + RMS_EPS) * g


def _dot(a, b):
    return jnp.dot(a, b, preferred_element_type=F32)


VT_ROWS = NSA_HEAD_DIM + 16


def _rope(t, c, s1, s2):
    return t * c + pltpu.roll(t, LANES - ROPE_DIM // 2, axis=1) * s1 + pltpu.roll(t, ROPE_DIM // 2, axis=1) * s2


def _nsa_in_kernel(x_ref, g_ref, w_ref, c_ref, s1_ref, s2_ref,
                   q_ref, ks_ref, vst_ref, kw_ref, vwt_ref, kcv_ref, gate_ref, *, d_model, n_groups):
    tm = x_ref.shape[0]
    dh = NSA_HEAD_DIM
    xn = _rmsnorm(x_ref[...], g_ref[...]).astype(BF16)
    y = _dot(xn, w_ref[...])
    c, s1, s2 = c_ref[...], s1_ref[...], s2_ref[...]
    lane = lax.broadcasted_iota(jnp.int32, c.shape, 1)
    khalf = lane < dh
    ck, s1k, s2k = jnp.where(khalf, c, 1.0), jnp.where(khalf, s1, 0.0), jnp.where(khalf, s2, 0.0)
    scale = dh ** -0.5 * LOG2_E
    for j in range(d_model // LANES):
        sl = slice(j * LANES, (j + 1) * LANES)
        q_ref[:, sl] = (_rope(y[:, sl], c, s1, s2) * scale).astype(BF16)
    ones = jnp.ones((VT_ROWS - dh, LANES), F32)
    tok = pl.program_id(1) * tm + lax.broadcasted_iota(jnp.int32, c.shape, 0)
    blk_onehot = jnp.where(lane - dh == tok // SEL_BLOCK, 1.0, 0.0)
    off = d_model
    for k_ref, vt_ref, tag_blocks in ((ks_ref, vst_ref, True), (kw_ref, vwt_ref, False)):
        for g in range(n_groups):
            slab = _rope(y[:, off + g * LANES: off + (g + 1) * LANES], ck, s1k, s2k)
            k_ref[g] = (jnp.where(khalf, slab, blk_onehot) if tag_blocks else slab).astype(BF16)
            for ch in range(tm // LANES):
                vt = slab[ch * LANES:(ch + 1) * LANES, :].T[dh:, :]
                vt_ref[g, ch] = jnp.concatenate([vt, ones], axis=0).astype(BF16)
        off += n_groups * LANES
    nkc = n_groups * dh
    for j in range(nkc // LANES):
        sl = slice(off + j * LANES, off + (j + 1) * LANES)
        kcv_ref[:, j * LANES:(j + 1) * LANES] = _rope(y[:, sl], c, s1, s2)
    off += nkc
    kcv_ref[:, nkc:2 * nkc] = y[:, off:off + nkc]
    off += nkc
    gate_ref[...] = jax.nn.sigmoid(y[:, off:off + LANES])


def _nsa_in(x, g, w_perm, c_tab, s1_tab, s2_tab, *, tm=512):
    B, S, D = x.shape
    G = NSA_KV_GROUPS
    n_out = w_perm.shape[1]
    nkc = G * NSA_HEAD_DIM
    tok = lambda w: pl.BlockSpec((None, tm, w), lambda b, i: (b, i, 0))
    grp = pl.BlockSpec((None, G, tm, LANES), lambda b, i: (b, 0, i, 0))
    grp_t = pl.BlockSpec((None, G, tm // LANES, VT_ROWS, LANES), lambda b, i: (b, 0, i, 0, 0))
    k_shape = jax.ShapeDtypeStruct((B, G, S, LANES), BF16)
    vt_shape = jax.ShapeDtypeStruct((B, G, S // LANES, VT_ROWS, LANES), BF16)
    return pl.pallas_call(
        functools.partial(_nsa_in_kernel, d_model=D, n_groups=G),
        grid=(B, S // tm),
        in_specs=[tok(D), _const_spec((1, D)), _const_spec((D, n_out)), tok(LANES), tok(LANES), tok(LANES)],
        out_specs=[tok(D), grp, grp_t, grp, grp_t, tok(2 * nkc), tok(LANES)],
        out_shape=[jax.ShapeDtypeStruct((B, S, D), BF16), k_shape, vt_shape, k_shape, vt_shape,
                   jax.ShapeDtypeStruct((B, S, 2 * nkc), F32),
                   jax.ShapeDtypeStruct((B, S, LANES), F32)],
        compiler_params=pltpu.CompilerParams(
            dimension_semantics=("parallel", "parallel"), vmem_limit_bytes=VMEM_LIMIT),
        name="nsa_in",
    )(x, g, w_perm, c_tab, s1_tab, s2_tab)


def _nsa_cmp_kernel(xk_ref, xv_ref, pk_ref, pv_ref, wk1_ref, wk2_ref, wv1_ref, wv2_ref, kc_ref, vct_ref):
    half = xk_ref.shape[-1]

    def hidden(x_ref, p_ref, w1_ref):
        x = x_ref[...]
        a = _dot((x + p_ref[:, :half]).astype(BF16), w1_ref[:half, :])
        b = _dot((x + p_ref[:, half:]).astype(BF16), w1_ref[half:, :])
        return jax.nn.gelu(a + pltpu.roll(b, b.shape[0] - 1, axis=0)).astype(BF16)

    kv = _dot(hidden(xk_ref, pk_ref, wk1_ref), wk2_ref[...]) + _dot(hidden(xv_ref, pv_ref, wv1_ref), wv2_ref[...])
    kc_ref[...] = kv.astype(BF16)
    vct_ref[...] = kv.T[NSA_HEAD_DIM:, :].astype(BF16)


def _nsa_cmp(xk, xv, pk, pv, wk1, wk2x, wv1, wv2x):
    B, G, NC, W = xk.shape
    blk = pl.BlockSpec((None, None, NC, W), lambda b, g: (b, g, 0, 0))
    return pl.pallas_call(
        _nsa_cmp_kernel,
        grid=(B, G),
        in_specs=[blk, blk, _const_spec(pk.shape), _const_spec(pv.shape), _const_spec(wk1.shape),
                  _const_spec(wk2x.shape), _const_spec(wv1.shape), _const_spec(wv2x.shape)],
        out_specs=[pl.BlockSpec((None, None, NC, LANES), lambda b, g: (b, g, 0, 0)),
                   pl.BlockSpec((None, None, NSA_HEAD_DIM, NC), lambda b, g: (b, g, 0, 0))],
        out_shape=[jax.ShapeDtypeStruct((B, G, NC, LANES), BF16),
                   jax.ShapeDtypeStruct((B, G, NSA_HEAD_DIM, NC), BF16)],
        compiler_params=pltpu.CompilerParams(
            dimension_semantics=("parallel", "parallel"), vmem_limit_bytes=VMEM_LIMIT),
        name="nsa_cmp",
    )(xk, xv, pk, pv, wk1, wk2x, wv1, wv2x)


def _nsa_attn_kernel(q_ref, kc_ref, vct_ref, ks_ref, vst_ref, kw_ref, vwt_ref, gate_ref, ovt_ref, o_ref,
                     q4w, q4s, acc_s, m_s, acc_w, m_w, ocmp, gt, *, tq, n_sel, n_grp):
    P, dh = NSA_HPG, NSA_HEAD_DIM
    H = n_grp * P
    i = pl.program_id(2)
    qs = i * tq
    head_cols = [slice(h * tq, (h + 1) * tq) for h in range(H)]
    grp_of = [h // P for h in range(H)]

    qt = q_ref[...].astype(F32).T
    zpad = jnp.zeros((LANES - dh, tq), BF16)
    for h, cols in enumerate(head_cols):
        qh = qt[h * dh:(h + 1) * dh].astype(BF16)
        q4w[0:dh, cols] = qh
        q4w[dh:, cols] = zpad
        q4s[0:dh, cols] = qh

    t_q = qs + lax.broadcasted_iota(jnp.int32, (1, tq), 1)

    tk = tq
    n_pre = WINDOW // tk
    sub_i = lax.broadcasted_iota(jnp.int32, (SEL_BLOCK, tq), 0)

    def scores_of(k_ref, k0, q4):
        kts = [k_ref[n, pl.ds(k0, tk), :] for n in range(n_grp)]
        return [_dot(kts[grp_of[h]], q4[:, cols]) for h, cols in enumerate(head_cols)]

    def absorb(scores, vt_ref, k0, acc, m_ref, keep=None, shift=None):
        j0 = k0 // LANES
        vts = [jnp.concatenate([vt_ref[n, j0 + c] for c in range(tk // LANES)], axis=1) for n in range(n_grp)]
        n_slab = tk // SEL_BLOCK
        keeps = None if keep is None else [keep(k0 + b * SEL_BLOCK + sub_i) for b in range(n_slab)]
        for h, cols in enumerate(head_cols):
            slabs = []
            for b in range(n_slab):
                sb = scores[h][b * SEL_BLOCK:(b + 1) * SEL_BLOCK]
                if shift is not None:
                    sb = sb + shift
                if keeps is not None:
                    sb = jnp.where(keeps[b], sb, NEG_INF)
                slabs.append(sb)
            top = slabs[0]
            for sb in slabs[1:]:
                top = jnp.maximum(top, sb)
            m_old = m_ref[0:1, cols]
            m_new = jnp.maximum(m_old, jnp.max(top, axis=0, keepdims=True))
            alpha = jnp.exp2(m_old - m_new)
            pr = jnp.concatenate([jnp.exp2(sb - m_new).astype(BF16) for sb in slabs], axis=0)
            acc[:, cols] = acc[:, cols] * alpha + _dot(vts[grp_of[h]], pr)
            m_ref[0:1, cols] = m_new

    for m_ref, acc in ((m_s, acc_s), (m_w, acc_w)):
        m_ref[...] = jnp.full(m_ref.shape, NEG_INF, F32)
        acc[...] = jnp.zeros(acc.shape, F32)

    nc = kc_ref.shape[1]
    sc_all = [_dot(kc_ref[grp_of[h]], q4w[:, cols]) for h, cols in enumerate(head_cols)]
    win = []
    for d in range(n_pre, -1, -1):
        j = i - d
        k0 = pl.multiple_of(jnp.maximum(j, 0) * tk, tk)
        win.append((d, j, k0, scores_of(kw_ref, k0, q4w)))

    n_i = lax.broadcasted_iota(jnp.int32, (nc, tq), 0)
    cbias = jnp.where(n_i * CMP_STRIDE + (CMP_BLOCK - 1) <= t_q, 0.0, NEG_INF)
    has_c = t_q >= CMP_BLOCK - 1
    pcs = []
    for sc in sc_all:
        sc = sc + cbias
        ec = jnp.exp2(sc - jnp.max(sc, axis=0, keepdims=True))
        pcs.append(ec * jnp.where(has_c, 1.0 / jnp.sum(ec, axis=0, keepdims=True), 0.0))
    for h, cols in enumerate(head_cols):
        ocmp[:, cols] = _dot(vct_ref[grp_of[h]], pcs[h].astype(BF16))

    ovt = ovt_ref[...]
    m_i = lax.broadcasted_iota(jnp.int32, (n_sel, tq), 0)
    cur = jnp.right_shift(t_q, 6)
    forced = (m_i == 0) | (m_i == cur) | (m_i == cur - 1)
    in_past = m_i * SEL_BLOCK <= t_q
    sub8 = lax.broadcasted_iota(jnp.int32, (8, tq), 0)
    for n in range(n_grp):
        psum = (pcs[n * P] + pcs[n * P + 1]) + (pcs[n * P + 2] + pcs[n * P + 3])
        hi = psum.astype(BF16)
        r1 = psum - hi.astype(F32)
        mid = r1.astype(BF16)
        lo = (r1 - mid.astype(F32)).astype(BF16)
        imp = (_dot(ovt, hi) + _dot(ovt, mid) + _dot(ovt, lo))[0:n_sel]
        imp = jnp.where(forced, FORCE_SCORE, imp)
        imp = jnp.where(in_past, imp, NEG_INF)
        groups = [imp[8 * r:8 * r + 8] for r in range(n_sel // 8)]
        ranks = [jnp.zeros((8, tq), F32) for _ in groups]
        for mp in range(n_sel):
            row = imp[mp:mp + 1, :]
            for r, vr in enumerate(groups):
                if 8 * r > mp:
                    beats = row >= vr
                elif 8 * r + 7 < mp:
                    beats = row > vr
                else:
                    beats = (row > vr) | ((row == vr) & (sub8 + 8 * r > mp))
                ranks[r] = ranks[r] + jnp.where(beats, 1.0, 0.0)
        sel_bias = jnp.where(jnp.concatenate(ranks, axis=0) < float(min(SEL_TOPN, n_sel)), 0.0, NEG_INF)
        if n_sel < LANES - dh:
            sel_bias = jnp.concatenate([sel_bias, jnp.zeros((LANES - dh - n_sel, tq), F32)], axis=0)
        for cols in head_cols[n * P:(n + 1) * P]:
            q4s[dh:, cols] = sel_bias.astype(BF16)

    causal = lambda kpos: kpos <= t_q
    for d, j, k0, s_win in win:
        if d == n_pre:
            lowest = t_q - WINDOW + jnp.where(j >= 0, 0, 1 << 30)
            absorb(s_win, vwt_ref, k0, acc_w, m_w, keep=lambda kpos: kpos > lowest)
        elif d == 0:
            absorb(s_win, vwt_ref, k0, acc_w, m_w, keep=causal)
        else:
            absorb(s_win, vwt_ref, k0, acc_w, m_w, shift=jnp.where(j >= 0, 0.0, NEG_INF))

    k_diag = pl.multiple_of(qs, tk)
    absorb(scores_of(ks_ref, k_diag, q4s), vst_ref, k_diag, acc_s, m_s, keep=causal)

    def sel_tiles(k0s):
        sc = [scores_of(ks_ref, k0, q4s) for k0 in k0s]
        for k0, s in zip(k0s, sc):
            absorb(s, vst_ref, k0, acc_s, m_s)

    unroll = 4

    def main_body(a, c):
        sel_tiles([pl.multiple_of((a * unroll + u) * tk, tk) for u in range(unroll)])
        return c

    lax.fori_loop(0, i // unroll, main_body, 0)
    done = (i // unroll) * unroll
    rem = i - done

    @pl.when(jnp.bitwise_and(rem, 2) != 0)
    def _():
        sel_tiles([pl.multiple_of((done + u) * tk, tk) for u in range(2)])

    @pl.when(jnp.bitwise_and(rem, 1) != 0)
    def _():
        sel_tiles([pl.multiple_of((done + jnp.bitwise_and(rem, 2)) * tk, tk)])

    o_s = acc_s[0:dh, :] * (1.0 / acc_s[dh:dh + 1, :])
    o_w = acc_w[0:dh, :] * (1.0 / acc_w[dh:dh + 1, :])

    gt[...] = gate_ref[...].T
    h0 = H * pl.program_id(1)
    outs = []
    for h, cols in enumerate(head_cols):
        outs.append(gt[pl.ds(h0 + h, 1), :] * ocmp[:, cols]
                    + gt[pl.ds(NSA_HEADS + h0 + h, 1), :] * o_s[:, cols]
                    + gt[pl.ds(2 * NSA_HEADS + h0 + h, 1), :] * o_w[:, cols])
    o_ref[...] = jnp.concatenate(outs, axis=0).T.astype(BF16)


def _nsa_attn(q, kc, vct, ks, vst, kw, vwt, gates, ovt, *, tq=256, n_grp=2):
    B, S, D = q.shape
    G, P, dh = NSA_KV_GROUPS, NSA_HPG, NSA_HEAD_DIM
    NC = kc.shape[2]
    n_sel = S // SEL_BLOCK
    assert tq % LANES == 0 and tq & (tq - 1) == 0 and WINDOW % tq == 0 and G % n_grp == 0
    assert n_sel % 16 == 0 and n_sel <= LANES - dh
    L = n_grp * P * tq
    qblk = pl.BlockSpec((None, tq, n_grp * P * dh), lambda b, g, i: (b, i, g))
    grp = lambda *tail: pl.BlockSpec((None, n_grp) + tail, lambda b, g, i: (b, g) + (0,) * len(tail))
    seq, seq_t = grp(S, LANES), grp(S // LANES, VT_ROWS, LANES)
    return pl.pallas_call(
        functools.partial(_nsa_attn_kernel, tq=tq, n_sel=n_sel, n_grp=n_grp),
        grid=(B, G // n_grp, S // tq),
        in_specs=[qblk, grp(NC, LANES), grp(dh, NC), seq, seq_t, seq, seq_t,
                  pl.BlockSpec((None, tq, LANES), lambda b, g, i: (b, i, 0)),
                  _const_spec(ovt.shape)],
        out_specs=qblk,
        out_shape=jax.ShapeDtypeStruct((B, S, D), BF16),
        scratch_shapes=[pltpu.VMEM((LANES, L), BF16),
                        pltpu.VMEM((LANES, L), BF16),
                        pltpu.VMEM((VT_ROWS, L), F32),
                        pltpu.VMEM((8, L), F32),
                        pltpu.VMEM((VT_ROWS, L), F32),
                        pltpu.VMEM((8, L), F32),
                        pltpu.VMEM((dh, L), F32),
                        pltpu.VMEM((LANES, tq), F32)],
        compiler_params=pltpu.CompilerParams(
            dimension_semantics=("parallel", "parallel", "arbitrary"), vmem_limit_bytes=VMEM_LIMIT),
        name="nsa_attn",
    )(q, kc, vct, ks, vst, kw, vwt, gates, ovt)


def _sgu_kernel(x_ref, g_ref, w_ref, lng_ref, lnb_ref, ws_ref, bs_ref, o_ref, *, width):
    T = SGU_CHUNK
    tm = x_ref.shape[0]
    xn = _rmsnorm(x_ref[...], g_ref[...]).astype(BF16)
    y = jax.nn.gelu(_dot(xn, w_ref[...]))
    u, v = y[:, :width], y[:, width:]
    mu = jnp.mean(v, axis=-1, keepdims=True)
    vc = v - mu
    var = jnp.mean(vc * vc, axis=-1, keepdims=True)
    v = (vc * lax.rsqrt(var + LN_EPS) * lng_ref[...] + lnb_ref[...]).astype(BF16)
    r_i = lax.broadcasted_iota(jnp.int32, (T, T), 0)
    c_i = lax.broadcasted_iota(jnp.int32, (T, T), 1)
    gd = width // SGU_GROUPS
    for gi in range(SGU_GROUPS):
        ws = jnp.where(r_i >= c_i, ws_ref[gi], 0.0).astype(BF16)
        cols = slice(gi * gd, (gi + 1) * gd)
        for c in range(tm // T):
            rws = slice(c * T, (c + 1) * T)
            sv = _dot(ws, v[rws, cols]) + bs_ref[gi]
            o_ref[rws, cols] = (u[rws, cols] * sv).astype(BF16)


def _sgu(x2, g, w_in, ln_g, ln_b, w_s, bs_b, *, tm=256):
    N, D = x2.shape
    W = w_in.shape[1] // 2
    tok = lambda w: pl.BlockSpec((tm, w), lambda i: (i, 0))
    return pl.pallas_call(
        functools.partial(_sgu_kernel, width=W),
        grid=(N // tm,),
        in_specs=[tok(D), _const_spec((1, D)), _const_spec(w_in.shape), _const_spec((1, W)), _const_spec((1, W)),
                  _const_spec(w_s.shape), _const_spec(bs_b.shape)],
        out_specs=tok(W),
        out_shape=jax.ShapeDtypeStruct((N, W), BF16),
        compiler_params=pltpu.CompilerParams(dimension_semantics=("parallel",), vmem_limit_bytes=VMEM_LIMIT),
        name="sgu",
    )(x2, g, w_in, ln_g, ln_b, w_s, bs_b)


def _pool_kernel(x_ref, g_ref, w_ref, wg_ref, sc_ref, o_ref, zbuf, *, hist):
    tm = x_ref.shape[0]
    i = pl.program_id(1)

    @pl.when(i == 0)
    def _():
        zbuf[0:hist, :] = jnp.zeros((hist, zbuf.shape[1]), F32)

    xn = _rmsnorm(x_ref[...], g_ref[...]).astype(BF16)
    z = _dot(xn, w_ref[...])
    zbuf[hist:hist + tm, :] = z
    t = i * tm + lax.broadcasted_iota(jnp.int32, (tm, 1), 0)
    gd = z.shape[1] // len(POOL_WINDOWS)
    for gi, w in enumerate(POOL_WINDOWS):
        cols = slice(gi * gd, (gi + 1) * gd)
        win = z[:, cols]
        for k in range(1, w):
            win = win + zbuf[hist - k:hist - k + tm, cols]
        cnt = jnp.minimum(t + 1, w).astype(F32)
        d = win / cnt - z[:, cols]
        y = _dot(d.astype(BF16), wg_ref[gi]) * sc_ref[:, cols]
        o_ref[:, cols] = y.astype(BF16)
    zbuf[0:hist, :] = zbuf[tm:tm + hist, :]


def _pool(x, g, w_in, w_grp, scale, *, tm=256):
    B, S, D = x.shape
    PW = w_in.shape[1]
    hist = 16
    assert hist >= max(POOL_WINDOWS) - 1 and tm >= hist
    tok = lambda w: pl.BlockSpec((None, tm, w), lambda b, i: (b, i, 0))
    return pl.pallas_call(
        functools.partial(_pool_kernel, hist=hist),
        grid=(B, S // tm),
        in_specs=[tok(D), _const_spec((1, D)), _const_spec(w_in.shape), _const_spec(w_grp.shape),
                  _const_spec((1, PW))],
        out_specs=tok(PW),
        out_shape=jax.ShapeDtypeStruct((B, S, PW), BF16),
        scratch_shapes=[pltpu.VMEM((hist + tm, PW), F32)],
        compiler_params=pltpu.CompilerParams(
            dimension_semantics=("arbitrary", "arbitrary"), vmem_limit_bytes=VMEM_LIMIT),
        name="pool",
    )(x, g, w_in, w_grp, scale)


def _post_kernel(x_ref, u_ref, wo_ref, g_ref, wgu_ref, wd_ref, fg_ref, o_ref, *, hidden, final):
    x1 = x_ref[...] + _dot(u_ref[...], wo_ref[...])
    xn = _rmsnorm(x1, g_ref[...]).astype(BF16)
    gate = _dot(xn, wgu_ref[:, :hidden])
    up = _dot(xn, wgu_ref[:, hidden:])
    a = (jax.nn.silu(gate) * up).astype(BF16)
    x2 = x1 + _dot(a, wd_ref[...])
    if final:
        x2 = _rmsnorm(x2, fg_ref[...])
    o_ref[...] = x2


def _post(x2, u2, w_out, g, w_gu, w_down, fg, *, final, tm=256):
    N, D = x2.shape
    hidden = w_down.shape[0]
    tok = lambda w: pl.BlockSpec((tm, w), lambda i: (i, 0))
    return pl.pallas_call(
        functools.partial(_post_kernel, hidden=hidden, final=final),
        grid=(N // tm,),
        in_specs=[tok(D), tok(u2.shape[1]), _const_spec(w_out.shape), _const_spec((1, D)),
                  _const_spec(w_gu.shape), _const_spec(w_down.shape), _const_spec((1, D))],
        out_specs=tok(D),
        out_shape=jax.ShapeDtypeStruct((N, D), F32),
        compiler_params=pltpu.CompilerParams(dimension_semantics=("parallel",), vmem_limit_bytes=VMEM_LIMIT),
        name="post",
    )(x2, u2, w_out, g, w_gu, w_down, fg)


def _rope_tables(positions):
    half = ROPE_DIM // 2
    inv_freq = ROPE_THETA ** (-jnp.arange(0, ROPE_DIM, 2, dtype=F32) / ROPE_DIM)
    ang = positions.astype(F32)[..., None] * inv_freq
    cos, sin = jnp.cos(ang), jnp.sin(ang)
    z = lambda n: jnp.zeros(ang.shape[:-1] + (n,), F32)
    dh = NSA_HEAD_DIM
    c64 = jnp.concatenate([cos, cos, 1.0 + z(dh - ROPE_DIM)], axis=-1)
    s1 = jnp.concatenate([-sin, z(dh - half)], axis=-1)
    s2 = jnp.concatenate([z(half), sin, z(dh - ROPE_DIM)], axis=-1)
    rep = LANES // dh
    return tuple(jnp.tile(t, (1, 1, rep)) for t in (c64, s1, s2))


def _permute_nsa_w_in(w):
    H, G, dh = NSA_HEADS, NSA_KV_GROUPS, NSA_HEAD_DIM
    base = H * dh
    seg = lambda k: w[:, base + k * G * dh: base + (k + 1) * G * dh]
    kc, vc, ks, vs, kw, vw = (seg(k) for k in range(6))
    gl = w[:, base + 6 * G * dh:]
    inter = lambda a, b: jnp.concatenate(
        [a.reshape(-1, G, 1, dh), b.reshape(-1, G, 1, dh)], axis=2).reshape(-1, 2 * G * dh)
    glp = jnp.pad(gl, ((0, 0), (0, LANES - gl.shape[1])))
    return jnp.concatenate([w[:, :base], inter(ks, vs), inter(kw, vw), kc, vc, glp], axis=1).astype(BF16)


def _overlap_matrix_t(n_chunks, n_sel):
    n = np.arange(n_chunks)[None, :]
    m = np.arange(LANES)[:, None]
    start, end = n * CMP_STRIDE, n * CMP_STRIDE + CMP_BLOCK - 1
    ov = (start <= m * SEL_BLOCK + SEL_BLOCK - 1) & (end >= m * SEL_BLOCK) & (m < n_sel)
    return jnp.asarray(ov, BF16)


def _nsa_layer(x, tabs, g, w_in, pos_k, k_w1, k_w2, pos_v, v_w1, v_w2):
    B, S, D = x.shape
    G, dh = NSA_KV_GROUPS, NSA_HEAD_DIM
    q, ks, vst, kw, vwt, kcv, gates = _nsa_in(x, g.reshape(1, D), _permute_nsa_w_in(w_in), *tabs)
    n_chunks = S // CMP_STRIDE
    chunks = lambda t: t.reshape(B, n_chunks, CMP_STRIDE, G, dh).transpose(0, 3, 1, 2, 4).reshape(
        B, G, n_chunks, CMP_STRIDE * dh)
    nkc = G * dh
    pad_r = lambda w2: jnp.pad(w2, ((0, 0), (0, LANES - dh))).astype(BF16)
    pad_l = lambda w2: jnp.pad(w2, ((0, 0), (LANES - dh, 0))).astype(BF16)
    kc, vct = _nsa_cmp(chunks(kcv[..., :nkc]), chunks(kcv[..., nkc:]),
                       pos_k.reshape(1, CMP_BLOCK * dh), pos_v.reshape(1, CMP_BLOCK * dh),
                       k_w1.astype(BF16), pad_r(k_w2), v_w1.astype(BF16), pad_l(v_w2))
    ovt = _overlap_matrix_t(n_chunks, S // SEL_BLOCK)
    return _nsa_attn(q, kc, vct, ks, vst, kw, vwt, gates, ovt)


def kernel(x, positions, mix_norm, ffn_norm, final_norm, nsa_w_in, nsa_cmp_pos_k, nsa_cmp_k_w1, nsa_cmp_k_w2, nsa_cmp_pos_v, nsa_cmp_v_w1, nsa_cmp_v_w2, nsa_w_out, sgu_w_in, sgu_ln_g, sgu_ln_b, sgu_w_s, sgu_b_s, sgu_w_out, pool_w_in, pool_w_grp, pool_scale, pool_w_out, ffn_w_gate_up, ffn_w_down):
    B, S, D = x.shape
    depth = mix_norm.shape[0]
    tabs = _rope_tables(positions)
    row = lambda v: v.reshape(1, -1)
    for i in range(depth):
        kind, j = i % N_MIXERS, i // N_MIXERS
        if kind == 0:
            u = _nsa_layer(x, tabs, mix_norm[i], nsa_w_in[j], nsa_cmp_pos_k[j], nsa_cmp_k_w1[j], nsa_cmp_k_w2[j],
                           nsa_cmp_pos_v[j], nsa_cmp_v_w1[j], nsa_cmp_v_w2[j])
            w_out = nsa_w_out[j]
        elif kind == 1:
            gd = sgu_w_in.shape[-1] // 2 // SGU_GROUPS
            bs_b = jnp.broadcast_to(sgu_b_s[j][:, :, None], sgu_b_s[j].shape + (gd,))
            u = _sgu(x.reshape(B * S, D), row(mix_norm[i]), sgu_w_in[j].astype(BF16), row(sgu_ln_g[j]),
                     row(sgu_ln_b[j]), sgu_w_s[j], bs_b)
            w_out = sgu_w_out[j]
        else:
            u = _pool(x, row(mix_norm[i]), pool_w_in[j].astype(BF16), pool_w_grp[j].astype(BF16),
                      row(pool_scale[j]))
            w_out = pool_w_out[j]
        x = _post(x.reshape(B * S, D), u.reshape(B * S, -1), w_out.astype(BF16), row(ffn_norm[i]),
                  ffn_w_gate_up[i].astype(BF16), ffn_w_down[i].astype(BF16), row(final_norm),
                  final=(i == depth - 1)).reshape(B, S, D)
    return x
```

```python
import functools

import jax
import jax.numpy as jnp
import numpy as np
from jax import lax
from jax.experimental import pallas as pl
from jax.experimental.pallas import tpu as pltpu

RMS_EPS = 1e-6
LN_EPS = 1e-5
NEG_INF = -1e30
FORCE_SCORE = 1e9
LOG2_E = 1.4426950408889634

N_MIXERS = 3
NSA_HEADS = 16
NSA_KV_GROUPS = 4
NSA_HPG = NSA_HEADS // NSA_KV_GROUPS
NSA_HEAD_DIM = 64
CMP_BLOCK = 32
CMP_STRIDE = 16
SEL_BLOCK = 64
SEL_TOPN = 16
WINDOW = 512
ROPE_THETA = 500000.0
ROPE_DIM = NSA_HEAD_DIM // 4
SGU_GROUPS = 8
SGU_CHUNK = 128
POOL_WINDOWS = (2, 4, 8, 16)

LANES = 128
VMEM_LIMIT = 56 * 1024 * 1024

BF16 = jnp.bfloat16
F32 = jnp.float32


def _const_spec(shape):
    nd = len(shape)
    return pl.BlockSpec(shape, lambda *_: (0,) * nd, pipeline_mode=pl.Buffered(1))


def _rmsnorm(x, g):
    ms = jnp.mean(x * x, axis=-1, keepdims=True)
    return x * lax.rsqrt(ms + RMS_EPS) * g


def _dot(a, b):
    return jnp.dot(a, b, preferred_element_type=F32)


VT_ROWS = NSA_HEAD_DIM + 16


def _rope(t, c, s1, s2):
    return t * c + pltpu.roll(t, LANES - ROPE_DIM // 2, axis=1) * s1 + pltpu.roll(t, ROPE_DIM // 2, axis=1) * s2


def _nsa_in_kernel(x_ref, g_ref, w_ref, c_ref, s1_ref, s2_ref,
                   q_ref, ks_ref, vst_ref, kw_ref, vwt_ref, kcv_ref, gate_ref, *, d_model, n_groups):
    tm = x_ref.shape[0]
    dh = NSA_HEAD_DIM
    xn = _rmsnorm(x_ref[...], g_ref[...]).astype(BF16)
    y = _dot(xn, w_ref[...])
    c, s1, s2 = c_ref[...], s1_ref[...], s2_ref[...]
    lane = lax.broadcasted_iota(jnp.int32, c.shape, 1)
    khalf = lane < dh
    ck, s1k, s2k = jnp.where(khalf, c, 1.0), jnp.where(khalf, s1, 0.0), jnp.where(khalf, s2, 0.0)
    scale = dh ** -0.5 * LOG2_E
    for j in range(d_model // LANES):
        sl = slice(j * LANES, (j + 1) * LANES)
        q_ref[:, sl] = (_rope(y[:, sl], c, s1, s2) * scale).astype(BF16)
    ones = jnp.ones((VT_ROWS - dh, LANES), F32)
    tok = pl.program_id(1) * tm + lax.broadcasted_iota(jnp.int32, c.shape, 0)
    blk_onehot = jnp.where(lane - dh == tok // SEL_BLOCK, 1.0, 0.0)
    off = d_model
    for k_ref, vt_ref, tag_blocks in ((ks_ref, vst_ref, True), (kw_ref, vwt_ref, False)):
        for g in range(n_groups):
            slab = _rope(y[:, off + g * LANES: off + (g + 1) * LANES], ck, s1k, s2k)
            k_ref[g] = (jnp.where(khalf, slab, blk_onehot) if tag_blocks else slab).astype(BF16)
            for ch in range(tm // LANES):
                vt = slab[ch * LANES:(ch + 1) * LANES, :].T[dh:, :]
                vt_ref[g, ch] = jnp.concatenate([vt, ones], axis=0).astype(BF16)
        off += n_groups * LANES
    for g in range(n_groups):
        kcv_ref[g] = _rope(y[:, off + g * LANES: off + (g + 1) * LANES], ck, s1k, s2k)
    off += n_groups * LANES
    gate_ref[...] = jax.nn.sigmoid(y[:, off:off + LANES])


def _nsa_in(x, g, w_perm, c_tab, s1_tab, s2_tab, *, tm=512):
    B, S, D = x.shape
    G = NSA_KV_GROUPS
    n_out = w_perm.shape[1]
    tok = lambda w: pl.BlockSpec((None, tm, w), lambda b, i: (b, i, 0))
    grp = pl.BlockSpec((None, G, tm, LANES), lambda b, i: (b, 0, i, 0))
    grp_t = pl.BlockSpec((None, G, tm // LANES, VT_ROWS, LANES), lambda b, i: (b, 0, i, 0, 0))
    k_shape = jax.ShapeDtypeStruct((B, G, S, LANES), BF16)
    vt_shape = jax.ShapeDtypeStruct((B, G, S // LANES, VT_ROWS, LANES), BF16)
    return pl.pallas_call(
        functools.partial(_nsa_in_kernel, d_model=D, n_groups=G),
        grid=(B, S // tm),
        in_specs=[tok(D), _const_spec((1, D)), _const_spec((D, n_out)), tok(LANES), tok(LANES), tok(LANES)],
        out_specs=[tok(D), grp, grp_t, grp, grp_t, grp, tok(LANES)],
        out_shape=[jax.ShapeDtypeStruct((B, S, D), BF16), k_shape, vt_shape, k_shape, vt_shape,
                   jax.ShapeDtypeStruct((B, G, S, LANES), F32),
                   jax.ShapeDtypeStruct((B, S, LANES), F32)],
        compiler_params=pltpu.CompilerParams(
            dimension_semantics=("parallel", "parallel"), vmem_limit_bytes=VMEM_LIMIT),
        name="nsa_in",
    )(x, g, w_perm, c_tab, s1_tab, s2_tab)


def _nsa_cmp_kernel(x_ref, pa_ref, pb_ref, wa_ref, wb_ref, wk2_ref, wv2_ref, kc_ref, vct_ref, *, n_chunks):
    acc_a = jnp.zeros((n_chunks, wa_ref.shape[-1]), F32)
    acc_b = jnp.zeros((n_chunks, wb_ref.shape[-1]), F32)
    for l in range(CMP_STRIDE):
        xl = x_ref[pl.ds(l, n_chunks, stride=CMP_STRIDE), :]
        acc_a = acc_a + _dot((xl + pa_ref[l:l + 1, :]).astype(BF16), wa_ref[l])
        acc_b = acc_b + _dot((xl + pb_ref[l:l + 1, :]).astype(BF16), wb_ref[l])
    hid = jax.nn.gelu(acc_a + pltpu.roll(acc_b, n_chunks - 1, axis=0)).astype(BF16)
    hw = hid.shape[1] // 2
    kv = _dot(hid[:, :hw], wk2_ref[...]) + _dot(hid[:, hw:], wv2_ref[...])
    kc_ref[...] = kv.astype(BF16)
    vct_ref[...] = kv.T[NSA_HEAD_DIM:, :].astype(BF16)


def _nsa_cmp(kcv, pa, pb, wa, wb, wk2x, wv2x):
    B, G, S, _ = kcv.shape
    NC = S // CMP_STRIDE
    return pl.pallas_call(
        functools.partial(_nsa_cmp_kernel, n_chunks=NC),
        grid=(B, G),
        in_specs=[pl.BlockSpec((None, None, S, LANES), lambda b, g: (b, g, 0, 0)),
                  _const_spec(pa.shape), _const_spec(pb.shape), _const_spec(wa.shape), _const_spec(wb.shape),
                  _const_spec(wk2x.shape), _const_spec(wv2x.shape)],
        out_specs=[pl.BlockSpec((None, None, NC, LANES), lambda b, g: (b, g, 0, 0)),
                   pl.BlockSpec((None, None, NSA_HEAD_DIM, NC), lambda b, g: (b, g, 0, 0))],
        out_shape=[jax.ShapeDtypeStruct((B, G, NC, LANES), BF16),
                   jax.ShapeDtypeStruct((B, G, NSA_HEAD_DIM, NC), BF16)],
        compiler_params=pltpu.CompilerParams(
            dimension_semantics=("parallel", "parallel"), vmem_limit_bytes=VMEM_LIMIT),
        name="nsa_cmp",
    )(kcv, pa, pb, wa, wb, wk2x, wv2x)


def _nsa_attn_kernel(q_ref, kc_ref, vct_ref, ks_ref, vst_ref, kw_ref, vwt_ref, gate_ref, ovt_ref, o_ref,
                     q4w, q4s, acc_s, m_s, acc_w, m_w, ocmp, gt, *, tq, n_sel, n_grp):
    P, dh = NSA_HPG, NSA_HEAD_DIM
    H = n_grp * P
    i = pl.program_id(2)
    qs = i * tq
    head_cols = [slice(h * tq, (h + 1) * tq) for h in range(H)]
    grp_of = [h // P for h in range(H)]

    qt = q_ref[...].astype(F32).T
    zpad = jnp.zeros((LANES - dh, tq), BF16)
    for h, cols in enumerate(head_cols):
        qh = qt[h * dh:(h + 1) * dh].astype(BF16)
        q4w[0:dh, cols] = qh
        q4w[dh:, cols] = zpad
        q4s[0:dh, cols] = qh

    t_q = qs + lax.broadcasted_iota(jnp.int32, (1, tq), 1)

    tk = tq
    n_pre = WINDOW // tk
    sub_i = lax.broadcasted_iota(jnp.int32, (SEL_BLOCK, tq), 0)

    def scores_of(k_ref, k0, q4):
        kts = [k_ref[n, pl.ds(k0, tk), :] for n in range(n_grp)]
        return [_dot(kts[grp_of[h]], q4[:, cols]) for h, cols in enumerate(head_cols)]

    def absorb(scores, vt_ref, k0, acc, m_ref, keep=None, shift=None):
        j0 = k0 // LANES
        vts = [jnp.concatenate([vt_ref[n, j0 + c] for c in range(tk // LANES)], axis=1) for n in range(n_grp)]
        n_slab = tk // SEL_BLOCK
        keeps = None if keep is None else [keep(k0 + b * SEL_BLOCK + sub_i) for b in range(n_slab)]
        for h, cols in enumerate(head_cols):
            slabs = []
            for b in range(n_slab):
                sb = scores[h][b * SEL_BLOCK:(b + 1) * SEL_BLOCK]
                if shift is not None:
                    sb = sb + shift
                if keeps is not None:
                    sb = jnp.where(keeps[b], sb, NEG_INF)
                slabs.append(sb)
            top = slabs[0]
            for sb in slabs[1:]:
                top = jnp.maximum(top, sb)
            m_old = m_ref[0:1, cols]
            m_new = jnp.maximum(m_old, jnp.max(top, axis=0, keepdims=True))
            alpha = jnp.exp2(m_old - m_new)
            pr = jnp.concatenate([jnp.exp2(sb - m_new).astype(BF16) for sb in slabs], axis=0)
            acc[:, cols] = acc[:, cols] * alpha + _dot(vts[grp_of[h]], pr)
            m_ref[0:1, cols] = m_new

    for m_ref, acc in ((m_s, acc_s), (m_w, acc_w)):
        m_ref[...] = jnp.full(m_ref.shape, NEG_INF, F32)
        acc[...] = jnp.zeros(acc.shape, F32)

    nc = kc_ref.shape[1]
    sc_all = [_dot(kc_ref[grp_of[h]], q4w[:, cols]) for h, cols in enumerate(head_cols)]
    win = []
    for d in range(n_pre, -1, -1):
        j = i - d
        k0 = pl.multiple_of(jnp.maximum(j, 0) * tk, tk)
        win.append((d, j, k0, scores_of(kw_ref, k0, q4w)))

    n_i = lax.broadcasted_iota(jnp.int32, (nc, tq), 0)
    cbias = jnp.where(n_i * CMP_STRIDE + (CMP_BLOCK - 1) <= t_q, 0.0, NEG_INF)
    has_c = t_q >= CMP_BLOCK - 1
    pcs = []
    for sc in sc_all:
        sc = sc + cbias
        ec = jnp.exp2(sc - jnp.max(sc, axis=0, keepdims=True))
        pcs.append(ec * jnp.where(has_c, 1.0 / jnp.sum(ec, axis=0, keepdims=True), 0.0))
    for h, cols in enumerate(head_cols):
        ocmp[:, cols] = _dot(vct_ref[grp_of[h]], pcs[h].astype(BF16))

    ovt = ovt_ref[...]
    m_i = lax.broadcasted_iota(jnp.int32, (n_sel, tq), 0)
    cur = jnp.right_shift(t_q, 6)
    forced = (m_i == 0) | (m_i == cur) | (m_i == cur - 1)
    in_past = m_i * SEL_BLOCK <= t_q
    sub8 = lax.broadcasted_iota(jnp.int32, (8, tq), 0)
    for n in range(n_grp):
        psum = (pcs[n * P] + pcs[n * P + 1]) + (pcs[n * P + 2] + pcs[n * P + 3])
        hi = psum.astype(BF16)
        r1 = psum - hi.astype(F32)
        mid = r1.astype(BF16)
        lo = (r1 - mid.astype(F32)).astype(BF16)
        imp = (_dot(ovt, hi) + _dot(ovt, mid) + _dot(ovt, lo))[0:n_sel]
        imp = jnp.where(forced, FORCE_SCORE, imp)
        imp = jnp.where(in_past, imp, NEG_INF)
        groups = [imp[8 * r:8 * r + 8] for r in range(n_sel // 8)]
        ranks = [jnp.zeros((8, tq), F32) for _ in groups]
        for mp in range(n_sel):
            row = imp[mp:mp + 1, :]
            for r, vr in enumerate(groups):
                if 8 * r > mp:
                    beats = row >= vr
                elif 8 * r + 7 < mp:
                    beats = row > vr
                else:
                    beats = (row > vr) | ((row == vr) & (sub8 + 8 * r > mp))
                ranks[r] = ranks[r] + jnp.where(beats, 1.0, 0.0)
        sel_bias = jnp.where(jnp.concatenate(ranks, axis=0) < float(min(SEL_TOPN, n_sel)), 0.0, NEG_INF)
        if n_sel < LANES - dh:
            sel_bias = jnp.concatenate([sel_bias, jnp.zeros((LANES - dh - n_sel, tq), F32)], axis=0)
        for cols in head_cols[n * P:(n + 1) * P]:
            q4s[dh:, cols] = sel_bias.astype(BF16)

    causal = lambda kpos: kpos <= t_q
    for d, j, k0, s_win in win:
        if d == n_pre:
            lowest = t_q - WINDOW + jnp.where(j >= 0, 0, 1 << 30)
            absorb(s_win, vwt_ref, k0, acc_w, m_w, keep=lambda kpos: kpos > lowest)
        elif d == 0:
            absorb(s_win, vwt_ref, k0, acc_w, m_w, keep=causal)
        else:
            absorb(s_win, vwt_ref, k0, acc_w, m_w, shift=jnp.where(j >= 0, 0.0, NEG_INF))

    k_diag = pl.multiple_of(qs, tk)
    absorb(scores_of(ks_ref, k_diag, q4s), vst_ref, k_diag, acc_s, m_s, keep=causal)

    def sel_tiles(k0s):
        sc = [scores_of(ks_ref, k0, q4s) for k0 in k0s]
        for k0, s in zip(k0s, sc):
            absorb(s, vst_ref, k0, acc_s, m_s)

    unroll = 4

    def main_body(a, c):
        sel_tiles([pl.multiple_of((a * unroll + u) * tk, tk) for u in range(unroll)])
        return c

    lax.fori_loop(0, i // unroll, main_body, 0)
    done = (i // unroll) * unroll
    rem = i - done

    @pl.when(jnp.bitwise_and(rem, 2) != 0)
    def _():
        sel_tiles([pl.multiple_of((done + u) * tk, tk) for u in range(2)])

    @pl.when(jnp.bitwise_and(rem, 1) != 0)
    def _():
        sel_tiles([pl.multiple_of((done + jnp.bitwise_and(rem, 2)) * tk, tk)])

    o_s = acc_s[0:dh, :] * (1.0 / acc_s[dh:dh + 1, :])
    o_w = acc_w[0:dh, :] * (1.0 / acc_w[dh:dh + 1, :])

    gt[...] = gate_ref[...].T
    h0 = H * pl.program_id(1)
    outs = []
    for h, cols in enumerate(head_cols):
        outs.append(gt[pl.ds(h0 + h, 1), :] * ocmp[:, cols]
                    + gt[pl.ds(NSA_HEADS + h0 + h, 1), :] * o_s[:, cols]
                    + gt[pl.ds(2 * NSA_HEADS + h0 + h, 1), :] * o_w[:, cols])
    o_ref[...] = jnp.concatenate(outs, axis=0).T.astype(BF16)


def _nsa_attn(q, kc, vct, ks, vst, kw, vwt, gates, ovt, *, tq=256, n_grp=2):
    B, S, D = q.shape
    G, P, dh = NSA_KV_GROUPS, NSA_HPG, NSA_HEAD_DIM
    NC = kc.shape[2]
    n_sel = S // SEL_BLOCK
    assert tq % LANES == 0 and tq & (tq - 1) == 0 and WINDOW % tq == 0 and G % n_grp == 0
    assert n_sel % 16 == 0 and n_sel <= LANES - dh
    L = n_grp * P * tq
    qblk = pl.BlockSpec((None, tq, n_grp * P * dh), lambda b, g, i: (b, i, g))
    grp = lambda *tail: pl.BlockSpec((None, n_grp) + tail, lambda b, g, i: (b, g) + (0,) * len(tail))
    seq, seq_t = grp(S, LANES), grp(S // LANES, VT_ROWS, LANES)
    return pl.pallas_call(
        functools.partial(_nsa_attn_kernel, tq=tq, n_sel=n_sel, n_grp=n_grp),
        grid=(B, G // n_grp, S // tq),
        in_specs=[qblk, grp(NC, LANES), grp(dh, NC), seq, seq_t, seq, seq_t,
                  pl.BlockSpec((None, tq, LANES), lambda b, g, i: (b, i, 0)),
                  _const_spec(ovt.shape)],
        out_specs=qblk,
        out_shape=jax.ShapeDtypeStruct((B, S, D), BF16),
        scratch_shapes=[pltpu.VMEM((LANES, L), BF16),
                        pltpu.VMEM((LANES, L), BF16),
                        pltpu.VMEM((VT_ROWS, L), F32),
                        pltpu.VMEM((8, L), F32),
                        pltpu.VMEM((VT_ROWS, L), F32),
                        pltpu.VMEM((8, L), F32),
                        pltpu.VMEM((dh, L), F32),
                        pltpu.VMEM((LANES, tq), F32)],
        compiler_params=pltpu.CompilerParams(
            dimension_semantics=("parallel", "parallel", "arbitrary"), vmem_limit_bytes=VMEM_LIMIT),
        name="nsa_attn",
    )(q, kc, vct, ks, vst, kw, vwt, gates, ovt)


def _sgu_kernel(x_ref, g_ref, w_ref, lng_ref, lnb_ref, ws_ref, bs_ref, o_ref, *, width):
    T = SGU_CHUNK
    tm = x_ref.shape[0]
    xn = _rmsnorm(x_ref[...], g_ref[...]).astype(BF16)
    y = jax.nn.gelu(_dot(xn, w_ref[...]))
    u, v = y[:, :width], y[:, width:]
    mu = jnp.mean(v, axis=-1, keepdims=True)
    vc = v - mu
    var = jnp.mean(vc * vc, axis=-1, keepdims=True)
    v = (vc * lax.rsqrt(var + LN_EPS) * lng_ref[...] + lnb_ref[...]).astype(BF16)
    r_i = lax.broadcasted_iota(jnp.int32, (T, T), 0)
    c_i = lax.broadcasted_iota(jnp.int32, (T, T), 1)
    gd = width // SGU_GROUPS
    for gi in range(SGU_GROUPS):
        ws = jnp.where(r_i >= c_i, ws_ref[gi], 0.0).astype(BF16)
        cols = slice(gi * gd, (gi + 1) * gd)
        for c in range(tm // T):
            rws = slice(c * T, (c + 1) * T)
            sv = _dot(ws, v[rws, cols]) + bs_ref[gi]
            o_ref[rws, cols] = (u[rws, cols] * sv).astype(BF16)


def _sgu(x2, g, w_in, ln_g, ln_b, w_s, bs_b, *, tm=256):
    N, D = x2.shape
    W = w_in.shape[1] // 2
    tok = lambda w: pl.BlockSpec((tm, w), lambda i: (i, 0))
    return pl.pallas_call(
        functools.partial(_sgu_kernel, width=W),
        grid=(N // tm,),
        in_specs=[tok(D), _const_spec((1, D)), _const_spec(w_in.shape), _const_spec((1, W)), _const_spec((1, W)),
                  _const_spec(w_s.shape), _const_spec(bs_b.shape)],
        out_specs=tok(W),
        out_shape=jax.ShapeDtypeStruct((N, W), BF16),
        compiler_params=pltpu.CompilerParams(dimension_semantics=("parallel",), vmem_limit_bytes=VMEM_LIMIT),
        name="sgu",
    )(x2, g, w_in, ln_g, ln_b, w_s, bs_b)


def _pool_kernel(x_ref, g_ref, w_ref, wg_ref, sc_ref, o_ref, zbuf, *, hist):
    tm = x_ref.shape[0]
    i = pl.program_id(1)

    @pl.when(i == 0)
    def _():
        zbuf[0:hist, :] = jnp.zeros((hist, zbuf.shape[1]), F32)

    xn = _rmsnorm(x_ref[...], g_ref[...]).astype(BF16)
    z = _dot(xn, w_ref[...])
    zbuf[hist:hist + tm, :] = z
    t = i * tm + lax.broadcasted_iota(jnp.int32, (tm, 1), 0)
    gd = z.shape[1] // len(POOL_WINDOWS)
    for gi, w in enumerate(POOL_WINDOWS):
        cols = slice(gi * gd, (gi + 1) * gd)
        win = z[:, cols]
        for k in range(1, w):
            win = win + zbuf[hist - k:hist - k + tm, cols]
        cnt = jnp.minimum(t + 1, w).astype(F32)
        d = win / cnt - z[:, cols]
        y = _dot(d.astype(BF16), wg_ref[gi]) * sc_ref[:, cols]
        o_ref[:, cols] = y.astype(BF16)
    zbuf[0:hist, :] = zbuf[tm:tm + hist, :]


def _pool(x, g, w_in, w_grp, scale, *, tm=256):
    B, S, D = x.shape
    PW = w_in.shape[1]
    hist = 16
    assert hist >= max(POOL_WINDOWS) - 1 and tm >= hist
    tok = lambda w: pl.BlockSpec((None, tm, w), lambda b, i: (b, i, 0))
    return pl.pallas_call(
        functools.partial(_pool_kernel, hist=hist),
        grid=(B, S // tm),
        in_specs=[tok(D), _const_spec((1, D)), _const_spec(w_in.shape), _const_spec(w_grp.shape),
                  _const_spec((1, PW))],
        out_specs=tok(PW),
        out_shape=jax.ShapeDtypeStruct((B, S, PW), BF16),
        scratch_shapes=[pltpu.VMEM((hist + tm, PW), F32)],
        compiler_params=pltpu.CompilerParams(
            dimension_semantics=("arbitrary", "arbitrary"), vmem_limit_bytes=VMEM_LIMIT),
        name="pool",
    )(x, g, w_in, w_grp, scale)


def _post_kernel(x_ref, u_ref, wo_ref, g_ref, wgu_ref, wd_ref, fg_ref, o_ref, *, hidden, final):
    x1 = x_ref[...] + _dot(u_ref[...], wo_ref[...])
    xn = _rmsnorm(x1, g_ref[...]).astype(BF16)
    gate = _dot(xn, wgu_ref[:, :hidden])
    up = _dot(xn, wgu_ref[:, hidden:])
    a = (jax.nn.silu(gate) * up).astype(BF16)
    x2 = x1 + _dot(a, wd_ref[...])
    if final:
        x2 = _rmsnorm(x2, fg_ref[...])
    o_ref[...] = x2


def _post(x2, u2, w_out, g, w_gu, w_down, fg, *, final, tm=256):
    N, D = x2.shape
    hidden = w_down.shape[0]
    tok = lambda w: pl.BlockSpec((tm, w), lambda i: (i, 0))
    return pl.pallas_call(
        functools.partial(_post_kernel, hidden=hidden, final=final),
        grid=(N // tm,),
        in_specs=[tok(D), tok(u2.shape[1]), _const_spec(w_out.shape), _const_spec((1, D)),
                  _const_spec(w_gu.shape), _const_spec(w_down.shape), _const_spec((1, D))],
        out_specs=tok(D),
        out_shape=jax.ShapeDtypeStruct((N, D), F32),
        compiler_params=pltpu.CompilerParams(dimension_semantics=("parallel",), vmem_limit_bytes=VMEM_LIMIT),
        name="post",
    )(x2, u2, w_out, g, w_gu, w_down, fg)


def _rope_tables(positions):
    half = ROPE_DIM // 2
    inv_freq = ROPE_THETA ** (-jnp.arange(0, ROPE_DIM, 2, dtype=F32) / ROPE_DIM)
    ang = positions.astype(F32)[..., None] * inv_freq
    cos, sin = jnp.cos(ang), jnp.sin(ang)
    z = lambda n: jnp.zeros(ang.shape[:-1] + (n,), F32)
    dh = NSA_HEAD_DIM
    c64 = jnp.concatenate([cos, cos, 1.0 + z(dh - ROPE_DIM)], axis=-1)
    s1 = jnp.concatenate([-sin, z(dh - half)], axis=-1)
    s2 = jnp.concatenate([z(half), sin, z(dh - ROPE_DIM)], axis=-1)
    rep = LANES // dh
    return tuple(jnp.tile(t, (1, 1, rep)) for t in (c64, s1, s2))


def _permute_nsa_w_in(w):
    H, G, dh = NSA_HEADS, NSA_KV_GROUPS, NSA_HEAD_DIM
    base = H * dh
    seg = lambda k: w[:, base + k * G * dh: base + (k + 1) * G * dh]
    kc, vc, ks, vs, kw, vw = (seg(k) for k in range(6))
    gl = w[:, base + 6 * G * dh:]
    inter = lambda a, b: jnp.concatenate(
        [a.reshape(-1, G, 1, dh), b.reshape(-1, G, 1, dh)], axis=2).reshape(-1, 2 * G * dh)
    glp = jnp.pad(gl, ((0, 0), (0, LANES - gl.shape[1])))
    return jnp.concatenate([w[:, :base], inter(ks, vs), inter(kw, vw), inter(kc, vc), glp], axis=1).astype(BF16)


def _overlap_matrix_t(n_chunks, n_sel):
    n = np.arange(n_chunks)[None, :]
    m = np.arange(LANES)[:, None]
    start, end = n * CMP_STRIDE, n * CMP_STRIDE + CMP_BLOCK - 1
    ov = (start <= m * SEL_BLOCK + SEL_BLOCK - 1) & (end >= m * SEL_BLOCK) & (m < n_sel)
    return jnp.asarray(ov, BF16)


def _cmp_weights(k_w1, v_w1):
    dh = NSA_HEAD_DIM
    kw = k_w1.reshape(CMP_BLOCK, dh, -1)
    vw = v_w1.reshape(CMP_BLOCK, dh, -1)
    z = jnp.zeros_like(kw)
    w = jnp.concatenate([jnp.concatenate([kw, z], axis=2), jnp.concatenate([z, vw], axis=2)], axis=1).astype(BF16)
    return w[:CMP_STRIDE], w[CMP_STRIDE:]


def _nsa_layer(x, tabs, g, w_in, pos_k, k_w1, k_w2, pos_v, v_w1, v_w2):
    B, S, D = x.shape
    dh = NSA_HEAD_DIM
    assert CMP_BLOCK == 2 * CMP_STRIDE
    q, ks, vst, kw, vwt, kcv, gates = _nsa_in(x, g.reshape(1, D), _permute_nsa_w_in(w_in), *tabs)
    pos = jnp.concatenate([pos_k, pos_v], axis=1)
    wa, wb = _cmp_weights(k_w1, v_w1)
    pad_r = lambda w2: jnp.pad(w2, ((0, 0), (0, LANES - dh))).astype(BF16)
    pad_l = lambda w2: jnp.pad(w2, ((0, 0), (LANES - dh, 0))).astype(BF16)
    kc, vct = _nsa_cmp(kcv, pos[:CMP_STRIDE], pos[CMP_STRIDE:], wa, wb, pad_r(k_w2), pad_l(v_w2))
    ovt = _overlap_matrix_t(S // CMP_STRIDE, S // SEL_BLOCK)
    return _nsa_attn(q, kc, vct, ks, vst, kw, vwt, gates, ovt)


def kernel(x, positions, mix_norm, ffn_norm, final_norm, nsa_w_in, nsa_cmp_pos_k, nsa_cmp_k_w1, nsa_cmp_k_w2, nsa_cmp_pos_v, nsa_cmp_v_w1, nsa_cmp_v_w2, nsa_w_out, sgu_w_in, sgu_ln_g, sgu_ln_b, sgu_w_s, sgu_b_s, sgu_w_out, pool_w_in, pool_w_grp, pool_scale, pool_w_out, ffn_w_gate_up, ffn_w_down):
    B, S, D = x.shape
    depth = mix_norm.shape[0]
    tabs = _rope_tables(positions)
    row = lambda v: v.reshape(1, -1)
    for i in range(depth):
        kind, j = i % N_MIXERS, i // N_MIXERS
        if kind == 0:
            u = _nsa_layer(x, tabs, mix_norm[i], nsa_w_in[j], nsa_cmp_pos_k[j], nsa_cmp_k_w1[j], nsa_cmp_k_w2[j],
                           nsa_cmp_pos_v[j], nsa_cmp_v_w1[j], nsa_cmp_v_w2[j])
            w_out = nsa_w_out[j]
        elif kind == 1:
            gd = sgu_w_in.shape[-1] // 2 // SGU_GROUPS
            bs_b = jnp.broadcast_to(sgu_b_s[j][:, :, None], sgu_b_s[j].shape + (gd,))
            u = _sgu(x.reshape(B * S, D), row(mix_norm[i]), sgu_w_in[j].astype(BF16), row(sgu_ln_g[j]),
                     row(sgu_ln_b[j]), sgu_w_s[j], bs_b)
            w_out = sgu_w_out[j]
        else:
            u = _pool(x, row(mix_norm[i]), pool_w_in[j].astype(BF16), pool_w_grp[j].astype(BF16),
                      row(pool_scale[j]))
            w_out = pool_w_out[j]
        x = _post(x.reshape(B * S, D), u.reshape(B * S, -1), w_out.astype(BF16), row(ffn_norm[i]),
                  ffn_w_gate_up[i].astype(BF16), ffn_w_down[i].astype(BF16), row(final_norm),
                  final=(i == depth - 1)).reshape(B, S, D)
    return x
```

```python
import functools

import jax
import jax.numpy as jnp
import numpy as np
from jax import lax
from jax.experimental import pallas as pl
from jax.experimental.pallas import tpu as pltpu

RMS_EPS = 1e-6
LN_EPS = 1e-5
NEG_INF = -1e30
FORCE_SCORE = 1e9
LOG2_E = 1.4426950408889634

N_MIXERS = 3
NSA_HEADS = 16
NSA_KV_GROUPS = 4
NSA_HPG = NSA_HEADS // NSA_KV_GROUPS
NSA_HEAD_DIM = 64
CMP_BLOCK = 32
CMP_STRIDE = 16
SEL_BLOCK = 64
SEL_TOPN = 16
WINDOW = 512
ROPE_THETA = 500000.0
ROPE_DIM = NSA_HEAD_DIM // 4
SGU_GROUPS = 8
SGU_CHUNK = 128
POOL_WINDOWS = (2, 4, 8, 16)

LANES = 128
VMEM_LIMIT = 56 * 1024 * 1024

BF16 = jnp.bfloat16
F32 = jnp.float32


def _const_spec(shape):
    nd = len(shape)
    return pl.BlockSpec(shape, lambda *_: (0,) * nd, pipeline_mode=pl.Buffered(1))


def _rmsnorm(x, g):
    ms = jnp.mean(x * x, axis=-1, keepdims=True)
    return x * lax.rsqrt(ms + RMS_EPS) * g


def _dot(a, b):
    return jnp.dot(a, b, preferred_element_type=F32)


VT_ROWS = NSA_HEAD_DIM + 16


def _rope(t, c, s1, s2):
    return t * c + pltpu.roll(t, LANES - ROPE_DIM // 2, axis=1) * s1 + pltpu.roll(t, ROPE_DIM // 2, axis=1) * s2


def _nsa_in_kernel(x_ref, g_ref, w_ref, c_ref, s1_ref, s2_ref,
                   q_ref, ks_ref, vst_ref, kw_ref, vwt_ref, kcv_ref, gate_ref, *, d_model, n_groups):
    tm = x_ref.shape[0]
    dh = NSA_HEAD_DIM
    xn = _rmsnorm(x_ref[...], g_ref[...]).astype(BF16)
    y = _dot(xn, w_ref[...])
    c, s1, s2 = c_ref[...], s1_ref[...], s2_ref[...]
    lane = lax.broadcasted_iota(jnp.int32, c.shape, 1)
    khalf = lane < dh
    ck, s1k, s2k = jnp.where(khalf, c, 1.0), jnp.where(khalf, s1, 0.0), jnp.where(khalf, s2, 0.0)
    scale = dh ** -0.5 * LOG2_E
    for j in range(d_model // LANES):
        sl = slice(j * LANES, (j + 1) * LANES)
        q_ref[:, sl] = (_rope(y[:, sl], c, s1, s2) * scale).astype(BF16)
    ones = jnp.ones((VT_ROWS - dh, LANES), F32)
    tok = pl.program_id(1) * tm + lax.broadcasted_iota(jnp.int32, c.shape, 0)
    blk_onehot = jnp.where(lane - dh == tok // SEL_BLOCK, 1.0, 0.0)
    off = d_model
    for k_ref, vt_ref, tag_blocks in ((ks_ref, vst_ref, True), (kw_ref, vwt_ref, False)):
        for g in range(n_groups):
            slab = _rope(y[:, off + g * LANES: off + (g + 1) * LANES], ck, s1k, s2k)
            k_ref[g] = (jnp.where(khalf, slab, blk_onehot) if tag_blocks else slab).astype(BF16)
            for ch in range(tm // LANES):
                vt = slab[ch * LANES:(ch + 1) * LANES, :].T[dh:, :]
                vt_ref[g, ch] = jnp.concatenate([vt, ones], axis=0).astype(BF16)
        off += n_groups * LANES
    for g in range(n_groups):
        kcv_ref[g] = _rope(y[:, off + g * LANES: off + (g + 1) * LANES], ck, s1k, s2k)
    off += n_groups * LANES
    gate_ref[...] = jax.nn.sigmoid(y[:, off:off + LANES])


def _nsa_in(x, g, w_perm, c_tab, s1_tab, s2_tab, *, tm=512):
    B, S, D = x.shape
    G = NSA_KV_GROUPS
    n_out = w_perm.shape[1]
    tok = lambda w: pl.BlockSpec((None, tm, w), lambda b, i: (b, i, 0))
    grp = pl.BlockSpec((None, G, tm, LANES), lambda b, i: (b, 0, i, 0))
    grp_t = pl.BlockSpec((None, G, tm // LANES, VT_ROWS, LANES), lambda b, i: (b, 0, i, 0, 0))
    k_shape = jax.ShapeDtypeStruct((B, G, S, LANES), BF16)
    vt_shape = jax.ShapeDtypeStruct((B, G, S // LANES, VT_ROWS, LANES), BF16)
    return pl.pallas_call(
        functools.partial(_nsa_in_kernel, d_model=D, n_groups=G),
        grid=(B, S // tm),
        in_specs=[tok(D), _const_spec((1, D)), _const_spec((D, n_out)), tok(LANES), tok(LANES), tok(LANES)],
        out_specs=[tok(D), grp, grp_t, grp, grp_t, grp, tok(LANES)],
        out_shape=[jax.ShapeDtypeStruct((B, S, D), BF16), k_shape, vt_shape, k_shape, vt_shape,
                   jax.ShapeDtypeStruct((B, G, S, LANES), F32),
                   jax.ShapeDtypeStruct((B, S, LANES), F32)],
        compiler_params=pltpu.CompilerParams(
            dimension_semantics=("parallel", "parallel"), vmem_limit_bytes=VMEM_LIMIT),
        name="nsa_in",
    )(x, g, w_perm, c_tab, s1_tab, s2_tab)


def _nsa_cmp_kernel(x_ref, pa_ref, pb_ref, wa_ref, wb_ref, wk2_ref, wv2_ref, kc_ref, vct_ref, *, n_chunks):
    acc_a = jnp.zeros((n_chunks, wa_ref.shape[-1]), F32)
    acc_b = jnp.zeros((n_chunks, wb_ref.shape[-1]), F32)
    for l in range(CMP_STRIDE):
        xl = x_ref[pl.ds(l, n_chunks, stride=CMP_STRIDE), :]
        acc_a = acc_a + _dot((xl + pa_ref[l:l + 1, :]).astype(BF16), wa_ref[l])
        acc_b = acc_b + _dot((xl + pb_ref[l:l + 1, :]).astype(BF16), wb_ref[l])
    hid = jax.nn.gelu(acc_a + pltpu.roll(acc_b, n_chunks - 1, axis=0)).astype(BF16)
    hw = hid.shape[1] // 2
    kv = _dot(hid[:, :hw], wk2_ref[...]) + _dot(hid[:, hw:], wv2_ref[...])
    kc_ref[...] = kv.astype(BF16)
    vct_ref[...] = kv.T[NSA_HEAD_DIM:, :].astype(BF16)


def _nsa_cmp(kcv, pa, pb, wa, wb, wk2x, wv2x):
    B, G, S, _ = kcv.shape
    NC = S // CMP_STRIDE
    return pl.pallas_call(
        functools.partial(_nsa_cmp_kernel, n_chunks=NC),
        grid=(B, G),
        in_specs=[pl.BlockSpec((None, None, S, LANES), lambda b, g: (b, g, 0, 0)),
                  _const_spec(pa.shape), _const_spec(pb.shape), _const_spec(wa.shape), _const_spec(wb.shape),
                  _const_spec(wk2x.shape), _const_spec(wv2x.shape)],
        out_specs=[pl.BlockSpec((None, None, NC, LANES), lambda b, g: (b, g, 0, 0)),
                   pl.BlockSpec((None, None, NSA_HEAD_DIM, NC), lambda b, g: (b, g, 0, 0))],
        out_shape=[jax.ShapeDtypeStruct((B, G, NC, LANES), BF16),
                   jax.ShapeDtypeStruct((B, G, NSA_HEAD_DIM, NC), BF16)],
        compiler_params=pltpu.CompilerParams(
            dimension_semantics=("parallel", "parallel"), vmem_limit_bytes=VMEM_LIMIT),
        name="nsa_cmp",
    )(kcv, pa, pb, wa, wb, wk2x, wv2x)


def _nsa_attn_kernel(q_ref, kc_ref, vct_ref, ks_ref, vst_ref, kw_ref, vwt_ref, gate_ref, ovt_ref, o_ref,
                     q4w, q4s, acc_s, m_s, acc_w, m_w, ocmp, imp_ref, rank_ref, gt, *, tq, n_sel, n_grp, unroll):
    P, dh = NSA_HPG, NSA_HEAD_DIM
    H = n_grp * P
    i = pl.program_id(2)
    qs = i * tq
    head_cols = [slice(h * tq, (h + 1) * tq) for h in range(H)]
    grp_of = [h // P for h in range(H)]

    qt = q_ref[...].astype(F32).T
    zpad = jnp.zeros((LANES - dh, tq), BF16)
    for h, cols in enumerate(head_cols):
        qh = qt[h * dh:(h + 1) * dh].astype(BF16)
        q4w[0:dh, cols] = qh
        q4w[dh:, cols] = zpad
        q4s[0:dh, cols] = qh

    t_q = qs + lax.broadcasted_iota(jnp.int32, (1, tq), 1)

    tk = tq
    n_pre = WINDOW // tk
    sub_i = lax.broadcasted_iota(jnp.int32, (SEL_BLOCK, tq), 0)

    def scores_of(k_ref, k0, q4):
        kts = [k_ref[n, pl.ds(k0, tk), :] for n in range(n_grp)]
        return [_dot(kts[grp_of[h]], q4[:, cols]) for h, cols in enumerate(head_cols)]

    def absorb(scores, vt_ref, k0, acc, m_ref, keep=None, shift=None):
        j0 = k0 // LANES
        vts = [jnp.concatenate([vt_ref[n, j0 + c] for c in range(tk // LANES)], axis=1) for n in range(n_grp)]
        n_slab = tk // SEL_BLOCK
        keeps = None if keep is None else [keep(k0 + b * SEL_BLOCK + sub_i) for b in range(n_slab)]
        for h, cols in enumerate(head_cols):
            slabs = []
            for b in range(n_slab):
                sb = scores[h][b * SEL_BLOCK:(b + 1) * SEL_BLOCK]
                if shift is not None:
                    sb = sb + shift
                if keeps is not None:
                    sb = jnp.where(keeps[b], sb, NEG_INF)
                slabs.append(sb)
            top = slabs[0]
            for sb in slabs[1:]:
                top = jnp.maximum(top, sb)
            m_old = m_ref[0:1, cols]
            m_new = jnp.maximum(m_old, jnp.max(top, axis=0, keepdims=True))
            alpha = jnp.exp2(m_old - m_new)
            pr = jnp.concatenate([jnp.exp2(sb - m_new).astype(BF16) for sb in slabs], axis=0)
            acc[:, cols] = acc[:, cols] * alpha + _dot(vts[grp_of[h]], pr)
            m_ref[0:1, cols] = m_new

    for m_ref, acc in ((m_s, acc_s), (m_w, acc_w)):
        m_ref[...] = jnp.full(m_ref.shape, NEG_INF, F32)
        acc[...] = jnp.zeros(acc.shape, F32)

    nc = kc_ref.shape[1]
    sc_all = [_dot(kc_ref[grp_of[h]], q4w[:, cols]) for h, cols in enumerate(head_cols)]
    win = []
    for d in range(n_pre, -1, -1):
        j = i - d
        k0 = pl.multiple_of(jnp.maximum(j, 0) * tk, tk)
        win.append((d, j, k0, scores_of(kw_ref, k0, q4w)))

    n_i = lax.broadcasted_iota(jnp.int32, (nc, tq), 0)
    cbias = jnp.where(n_i * CMP_STRIDE + (CMP_BLOCK - 1) <= t_q, 0.0, NEG_INF)
    has_c = t_q >= CMP_BLOCK - 1
    pcs = []
    for sc in sc_all:
        sc = sc + cbias
        ec = jnp.exp2(sc - jnp.max(sc, axis=0, keepdims=True))
        pcs.append(ec * jnp.where(has_c, 1.0 / jnp.sum(ec, axis=0, keepdims=True), 0.0))
    for h, cols in enumerate(head_cols):
        ocmp[:, cols] = _dot(vct_ref[grp_of[h]], pcs[h].astype(BF16))

    ovt = ovt_ref[...]
    m_i = lax.broadcasted_iota(jnp.int32, (n_sel, tq), 0)
    cur = jnp.right_shift(t_q, 6)
    forced = (m_i == 0) | (m_i == cur) | (m_i == cur - 1)
    in_past = m_i * SEL_BLOCK <= t_q
    sub8 = lax.broadcasted_iota(jnp.int32, (8, tq), 0)
    for n in range(n_grp):
        psum = (pcs[n * P] + pcs[n * P + 1]) + (pcs[n * P + 2] + pcs[n * P + 3])
        hi = psum.astype(BF16)
        r1 = psum - hi.astype(F32)
        mid = r1.astype(BF16)
        lo = (r1 - mid.astype(F32)).astype(BF16)
        imp = (_dot(ovt, hi) + _dot(ovt, mid) + _dot(ovt, lo))[0:n_sel]
        imp = jnp.where(forced, FORCE_SCORE, imp)
        imp_ref[n] = jnp.where(in_past, imp, NEG_INF)
        rank_ref[n] = jnp.zeros((n_sel, tq), F32)

    causal = lambda kpos: kpos <= t_q
    for d, j, k0, s_win in win:
        if d == n_pre:
            lowest = t_q - WINDOW + jnp.where(j >= 0, 0, 1 << 30)
            absorb(s_win, vwt_ref, k0, acc_w, m_w, keep=lambda kpos: kpos > lowest)
        elif d == 0:
            absorb(s_win, vwt_ref, k0, acc_w, m_w, keep=causal)
        else:
            absorb(s_win, vwt_ref, k0, acc_w, m_w, shift=jnp.where(j >= 0, 0.0, NEG_INF))

    n_rg = n_sel // 8
    last_rg = ((qs + tq - 1) // SEL_BLOCK) // 8
    for rp in range(n_rg):
        @pl.when(rp <= last_rg)
        def _(rp=rp):
            for n in range(n_grp):
                groups = [imp_ref[n, 8 * r:8 * r + 8, :] for r in range(n_rg)]
                ranks = [rank_ref[n, 8 * r:8 * r + 8, :] for r in range(n_rg)]
                for mp in range(8 * rp, 8 * rp + 8):
                    row = imp_ref[n, mp:mp + 1, :]
                    for r, vr in enumerate(groups):
                        if 8 * r > mp:
                            beats = row >= vr
                        elif 8 * r + 7 < mp:
                            beats = row > vr
                        else:
                            beats = (row > vr) | ((row == vr) & (sub8 + 8 * r > mp))
                        ranks[r] = ranks[r] + jnp.where(beats, 1.0, 0.0)
                for r in range(n_rg):
                    rank_ref[n, 8 * r:8 * r + 8, :] = ranks[r]

    for n in range(n_grp):
        sel_bias = jnp.where(rank_ref[n] < float(min(SEL_TOPN, n_sel)), 0.0, NEG_INF)
        if n_sel < LANES - dh:
            sel_bias = jnp.concatenate([sel_bias, jnp.zeros((LANES - dh - n_sel, tq), F32)], axis=0)
        for cols in head_cols[n * P:(n + 1) * P]:
            q4s[dh:, cols] = sel_bias.astype(BF16)

    def sel_tiles(k0s):
        sc = [scores_of(ks_ref, k0, q4s) for k0 in k0s]
        for k0, s in zip(k0s, sc):
            absorb(s, vst_ref, k0, acc_s, m_s)

    def main_body(a, c):
        sel_tiles([pl.multiple_of((a * unroll + u) * tk, tk) for u in range(unroll)])
        return c

    lax.fori_loop(0, i // unroll, main_body, 0)
    bit = unroll // 2
    while bit >= 1:
        first = i - jnp.bitwise_and(i, 2 * bit - 1)

        @pl.when(jnp.bitwise_and(i, bit) != 0)
        def _(first=first, bit=bit):
            sel_tiles([pl.multiple_of((first + u) * tk, tk) for u in range(bit)])

        bit //= 2

    k_diag = pl.multiple_of(qs, tk)
    s_diag = scores_of(ks_ref, k_diag, q4s)
    o_w = acc_w[0:dh, :] * (1.0 / acc_w[dh:dh + 1, :])
    gt[...] = gate_ref[...].T
    absorb(s_diag, vst_ref, k_diag, acc_s, m_s, keep=causal)
    o_s = acc_s[0:dh, :] * (1.0 / acc_s[dh:dh + 1, :])

    h0 = H * pl.program_id(1)
    outs = []
    for h, cols in enumerate(head_cols):
        outs.append(gt[pl.ds(h0 + h, 1), :] * ocmp[:, cols]
                    + gt[pl.ds(NSA_HEADS + h0 + h, 1), :] * o_s[:, cols]
                    + gt[pl.ds(2 * NSA_HEADS + h0 + h, 1), :] * o_w[:, cols])
    o_ref[...] = jnp.concatenate(outs, axis=0).T.astype(BF16)


def _nsa_attn(q, kc, vct, ks, vst, kw, vwt, gates, ovt, *, tq=256, n_grp=2, unroll=8):
    B, S, D = q.shape
    G, P, dh = NSA_KV_GROUPS, NSA_HPG, NSA_HEAD_DIM
    NC = kc.shape[2]
    n_sel = S // SEL_BLOCK
    assert tq % LANES == 0 and tq & (tq - 1) == 0 and WINDOW % tq == 0 and G % n_grp == 0
    assert n_sel % 16 == 0 and n_sel <= LANES - dh
    L = n_grp * P * tq
    qblk = pl.BlockSpec((None, tq, n_grp * P * dh), lambda b, g, i: (b, i, g))
    grp = lambda *tail: pl.BlockSpec((None, n_grp) + tail, lambda b, g, i: (b, g) + (0,) * len(tail))
    seq, seq_t = grp(S, LANES), grp(S // LANES, VT_ROWS, LANES)
    return pl.pallas_call(
        functools.partial(_nsa_attn_kernel, tq=tq, n_sel=n_sel, n_grp=n_grp, unroll=unroll),
        grid=(B, G // n_grp, S // tq),
        in_specs=[qblk, grp(NC, LANES), grp(dh, NC), seq, seq_t, seq, seq_t,
                  pl.BlockSpec((None, tq, LANES), lambda b, g, i: (b, i, 0)),
                  _const_spec(ovt.shape)],
        out_specs=qblk,
        out_shape=jax.ShapeDtypeStruct((B, S, D), BF16),
        scratch_shapes=[pltpu.VMEM((LANES, L), BF16),
                        pltpu.VMEM((LANES, L), BF16),
                        pltpu.VMEM((VT_ROWS, L), F32),
                        pltpu.VMEM((8, L), F32),
                        pltpu.VMEM((VT_ROWS, L), F32),
                        pltpu.VMEM((8, L), F32),
                        pltpu.VMEM((dh, L), F32),
                        pltpu.VMEM((n_grp, n_sel, tq), F32),
                        pltpu.VMEM((n_grp, n_sel, tq), F32),
                        pltpu.VMEM((LANES, tq), F32)],
        compiler_params=pltpu.CompilerParams(
            dimension_semantics=("parallel", "parallel", "arbitrary"), vmem_limit_bytes=VMEM_LIMIT),
        name="nsa_attn",
    )(q, kc, vct, ks, vst, kw, vwt, gates, ovt)


def _sgu_kernel(x_ref, g_ref, w_ref, lng_ref, lnb_ref, ws_ref, bs_ref, o_ref, *, width):
    T = SGU_CHUNK
    tm = x_ref.shape[0]
    xn = _rmsnorm(x_ref[...], g_ref[...]).astype(BF16)
    y = jax.nn.gelu(_dot(xn, w_ref[...]))
    u, v = y[:, :width], y[:, width:]
    mu = jnp.mean(v, axis=-1, keepdims=True)
    vc = v - mu
    var = jnp.mean(vc * vc, axis=-1, keepdims=True)
    v = (vc * lax.rsqrt(var + LN_EPS) * lng_ref[...] + lnb_ref[...]).astype(BF16)
    r_i = lax.broadcasted_iota(jnp.int32, (T, T), 0)
    c_i = lax.broadcasted_iota(jnp.int32, (T, T), 1)
    gd = width // SGU_GROUPS
    for gi in range(SGU_GROUPS):
        ws = jnp.where(r_i >= c_i, ws_ref[gi], 0.0).astype(BF16)
        cols = slice(gi * gd, (gi + 1) * gd)
        for c in range(tm // T):
            rws = slice(c * T, (c + 1) * T)
            sv = _dot(ws, v[rws, cols]) + bs_ref[gi]
            o_ref[rws, cols] = (u[rws, cols] * sv).astype(BF16)


def _sgu(x2, g, w_in, ln_g, ln_b, w_s, bs_b, *, tm=256):
    N, D = x2.shape
    W = w_in.shape[1] // 2
    tok = lambda w: pl.BlockSpec((tm, w), lambda i: (i, 0))
    return pl.pallas_call(
        functools.partial(_sgu_kernel, width=W),
        grid=(N // tm,),
        in_specs=[tok(D), _const_spec((1, D)), _const_spec(w_in.shape), _const_spec((1, W)), _const_spec((1, W)),
                  _const_spec(w_s.shape), _const_spec(bs_b.shape)],
        out_specs=tok(W),
        out_shape=jax.ShapeDtypeStruct((N, W), BF16),
        compiler_params=pltpu.CompilerParams(dimension_semantics=("parallel",), vmem_limit_bytes=VMEM_LIMIT),
        name="sgu",
    )(x2, g, w_in, ln_g, ln_b, w_s, bs_b)


def _pool_kernel(x_ref, g_ref, w_ref, wg_ref, sc_ref, o_ref, zbuf, *, hist):
    tm = x_ref.shape[0]
    i = pl.program_id(1)

    @pl.when(i == 0)
    def _():
        zbuf[0:hist, :] = jnp.zeros((hist, zbuf.shape[1]), F32)

    xn = _rmsnorm(x_ref[...], g_ref[...]).astype(BF16)
    z = _dot(xn, w_ref[...])
    zbuf[hist:hist + tm, :] = z
    t = i * tm + lax.broadcasted_iota(jnp.int32, (tm, 1), 0)
    gd = z.shape[1] // len(POOL_WINDOWS)
    for gi, w in enumerate(POOL_WINDOWS):
        cols = slice(gi * gd, (gi + 1) * gd)
        win = z[:, cols]
        for k in range(1, w):
            win = win + zbuf[hist - k:hist - k + tm, cols]
        cnt = jnp.minimum(t + 1, w).astype(F32)
        d = win / cnt - z[:, cols]
        y = _dot(d.astype(BF16), wg_ref[gi]) * sc_ref[:, cols]
        o_ref[:, cols] = y.astype(BF16)
    zbuf[0:hist, :] = zbuf[tm:tm + hist, :]


def _pool(x, g, w_in, w_grp, scale, *, tm=256):
    B, S, D = x.shape
    PW = w_in.shape[1]
    hist = 16
    assert hist >= max(POOL_WINDOWS) - 1 and tm >= hist
    tok = lambda w: pl.BlockSpec((None, tm, w), lambda b, i: (b, i, 0))
    return pl.pallas_call(
        functools.partial(_pool_kernel, hist=hist),
        grid=(B, S // tm),
        in_specs=[tok(D), _const_spec((1, D)), _const_spec(w_in.shape), _const_spec(w_grp.shape),
                  _const_spec((1, PW))],
        out_specs=tok(PW),
        out_shape=jax.ShapeDtypeStruct((B, S, PW), BF16),
        scratch_shapes=[pltpu.VMEM((hist + tm, PW), F32)],
        compiler_params=pltpu.CompilerParams(
            dimension_semantics=("arbitrary", "arbitrary"), vmem_limit_bytes=VMEM_LIMIT),
        name="pool",
    )(x, g, w_in, w_grp, scale)


def _post_kernel(x_ref, u_ref, wo_ref, g_ref, wgu_ref, wd_ref, fg_ref, o_ref, *, hidden, final):
    x1 = x_ref[...] + _dot(u_ref[...], wo_ref[...])
    xn = _rmsnorm(x1, g_ref[...]).astype(BF16)
    gate = _dot(xn, wgu_ref[:, :hidden])
    up = _dot(xn, wgu_ref[:, hidden:])
    a = (jax.nn.silu(gate) * up).astype(BF16)
    x2 = x1 + _dot(a, wd_ref[...])
    if final:
        x2 = _rmsnorm(x2, fg_ref[...])
    o_ref[...] = x2


def _post(x2, u2, w_out, g, w_gu, w_down, fg, *, final, tm=256):
    N, D = x2.shape
    hidden = w_down.shape[0]
    tok = lambda w: pl.BlockSpec((tm, w), lambda i: (i, 0))
    return pl.pallas_call(
        functools.partial(_post_kernel, hidden=hidden, final=final),
        grid=(N // tm,),
        in_specs=[tok(D), tok(u2.shape[1]), _const_spec(w_out.shape), _const_spec((1, D)),
                  _const_spec(w_gu.shape), _const_spec(w_down.shape), _const_spec((1, D))],
        out_specs=tok(D),
        out_shape=jax.ShapeDtypeStruct((N, D), F32),
        compiler_params=pltpu.CompilerParams(dimension_semantics=("parallel",), vmem_limit_bytes=VMEM_LIMIT),
        name="post",
    )(x2, u2, w_out, g, w_gu, w_down, fg)


def _rope_tables(positions):
    half = ROPE_DIM // 2
    inv_freq = ROPE_THETA ** (-jnp.arange(0, ROPE_DIM, 2, dtype=F32) / ROPE_DIM)
    ang = positions.astype(F32)[..., None] * inv_freq
    cos, sin = jnp.cos(ang), jnp.sin(ang)
    z = lambda n: jnp.zeros(ang.shape[:-1] + (n,), F32)
    dh = NSA_HEAD_DIM
    c64 = jnp.concatenate([cos, cos, 1.0 + z(dh - ROPE_DIM)], axis=-1)
    s1 = jnp.concatenate([-sin, z(dh - half)], axis=-1)
    s2 = jnp.concatenate([z(half), sin, z(dh - ROPE_DIM)], axis=-1)
    rep = LANES // dh
    return tuple(jnp.tile(t, (1, 1, rep)) for t in (c64, s1, s2))


def _permute_nsa_w_in(w):
    H, G, dh = NSA_HEADS, NSA_KV_GROUPS, NSA_HEAD_DIM
    base = H * dh
    seg = lambda k: w[:, base + k * G * dh: base + (k + 1) * G * dh]
    kc, vc, ks, vs, kw, vw = (seg(k) for k in range(6))
    gl = w[:, base + 6 * G * dh:]
    inter = lambda a, b: jnp.concatenate(
        [a.reshape(-1, G, 1, dh), b.reshape(-1, G, 1, dh)], axis=2).reshape(-1, 2 * G * dh)
    glp = jnp.pad(gl, ((0, 0), (0, LANES - gl.shape[1])))
    return jnp.concatenate([w[:, :base], inter(ks, vs), inter(kw, vw), inter(kc, vc), glp], axis=1).astype(BF16)


def _overlap_matrix_t(n_chunks, n_sel):
    n = np.arange(n_chunks)[None, :]
    m = np.arange(LANES)[:, None]
    start, end = n * CMP_STRIDE, n * CMP_STRIDE + CMP_BLOCK - 1
    ov = (start <= m * SEL_BLOCK + SEL_BLOCK - 1) & (end >= m * SEL_BLOCK) & (m < n_sel)
    return jnp.asarray(ov, BF16)


def _cmp_weights(k_w1, v_w1):
    dh = NSA_HEAD_DIM
    kw = k_w1.reshape(CMP_BLOCK, dh, -1)
    vw = v_w1.reshape(CMP_BLOCK, dh, -1)
    z = jnp.zeros_like(kw)
    w = jnp.concatenate([jnp.concatenate([kw, z], axis=2), jnp.concatenate([z, vw], axis=2)], axis=1).astype(BF16)
    return w[:CMP_STRIDE], w[CMP_STRIDE:]


def _nsa_layer(x, tabs, g, w_in, pos_k, k_w1, k_w2, pos_v, v_w1, v_w2):
    B, S, D = x.shape
    dh = NSA_HEAD_DIM
    assert CMP_BLOCK == 2 * CMP_STRIDE
    q, ks, vst, kw, vwt, kcv, gates = _nsa_in(x, g.reshape(1, D), _permute_nsa_w_in(w_in), *tabs)
    pos = jnp.concatenate([pos_k, pos_v], axis=1)
    wa, wb = _cmp_weights(k_w1, v_w1)
    pad_r = lambda w2: jnp.pad(w2, ((0, 0), (0, LANES - dh))).astype(BF16)
    pad_l = lambda w2: jnp.pad(w2, ((0, 0), (LANES - dh, 0))).astype(BF16)
    kc, vct = _nsa_cmp(kcv, pos[:CMP_STRIDE], pos[CMP_STRIDE:], wa, wb, pad_r(k_w2), pad_l(v_w2))
    ovt = _overlap_matrix_t(S // CMP_STRIDE, S // SEL_BLOCK)
    return _nsa_attn(q, kc, vct, ks, vst, kw, vwt, gates, ovt)


def kernel(x, positions, mix_norm, ffn_norm, final_norm, nsa_w_in, nsa_cmp_pos_k, nsa_cmp_k_w1, nsa_cmp_k_w2, nsa_cmp_pos_v, nsa_cmp_v_w1, nsa_cmp_v_w2, nsa_w_out, sgu_w_in, sgu_ln_g, sgu_ln_b, sgu_w_s, sgu_b_s, sgu_w_out, pool_w_in, pool_w_grp, pool_scale, pool_w_out, ffn_w_gate_up, ffn_w_down):
    B, S, D = x.shape
    depth = mix_norm.shape[0]
    tabs = _rope_tables(positions)
    row = lambda v: v.reshape(1, -1)
    for i in range(depth):
        kind, j = i % N_MIXERS, i // N_MIXERS
        if kind == 0:
            u = _nsa_layer(x, tabs, mix_norm[i], nsa_w_in[j], nsa_cmp_pos_k[j], nsa_cmp_k_w1[j], nsa_cmp_k_w2[j],
                           nsa_cmp_pos_v[j], nsa_cmp_v_w1[j], nsa_cmp_v_w2[j])
            w_out = nsa_w_out[j]
        elif kind == 1:
            gd = sgu_w_in.shape[-1] // 2 // SGU_GROUPS
            bs_b = jnp.broadcast_to(sgu_b_s[j][:, :, None], sgu_b_s[j].shape + (gd,))
            u = _sgu(x.reshape(B * S, D), row(mix_norm[i]), sgu_w_in[j].astype(BF16), row(sgu_ln_g[j]),
                     row(sgu_ln_b[j]), sgu_w_s[j], bs_b)
            w_out = sgu_w_out[j]
        else:
            u = _pool(x, row(mix_norm[i]), pool_w_in[j].astype(BF16), pool_w_grp[j].astype(BF16),
                      row(pool_scale[j]))
            w_out = pool_w_out[j]
        x = _post(x.reshape(B * S, D), u.reshape(B * S, -1), w_out.astype(BF16), row(ffn_norm[i]),
                  ffn_w_gate_up[i].astype(BF16), ffn_w_down[i].astype(BF16), row(final_norm),
                  final=(i == depth - 1)).reshape(B, S, D)
    return x
```

```python
import functools

import jax
import jax.numpy as jnp
import numpy as np
from jax import lax
from jax.experimental import pallas as pl
from jax.experimental.pallas import tpu as pltpu

RMS_EPS = 1e-6
LN_EPS = 1e-5
NEG_INF = -1e30
FORCE_SCORE = 1e9
LOG2_E = 1.4426950408889634
NO_KEY = 1 << 30

N_MIXERS = 3
NSA_HEADS = 16
NSA_KV_GROUPS = 4
NSA_HPG = NSA_HEADS // NSA_KV_GROUPS
NSA_HEAD_DIM = 64
CMP_BLOCK = 32
CMP_STRIDE = 16
SEL_BLOCK = 64
SEL_TOPN = 16
WINDOW = 512
ROPE_THETA = 500000.0
ROPE_DIM = NSA_HEAD_DIM // 4
SGU_GROUPS = 8
SGU_CHUNK = 128
POOL_WINDOWS = (2, 4, 8, 16)

LANES = 128
VMEM_LIMIT = 56 * 1024 * 1024

BF16 = jnp.bfloat16
F32 = jnp.float32


def _const_spec(shape):
    nd = len(shape)
    return pl.BlockSpec(shape, lambda *_: (0,) * nd, pipeline_mode=pl.Buffered(1))


def _rmsnorm(x, g):
    ms = jnp.mean(x * x, axis=-1, keepdims=True)
    return x * lax.rsqrt(ms + RMS_EPS) * g


def _dot(a, b):
    return jnp.dot(a, b, preferred_element_type=F32)


VT_ROWS = NSA_HEAD_DIM + 16


def _rope(t, c, s1, s2):
    return t * c + pltpu.roll(t, LANES - ROPE_DIM // 2, axis=1) * s1 + pltpu.roll(t, ROPE_DIM // 2, axis=1) * s2


def _nsa_in_kernel(x_ref, g_ref, w_ref, c_ref, s1_ref, s2_ref,
                   q_ref, ks_ref, vst_ref, kw_ref, vwt_ref, kcv_ref, gate_ref, *, d_model, n_groups):
    tm = x_ref.shape[0]
    dh = NSA_HEAD_DIM
    xn = _rmsnorm(x_ref[...], g_ref[...]).astype(BF16)
    y = _dot(xn, w_ref[...])
    c, s1, s2 = c_ref[...], s1_ref[...], s2_ref[...]
    lane = lax.broadcasted_iota(jnp.int32, c.shape, 1)
    khalf = lane < dh
    ck, s1k, s2k = jnp.where(khalf, c, 1.0), jnp.where(khalf, s1, 0.0), jnp.where(khalf, s2, 0.0)
    scale = dh ** -0.5 * LOG2_E
    for j in range(d_model // LANES):
        sl = slice(j * LANES, (j + 1) * LANES)
        q_ref[:, sl] = (_rope(y[:, sl], c, s1, s2) * scale).astype(BF16)
    ones = jnp.ones((VT_ROWS - dh, LANES), F32)
    tok = pl.program_id(1) * tm + lax.broadcasted_iota(jnp.int32, c.shape, 0)
    blk_onehot = jnp.where(lane - dh == tok // SEL_BLOCK, 1.0, 0.0)
    off = d_model
    for k_ref, vt_ref, tag_blocks in ((ks_ref, vst_ref, True), (kw_ref, vwt_ref, False)):
        for g in range(n_groups):
            slab = _rope(y[:, off + g * LANES: off + (g + 1) * LANES], ck, s1k, s2k)
            k_ref[g] = (jnp.where(khalf, slab, blk_onehot) if tag_blocks else slab).astype(BF16)
            for ch in range(tm // LANES):
                vt = slab[ch * LANES:(ch + 1) * LANES, :].T[dh:, :]
                vt_ref[g, ch] = jnp.concatenate([vt, ones], axis=0).astype(BF16)
        off += n_groups * LANES
    for g in range(n_groups):
        kcv_ref[g] = _rope(y[:, off + g * LANES: off + (g + 1) * LANES], ck, s1k, s2k)
    off += n_groups * LANES
    gate_ref[...] = jax.nn.sigmoid(y[:, off:off + LANES])


def _nsa_in(x, g, w_perm, c_tab, s1_tab, s2_tab, *, tm=512):
    B, S, D = x.shape
    G = NSA_KV_GROUPS
    n_out = w_perm.shape[1]
    tok = lambda w: pl.BlockSpec((None, tm, w), lambda b, i: (b, i, 0))
    grp = pl.BlockSpec((None, G, tm, LANES), lambda b, i: (b, 0, i, 0))
    grp_t = pl.BlockSpec((None, G, tm // LANES, VT_ROWS, LANES), lambda b, i: (b, 0, i, 0, 0))
    k_shape = jax.ShapeDtypeStruct((B, G, S, LANES), BF16)
    vt_shape = jax.ShapeDtypeStruct((B, G, S // LANES, VT_ROWS, LANES), BF16)
    return pl.pallas_call(
        functools.partial(_nsa_in_kernel, d_model=D, n_groups=G),
        grid=(B, S // tm),
        in_specs=[tok(D), _const_spec((1, D)), _const_spec((D, n_out)), tok(LANES), tok(LANES), tok(LANES)],
        out_specs=[tok(D), grp, grp_t, grp, grp_t, grp, tok(LANES)],
        out_shape=[jax.ShapeDtypeStruct((B, S, D), BF16), k_shape, vt_shape, k_shape, vt_shape,
                   jax.ShapeDtypeStruct((B, G, S, LANES), F32),
                   jax.ShapeDtypeStruct((B, S, LANES), F32)],
        compiler_params=pltpu.CompilerParams(
            dimension_semantics=("parallel", "parallel"), vmem_limit_bytes=VMEM_LIMIT),
        name="nsa_in",
    )(x, g, w_perm, c_tab, s1_tab, s2_tab)


def _nsa_cmp_kernel(x_ref, pa_ref, pb_ref, wa_ref, wb_ref, wk2_ref, wv2_ref, kc_ref, vct_ref, *, n_chunks):
    acc_a = jnp.zeros((n_chunks, wa_ref.shape[-1]), F32)
    acc_b = jnp.zeros((n_chunks, wb_ref.shape[-1]), F32)
    for l in range(CMP_STRIDE):
        xl = x_ref[pl.ds(l, n_chunks, stride=CMP_STRIDE), :]
        acc_a = acc_a + _dot((xl + pa_ref[l:l + 1, :]).astype(BF16), wa_ref[l])
        acc_b = acc_b + _dot((xl + pb_ref[l:l + 1, :]).astype(BF16), wb_ref[l])
    hid = jax.nn.gelu(acc_a + pltpu.roll(acc_b, n_chunks - 1, axis=0)).astype(BF16)
    hw = hid.shape[1] // 2
    kv = _dot(hid[:, :hw], wk2_ref[...]) + _dot(hid[:, hw:], wv2_ref[...])
    kc_ref[...] = kv.astype(BF16)
    vct_ref[...] = kv.T[NSA_HEAD_DIM:, :].astype(BF16)


def _nsa_cmp(kcv, pa, pb, wa, wb, wk2x, wv2x):
    B, G, S, _ = kcv.shape
    NC = S // CMP_STRIDE
    return pl.pallas_call(
        functools.partial(_nsa_cmp_kernel, n_chunks=NC),
        grid=(B, G),
        in_specs=[pl.BlockSpec((None, None, S, LANES), lambda b, g: (b, g, 0, 0)),
                  _const_spec(pa.shape), _const_spec(pb.shape), _const_spec(wa.shape), _const_spec(wb.shape),
                  _const_spec(wk2x.shape), _const_spec(wv2x.shape)],
        out_specs=[pl.BlockSpec((None, None, NC, LANES), lambda b, g: (b, g, 0, 0)),
                   pl.BlockSpec((None, None, NSA_HEAD_DIM, NC), lambda b, g: (b, g, 0, 0))],
        out_shape=[jax.ShapeDtypeStruct((B, G, NC, LANES), BF16),
                   jax.ShapeDtypeStruct((B, G, NSA_HEAD_DIM, NC), BF16)],
        compiler_params=pltpu.CompilerParams(
            dimension_semantics=("parallel", "parallel"), vmem_limit_bytes=VMEM_LIMIT),
        name="nsa_cmp",
    )(kcv, pa, pb, wa, wb, wk2x, wv2x)


def _nsa_attn_kernel(q_ref, kc_ref, vct_ref, ks_ref, vst_ref, kw_ref, vwt_ref, gate_ref, ovt_ref, o_ref,
                     q4w, q4s, acc_s, m_s, acc_w, m_w, ocmp, imp_ref, rank_ref, gt, *, tq, n_sel, n_grp, unroll):
    P, dh = NSA_HPG, NSA_HEAD_DIM
    H = n_grp * P
    i = pl.program_id(2)
    qs = i * tq
    head_cols = [slice(h * tq, (h + 1) * tq) for h in range(H)]
    grp_of = [h // P for h in range(H)]

    qt = q_ref[...].astype(F32).T
    zpad = jnp.zeros((LANES - dh, tq), BF16)
    for h, cols in enumerate(head_cols):
        qh = qt[h * dh:(h + 1) * dh].astype(BF16)
        q4w[0:dh, cols] = qh
        q4w[dh:, cols] = zpad
        q4s[0:dh, cols] = qh

    t_q = qs + lax.broadcasted_iota(jnp.int32, (1, tq), 1)

    tk = tq
    n_pre = WINDOW // tk
    sub_i = lax.broadcasted_iota(jnp.int32, (SEL_BLOCK, tq), 0)

    def scores_of(k_ref, k0, q4):
        kts = [k_ref[n, pl.ds(k0, tk), :] for n in range(n_grp)]
        return [_dot(kts[grp_of[h]], q4[:, cols]) for h, cols in enumerate(head_cols)]

    def absorb(scores, vt_ref, k0, acc, m_ref, keep=None, shift=None):
        j0 = k0 // LANES
        vts = [jnp.concatenate([vt_ref[n, j0 + c] for c in range(tk // LANES)], axis=1) for n in range(n_grp)]
        n_slab = tk // SEL_BLOCK
        keeps = None if keep is None else [keep(k0 + b * SEL_BLOCK + sub_i) for b in range(n_slab)]
        for h, cols in enumerate(head_cols):
            slabs = []
            for b in range(n_slab):
                sb = scores[h][b * SEL_BLOCK:(b + 1) * SEL_BLOCK]
                if shift is not None:
                    sb = sb + shift
                if keeps is not None:
                    sb = jnp.where(keeps[b], sb, NEG_INF)
                slabs.append(sb)
            top = slabs[0]
            for sb in slabs[1:]:
                top = jnp.maximum(top, sb)
            m_old = m_ref[0:1, cols]
            m_new = jnp.maximum(m_old, jnp.max(top, axis=0, keepdims=True))
            alpha = jnp.exp2(m_old - m_new)
            pr = jnp.concatenate([jnp.exp2(sb - m_new).astype(BF16) for sb in slabs], axis=0)
            acc[:, cols] = acc[:, cols] * alpha + _dot(vts[grp_of[h]], pr)
            m_ref[0:1, cols] = m_new

    for m_ref, acc in ((m_s, acc_s), (m_w, acc_w)):
        m_ref[...] = jnp.full(m_ref.shape, NEG_INF, F32)
        acc[...] = jnp.zeros(acc.shape, F32)

    nc = kc_ref.shape[1]
    sc_all = [_dot(kc_ref[grp_of[h]], q4w[:, cols]) for h, cols in enumerate(head_cols)]
    win = []
    for d in range(n_pre, -1, -1):
        j = i - d
        k0 = pl.multiple_of(jnp.maximum(j, 0) * tk, tk)
        win.append((d, j, k0, scores_of(kw_ref, k0, q4w)))

    n_i = lax.broadcasted_iota(jnp.int32, (nc, tq), 0)
    cbias = jnp.where(n_i * CMP_STRIDE + (CMP_BLOCK - 1) <= t_q, 0.0, NEG_INF)
    has_c = t_q >= CMP_BLOCK - 1
    pcs = []
    for sc in sc_all:
        sc = sc + cbias
        ec = jnp.exp2(sc - jnp.max(sc, axis=0, keepdims=True))
        pcs.append(ec * jnp.where(has_c, 1.0 / jnp.sum(ec, axis=0, keepdims=True), 0.0))
    for h, cols in enumerate(head_cols):
        ocmp[:, cols] = _dot(vct_ref[grp_of[h]], pcs[h].astype(BF16))

    ovt = ovt_ref[...]
    m_i = lax.broadcasted_iota(jnp.int32, (n_sel, tq), 0)
    cur = jnp.right_shift(t_q, SEL_BLOCK.bit_length() - 1)
    forced = (m_i == 0) | (m_i == cur) | (m_i == cur - 1)
    in_past = m_i * SEL_BLOCK <= t_q
    sub8 = lax.broadcasted_iota(jnp.int32, (8, tq), 0)
    for n in range(n_grp):
        psum = (pcs[n * P] + pcs[n * P + 1]) + (pcs[n * P + 2] + pcs[n * P + 3])
        hi = psum.astype(BF16)
        r1 = psum - hi.astype(F32)
        mid = r1.astype(BF16)
        lo = (r1 - mid.astype(F32)).astype(BF16)
        imp = (_dot(ovt, hi) + _dot(ovt, mid) + _dot(ovt, lo))[0:n_sel]
        imp = jnp.where(forced, FORCE_SCORE, imp)
        imp_ref[n] = jnp.where(in_past, imp, NEG_INF)
        rank_ref[n] = jnp.zeros((n_sel, tq), F32)

    causal = lambda kpos: kpos <= t_q
    for d, j, k0, s_win in win:
        if d == n_pre:
            lowest = t_q - WINDOW + jnp.where(j >= 0, 0, NO_KEY)
            absorb(s_win, vwt_ref, k0, acc_w, m_w, keep=lambda kpos: kpos > lowest)
        elif d == 0:
            absorb(s_win, vwt_ref, k0, acc_w, m_w, keep=causal)
        else:
            absorb(s_win, vwt_ref, k0, acc_w, m_w, shift=jnp.where(j >= 0, 0.0, NEG_INF))

    n_rg = n_sel // 8
    last_rg = ((qs + tq - 1) // SEL_BLOCK) // 8
    for rp in range(n_rg):
        @pl.when(rp <= last_rg)
        def _(rp=rp):
            for n in range(n_grp):
                groups = [imp_ref[n, 8 * r:8 * r + 8, :] for r in range(n_rg)]
                ranks = [rank_ref[n, 8 * r:8 * r + 8, :] for r in range(n_rg)]
                for mp in range(8 * rp, 8 * rp + 8):
                    row = imp_ref[n, mp:mp + 1, :]
                    for r, vr in enumerate(groups):
                        if 8 * r > mp:
                            beats = row >= vr
                        elif 8 * r + 7 < mp:
                            beats = row > vr
                        else:
                            beats = (row > vr) | ((row == vr) & (sub8 + 8 * r > mp))
                        ranks[r] = ranks[r] + jnp.where(beats, 1.0, 0.0)
                for r in range(n_rg):
                    rank_ref[n, 8 * r:8 * r + 8, :] = ranks[r]

    for n in range(n_grp):
        sel_bias = jnp.where(rank_ref[n] < float(min(SEL_TOPN, n_sel)), 0.0, NEG_INF)
        if n_sel < LANES - dh:
            sel_bias = jnp.concatenate([sel_bias, jnp.zeros((LANES - dh - n_sel, tq), F32)], axis=0)
        for cols in head_cols[n * P:(n + 1) * P]:
            q4s[dh:, cols] = sel_bias.astype(BF16)

    def sel_tiles(k0s):
        sc = [scores_of(ks_ref, k0, q4s) for k0 in k0s]
        for k0, s in zip(k0s, sc):
            absorb(s, vst_ref, k0, acc_s, m_s)

    def main_body(a, c):
        sel_tiles([pl.multiple_of((a * unroll + u) * tk, tk) for u in range(unroll)])
        return c

    lax.fori_loop(0, i // unroll, main_body, 0)
    bit = unroll // 2
    while bit >= 1:
        first = i - jnp.bitwise_and(i, 2 * bit - 1)

        @pl.when(jnp.bitwise_and(i, bit) != 0)
        def _(first=first, bit=bit):
            sel_tiles([pl.multiple_of((first + u) * tk, tk) for u in range(bit)])

        bit //= 2

    k_diag = pl.multiple_of(qs, tk)
    s_diag = scores_of(ks_ref, k_diag, q4s)
    o_w = acc_w[0:dh, :] * (1.0 / acc_w[dh:dh + 1, :])
    gt[...] = gate_ref[...].T
    absorb(s_diag, vst_ref, k_diag, acc_s, m_s, keep=causal)
    o_s = acc_s[0:dh, :] * (1.0 / acc_s[dh:dh + 1, :])

    h0 = H * pl.program_id(1)
    outs = []
    for h, cols in enumerate(head_cols):
        outs.append(gt[pl.ds(h0 + h, 1), :] * ocmp[:, cols]
                    + gt[pl.ds(NSA_HEADS + h0 + h, 1), :] * o_s[:, cols]
                    + gt[pl.ds(2 * NSA_HEADS + h0 + h, 1), :] * o_w[:, cols])
    o_ref[...] = jnp.concatenate(outs, axis=0).T.astype(BF16)


def _nsa_attn(q, kc, vct, ks, vst, kw, vwt, gates, ovt, *, tq=256, n_grp=2, unroll=8):
    B, S, D = q.shape
    G, P, dh = NSA_KV_GROUPS, NSA_HPG, NSA_HEAD_DIM
    NC = kc.shape[2]
    n_sel = S // SEL_BLOCK
    assert tq % LANES == 0 and tq & (tq - 1) == 0 and WINDOW % tq == 0 and G % n_grp == 0
    assert n_sel % 16 == 0 and n_sel <= LANES - dh and SEL_BLOCK & (SEL_BLOCK - 1) == 0
    L = n_grp * P * tq
    qblk = pl.BlockSpec((None, tq, n_grp * P * dh), lambda b, g, i: (b, i, g))
    grp = lambda *tail: pl.BlockSpec((None, n_grp) + tail, lambda b, g, i: (b, g) + (0,) * len(tail))
    seq, seq_t = grp(S, LANES), grp(S // LANES, VT_ROWS, LANES)
    return pl.pallas_call(
        functools.partial(_nsa_attn_kernel, tq=tq, n_sel=n_sel, n_grp=n_grp, unroll=unroll),
        grid=(B, G // n_grp, S // tq),
        in_specs=[qblk, grp(NC, LANES), grp(dh, NC), seq, seq_t, seq, seq_t,
                  pl.BlockSpec((None, tq, LANES), lambda b, g, i: (b, i, 0)),
                  _const_spec(ovt.shape)],
        out_specs=qblk,
        out_shape=jax.ShapeDtypeStruct((B, S, D), BF16),
        scratch_shapes=[pltpu.VMEM((LANES, L), BF16),
                        pltpu.VMEM((LANES, L), BF16),
                        pltpu.VMEM((VT_ROWS, L), F32),
                        pltpu.VMEM((8, L), F32),
                        pltpu.VMEM((VT_ROWS, L), F32),
                        pltpu.VMEM((8, L), F32),
                        pltpu.VMEM((dh, L), F32),
                        pltpu.VMEM((n_grp, n_sel, tq), F32),
                        pltpu.VMEM((n_grp, n_sel, tq), F32),
                        pltpu.VMEM((LANES, tq), F32)],
        compiler_params=pltpu.CompilerParams(
            dimension_semantics=("parallel", "parallel", "arbitrary"), vmem_limit_bytes=VMEM_LIMIT),
        name="nsa_attn",
    )(q, kc, vct, ks, vst, kw, vwt, gates, ovt)


def _sgu_kernel(x_ref, g_ref, w_ref, lng_ref, lnb_ref, ws_ref, bs_ref, o_ref, *, width):
    T = SGU_CHUNK
    tm = x_ref.shape[0]
    xn = _rmsnorm(x_ref[...], g_ref[...]).astype(BF16)
    y = jax.nn.gelu(_dot(xn, w_ref[...]))
    u, v = y[:, :width], y[:, width:]
    mu = jnp.mean(v, axis=-1, keepdims=True)
    vc = v - mu
    var = jnp.mean(vc * vc, axis=-1, keepdims=True)
    v = (vc * lax.rsqrt(var + LN_EPS) * lng_ref[...] + lnb_ref[...]).astype(BF16)
    r_i = lax.broadcasted_iota(jnp.int32, (T, T), 0)
    c_i = lax.broadcasted_iota(jnp.int32, (T, T), 1)
    gd = width // SGU_GROUPS
    for gi in range(SGU_GROUPS):
        ws = jnp.where(r_i >= c_i, ws_ref[gi], 0.0).astype(BF16)
        cols = slice(gi * gd, (gi + 1) * gd)
        for c in range(tm // T):
            rws = slice(c * T, (c + 1) * T)
            sv = _dot(ws, v[rws, cols]) + bs_ref[gi]
            o_ref[rws, cols] = (u[rws, cols] * sv).astype(BF16)


def _sgu(x2, g, w_in, ln_g, ln_b, w_s, bs_b, *, tm=512):
    N, D = x2.shape
    W = w_in.shape[1] // 2
    tok = lambda w: pl.BlockSpec((tm, w), lambda i: (i, 0))
    return pl.pallas_call(
        functools.partial(_sgu_kernel, width=W),
        grid=(N // tm,),
        in_specs=[tok(D), _const_spec((1, D)), _const_spec(w_in.shape), _const_spec((1, W)), _const_spec((1, W)),
                  _const_spec(w_s.shape), _const_spec(bs_b.shape)],
        out_specs=tok(W),
        out_shape=jax.ShapeDtypeStruct((N, W), BF16),
        compiler_params=pltpu.CompilerParams(dimension_semantics=("parallel",), vmem_limit_bytes=VMEM_LIMIT),
        name="sgu",
    )(x2, g, w_in, ln_g, ln_b, w_s, bs_b)


def _pool_stage_start(lo, span):
    return -(-(lo + span) // 8) * 8


def _pool_kernel(x_ref, g_ref, w_ref, wg_ref, sc_ref, o_ref, zbuf, sum_a, sum_b, *, hist):
    tm = x_ref.shape[0]
    i = pl.program_id(1)
    rows = hist + tm

    @pl.when(i == 0)
    def _():
        zbuf[0:hist, :] = jnp.zeros((hist, zbuf.shape[1]), F32)

    xn = _rmsnorm(x_ref[...], g_ref[...]).astype(BF16)
    z = _dot(xn, w_ref[...])
    zbuf[hist:rows, :] = z
    t = i * tm + lax.broadcasted_iota(jnp.int32, (tm, 1), 0)
    gd = z.shape[1] // len(POOL_WINDOWS)
    src, span, lo = zbuf, 1, 0
    for s, w in enumerate(POOL_WINDOWS):
        assert w == 2 * span
        lo = _pool_stage_start(lo, span)
        dst = sum_a if s % 2 == 0 else sum_b
        c0 = s * gd
        dst[lo:rows, c0:] = src[lo:rows, c0:] + src[lo - span:rows - span, c0:]
        src, span = dst, w
    assert lo <= hist
    for gi, w in enumerate(POOL_WINDOWS):
        cols = slice(gi * gd, (gi + 1) * gd)
        win = (sum_a if gi % 2 == 0 else sum_b)[hist:rows, cols]
        cnt = jnp.minimum(t + 1, w).astype(F32)
        d = win / cnt - z[:, cols]
        y = _dot(d.astype(BF16), wg_ref[gi]) * sc_ref[:, cols]
        o_ref[:, cols] = y.astype(BF16)
    zbuf[0:hist, :] = zbuf[tm:rows, :]


def _pool(x, g, w_in, w_grp, scale, *, tm=512):
    B, S, D = x.shape
    PW = w_in.shape[1]
    hist, span = 0, 1
    for w in POOL_WINDOWS:
        hist, span = _pool_stage_start(hist, span), w
    assert tm >= hist
    tok = lambda w: pl.BlockSpec((None, tm, w), lambda b, i: (b, i, 0))
    return pl.pallas_call(
        functools.partial(_pool_kernel, hist=hist),
        grid=(B, S // tm),
        in_specs=[tok(D), _const_spec((1, D)), _const_spec(w_in.shape), _const_spec(w_grp.shape),
                  _const_spec((1, PW))],
        out_specs=tok(PW),
        out_shape=jax.ShapeDtypeStruct((B, S, PW), BF16),
        scratch_shapes=[pltpu.VMEM((hist + tm, PW), F32)] * 3,
        compiler_params=pltpu.CompilerParams(
            dimension_semantics=("arbitrary", "arbitrary"), vmem_limit_bytes=VMEM_LIMIT),
        name="pool",
    )(x, g, w_in, w_grp, scale)


def _post_kernel(x_ref, u_ref, wo_ref, g_ref, wgu_ref, wd_ref, fg_ref, o_ref, *, hidden, final):
    x1 = x_ref[...] + _dot(u_ref[...], wo_ref[...])
    xn = _rmsnorm(x1, g_ref[...]).astype(BF16)
    gate = _dot(xn, wgu_ref[:, :hidden])
    up = _dot(xn, wgu_ref[:, hidden:])
    a = (jax.nn.silu(gate) * up).astype(BF16)
    x2 = x1 + _dot(a, wd_ref[...])
    if final:
        x2 = _rmsnorm(x2, fg_ref[...])
    o_ref[...] = x2


def _post(x2, u2, w_out, g, w_gu_all, w_down_all, fg, *, layer, final, tm=256):
    N, D = x2.shape
    hidden = w_down_all.shape[1]
    tok = lambda w: pl.BlockSpec((tm, w), lambda i: (i, 0))
    of_layer = lambda a: pl.BlockSpec((None,) + a.shape[1:], lambda i: (layer, 0, 0), pipeline_mode=pl.Buffered(1))
    return pl.pallas_call(
        functools.partial(_post_kernel, hidden=hidden, final=final),
        grid=(N // tm,),
        in_specs=[tok(D), tok(u2.shape[1]), _const_spec(w_out.shape), _const_spec((1, D)),
                  of_layer(w_gu_all), of_layer(w_down_all), _const_spec((1, D))],
        out_specs=tok(D),
        out_shape=jax.ShapeDtypeStruct((N, D), F32),
        compiler_params=pltpu.CompilerParams(dimension_semantics=("parallel",), vmem_limit_bytes=VMEM_LIMIT),
        name="post",
    )(x2, u2, w_out, g, w_gu_all, w_down_all, fg)


def _rope_tables(positions):
    half = ROPE_DIM // 2
    inv_freq = ROPE_THETA ** (-jnp.arange(0, ROPE_DIM, 2, dtype=F32) / ROPE_DIM)
    ang = positions.astype(F32)[..., None] * inv_freq
    cos, sin = jnp.cos(ang), jnp.sin(ang)
    z = lambda n: jnp.zeros(ang.shape[:-1] + (n,), F32)
    dh = NSA_HEAD_DIM
    c64 = jnp.concatenate([cos, cos, 1.0 + z(dh - ROPE_DIM)], axis=-1)
    s1 = jnp.concatenate([-sin, z(dh - half)], axis=-1)
    s2 = jnp.concatenate([z(half), sin, z(dh - ROPE_DIM)], axis=-1)
    rep = LANES // dh
    return tuple(jnp.tile(t, (1, 1, rep)) for t in (c64, s1, s2))


def _permute_nsa_w_in(w):
    H, G, dh = NSA_HEADS, NSA_KV_GROUPS, NSA_HEAD_DIM
    base = H * dh
    seg = lambda k: w[:, base + k * G * dh: base + (k + 1) * G * dh]
    kc, vc, ks, vs, kw, vw = (seg(k) for k in range(6))
    gl = w[:, base + 6 * G * dh:]
    inter = lambda a, b: jnp.concatenate(
        [a.reshape(-1, G, 1, dh), b.reshape(-1, G, 1, dh)], axis=2).reshape(-1, 2 * G * dh)
    glp = jnp.pad(gl, ((0, 0), (0, LANES - gl.shape[1])))
    return jnp.concatenate([w[:, :base], inter(ks, vs), inter(kw, vw), inter(kc, vc), glp], axis=1).astype(BF16)


def _overlap_matrix_t(n_chunks, n_sel):
    n = np.arange(n_chunks)[None, :]
    m = np.arange(LANES)[:, None]
    start, end = n * CMP_STRIDE, n * CMP_STRIDE + CMP_BLOCK - 1
    ov = (start <= m * SEL_BLOCK + SEL_BLOCK - 1) & (end >= m * SEL_BLOCK) & (m < n_sel)
    return jnp.asarray(ov, BF16)


def _cmp_weights(k_w1, v_w1):
    dh = NSA_HEAD_DIM
    kw = k_w1.reshape(CMP_BLOCK, dh, -1)
    vw = v_w1.reshape(CMP_BLOCK, dh, -1)
    z = jnp.zeros_like(kw)
    w = jnp.concatenate([jnp.concatenate([kw, z], axis=2), jnp.concatenate([z, vw], axis=2)], axis=1).astype(BF16)
    return w[:CMP_STRIDE], w[CMP_STRIDE:]


def _nsa_layer(x, tabs, g, w_in, pos_k, k_w1, k_w2, pos_v, v_w1, v_w2):
    B, S, D = x.shape
    dh = NSA_HEAD_DIM
    assert CMP_BLOCK == 2 * CMP_STRIDE
    q, ks, vst, kw, vwt, kcv, gates = _nsa_in(x, g.reshape(1, D), _permute_nsa_w_in(w_in), *tabs)
    pos = jnp.concatenate([pos_k, pos_v], axis=1)
    wa, wb = _cmp_weights(k_w1, v_w1)
    pad_r = lambda w2: jnp.pad(w2, ((0, 0), (0, LANES - dh))).astype(BF16)
    pad_l = lambda w2: jnp.pad(w2, ((0, 0), (LANES - dh, 0))).astype(BF16)
    kc, vct = _nsa_cmp(kcv, pos[:CMP_STRIDE], pos[CMP_STRIDE:], wa, wb, pad_r(k_w2), pad_l(v_w2))
    ovt = _overlap_matrix_t(S // CMP_STRIDE, S // SEL_BLOCK)
    return _nsa_attn(q, kc, vct, ks, vst, kw, vwt, gates, ovt)


def kernel(x, positions, mix_norm, ffn_norm, final_norm, nsa_w_in, nsa_cmp_pos_k, nsa_cmp_k_w1, nsa_cmp_k_w2, nsa_cmp_pos_v, nsa_cmp_v_w1, nsa_cmp_v_w2, nsa_w_out, sgu_w_in, sgu_ln_g, sgu_ln_b, sgu_w_s, sgu_b_s, sgu_w_out, pool_w_in, pool_w_grp, pool_scale, pool_w_out, ffn_w_gate_up, ffn_w_down):
    B, S, D = x.shape
    depth = mix_norm.shape[0]
    tabs = _rope_tables(positions)
    row = lambda v: v.reshape(1, -1)
    w_gu_all, w_down_all = ffn_w_gate_up.astype(BF16), ffn_w_down.astype(BF16)
    for i in range(depth):
        kind, j = i % N_MIXERS, i // N_MIXERS
        if kind == 0:
            u = _nsa_layer(x, tabs, mix_norm[i], nsa_w_in[j], nsa_cmp_pos_k[j], nsa_cmp_k_w1[j], nsa_cmp_k_w2[j],
                           nsa_cmp_pos_v[j], nsa_cmp_v_w1[j], nsa_cmp_v_w2[j])
            w_out = nsa_w_out[j]
        elif kind == 1:
            gd = sgu_w_in.shape[-1] // 2 // SGU_GROUPS
            bs_b = jnp.broadcast_to(sgu_b_s[j][:, :, None], sgu_b_s[j].shape + (gd,))
            u = _sgu(x.reshape(B * S, D), row(mix_norm[i]), sgu_w_in[j].astype(BF16), row(sgu_ln_g[j]),
                     row(sgu_ln_b[j]), sgu_w_s[j], bs_b)
            w_out = sgu_w_out[j]
        else:
            u = _pool(x, row(mix_norm[i]), pool_w_in[j].astype(BF16), pool_w_grp[j].astype(BF16),
                      row(pool_scale[j]))
            w_out = pool_w_out[j]
        x = _post(x.reshape(B * S, D), u.reshape(B * S, -1), w_out.astype(BF16), row(ffn_norm[i]),
                  w_gu_all, w_down_all, row(final_norm), layer=i, final=(i == depth - 1)).reshape(B, S, D)
    return x
```

```python
import functools

import jax
import jax.numpy as jnp
import numpy as np
from jax import lax
from jax.experimental import pallas as pl
from jax.experimental.pallas import tpu as pltpu

RMS_EPS = 1e-6
LN_EPS = 1e-5
NEG_INF = -1e30
FORCE_SCORE = 1e9
LOG2_E = 1.4426950408889634
NO_KEY = 1 << 30

N_MIXERS = 3
NSA_HEADS = 16
NSA_KV_GROUPS = 4
NSA_HPG = NSA_HEADS // NSA_KV_GROUPS
NSA_HEAD_DIM = 64
CMP_BLOCK = 32
CMP_STRIDE = 16
SEL_BLOCK = 64
SEL_TOPN = 16
WINDOW = 512
ROPE_THETA = 500000.0
ROPE_DIM = NSA_HEAD_DIM // 4
SGU_GROUPS = 8
SGU_CHUNK = 128
POOL_WINDOWS = (2, 4, 8, 16)

LANES = 128
VMEM_LIMIT = 56 * 1024 * 1024

BF16 = jnp.bfloat16
F32 = jnp.float32


def _const_spec(shape):
    nd = len(shape)
    return pl.BlockSpec(shape, lambda *_: (0,) * nd, pipeline_mode=pl.Buffered(1))


def _rmsnorm(x, g):
    ms = jnp.mean(x * x, axis=-1, keepdims=True)
    return x * lax.rsqrt(ms + RMS_EPS) * g


def _dot(a, b):
    return jnp.dot(a, b, preferred_element_type=F32)


VT_ROWS = NSA_HEAD_DIM + 16


def _rope(t, c, s1, s2):
    return t * c + pltpu.roll(t, LANES - ROPE_DIM // 2, axis=1) * s1 + pltpu.roll(t, ROPE_DIM // 2, axis=1) * s2


def _nsa_in_kernel(x_ref, g_ref, w_ref, rot_ref,
                   q_ref, ks_ref, vst_ref, kw_ref, vwt_ref, kcv_ref, gate_ref, *, d_model, n_groups):
    tm = x_ref.shape[0]
    dh = NSA_HEAD_DIM
    half = ROPE_DIM // 2
    xn = _rmsnorm(x_ref[...], g_ref[...]).astype(BF16)
    y = _dot(xn, w_ref[...])
    r = rot_ref[...]
    lane = lax.broadcasted_iota(jnp.int32, r.shape, 1)
    d = jnp.bitwise_and(lane, dh - 1)
    first, second = d < half, (d >= half) & (d < ROPE_DIM)
    c = jnp.where(first, r, jnp.where(second, pltpu.roll(r, half, axis=1), 1.0))
    s1 = jnp.where(first, -pltpu.roll(r, LANES - half, axis=1), 0.0)
    s2 = jnp.where(second, r, 0.0)
    khalf = lane < dh
    ck, s1k, s2k = jnp.where(khalf, c, 1.0), jnp.where(khalf, s1, 0.0), jnp.where(khalf, s2, 0.0)
    scale = dh ** -0.5 * LOG2_E
    for j in range(d_model // LANES):
        sl = slice(j * LANES, (j + 1) * LANES)
        q_ref[:, sl] = (_rope(y[:, sl], c, s1, s2) * scale).astype(BF16)
    ones = jnp.ones((VT_ROWS - dh, LANES), F32)
    tok = pl.program_id(1) * tm + lax.broadcasted_iota(jnp.int32, c.shape, 0)
    blk_onehot = jnp.where(lane - dh == tok // SEL_BLOCK, 1.0, 0.0)
    off = d_model
    for k_ref, vt_ref, tag_blocks in ((ks_ref, vst_ref, True), (kw_ref, vwt_ref, False)):
        for g in range(n_groups):
            slab = _rope(y[:, off + g * LANES: off + (g + 1) * LANES], ck, s1k, s2k)
            k_ref[g] = (jnp.where(khalf, slab, blk_onehot) if tag_blocks else slab).astype(BF16)
            for ch in range(tm // LANES):
                vt = slab[ch * LANES:(ch + 1) * LANES, :].T[dh:, :]
                vt_ref[g, ch] = jnp.concatenate([vt, ones], axis=0).astype(BF16)
        off += n_groups * LANES
    for g in range(n_groups):
        kcv_ref[g] = _rope(y[:, off + g * LANES: off + (g + 1) * LANES], ck, s1k, s2k)
    off += n_groups * LANES
    gate_ref[...] = jax.nn.sigmoid(y[:, off:off + LANES])


def _nsa_in(x, g, w_perm, rot_tab, *, tm=512):
    B, S, D = x.shape
    G = NSA_KV_GROUPS
    n_out = w_perm.shape[1]
    tok = lambda w: pl.BlockSpec((None, tm, w), lambda b, i: (b, i, 0))
    grp = pl.BlockSpec((None, G, tm, LANES), lambda b, i: (b, 0, i, 0))
    grp_t = pl.BlockSpec((None, G, tm // LANES, VT_ROWS, LANES), lambda b, i: (b, 0, i, 0, 0))
    k_shape = jax.ShapeDtypeStruct((B, G, S, LANES), BF16)
    vt_shape = jax.ShapeDtypeStruct((B, G, S // LANES, VT_ROWS, LANES), BF16)
    return pl.pallas_call(
        functools.partial(_nsa_in_kernel, d_model=D, n_groups=G),
        grid=(B, S // tm),
        in_specs=[tok(D), _const_spec((1, D)), _const_spec((D, n_out)), tok(LANES)],
        out_specs=[tok(D), grp, grp_t, grp, grp_t, grp, tok(LANES)],
        out_shape=[jax.ShapeDtypeStruct((B, S, D), BF16), k_shape, vt_shape, k_shape, vt_shape,
                   jax.ShapeDtypeStruct((B, G, S, LANES), F32),
                   jax.ShapeDtypeStruct((B, S, LANES), F32)],
        compiler_params=pltpu.CompilerParams(
            dimension_semantics=("parallel", "parallel"), vmem_limit_bytes=VMEM_LIMIT),
        name="nsa_in",
    )(x, g, w_perm, rot_tab)


def _nsa_cmp_kernel(x_ref, pa_ref, pb_ref, wa_ref, wb_ref, wk2_ref, wv2_ref, kc_ref, vct_ref, *, n_chunks):
    acc_a = jnp.zeros((n_chunks, wa_ref.shape[-1]), F32)
    acc_b = jnp.zeros((n_chunks, wb_ref.shape[-1]), F32)
    for lp in range(CMP_STRIDE // 2):
        pair = (2 * lp, 2 * lp + 1)
        xs = [x_ref[pl.ds(l, n_chunks, stride=CMP_STRIDE), :] for l in pair]
        xa = jnp.concatenate([(x + pa_ref[l:l + 1, :]).astype(BF16) for x, l in zip(xs, pair)], axis=1)
        xb = jnp.concatenate([(x + pb_ref[l:l + 1, :]).astype(BF16) for x, l in zip(xs, pair)], axis=1)
        acc_a = acc_a + _dot(xa, wa_ref[lp])
        acc_b = acc_b + _dot(xb, wb_ref[lp])
    hid = jax.nn.gelu(acc_a + pltpu.roll(acc_b, n_chunks - 1, axis=0)).astype(BF16)
    hw = hid.shape[1] // 2
    kv = _dot(hid[:, :hw], wk2_ref[...]) + _dot(hid[:, hw:], wv2_ref[...])
    kc_ref[...] = kv.astype(BF16)
    vct_ref[...] = kv.T[NSA_HEAD_DIM:, :].astype(BF16)


def _nsa_cmp(kcv, pa, pb, wa, wb, wk2x, wv2x):
    B, G, S, _ = kcv.shape
    NC = S // CMP_STRIDE
    return pl.pallas_call(
        functools.partial(_nsa_cmp_kernel, n_chunks=NC),
        grid=(B, G),
        in_specs=[pl.BlockSpec((None, None, S, LANES), lambda b, g: (b, g, 0, 0)),
                  _const_spec(pa.shape), _const_spec(pb.shape), _const_spec(wa.shape), _const_spec(wb.shape),
                  _const_spec(wk2x.shape), _const_spec(wv2x.shape)],
        out_specs=[pl.BlockSpec((None, None, NC, LANES), lambda b, g: (b, g, 0, 0)),
                   pl.BlockSpec((None, None, NSA_HEAD_DIM, NC), lambda b, g: (b, g, 0, 0))],
        out_shape=[jax.ShapeDtypeStruct((B, G, NC, LANES), BF16),
                   jax.ShapeDtypeStruct((B, G, NSA_HEAD_DIM, NC), BF16)],
        compiler_params=pltpu.CompilerParams(
            dimension_semantics=("parallel", "parallel"), vmem_limit_bytes=VMEM_LIMIT),
        name="nsa_cmp",
    )(kcv, pa, pb, wa, wb, wk2x, wv2x)


def _nsa_attn_kernel(q_ref, kc_ref, vct_ref, ks_ref, vst_ref, kw_ref, vwt_ref, gate_ref, ovt_ref, o_ref,
                     q4w, q4s, acc_s, m_s, acc_w, m_w, ocmp, imp_ref, rank_ref, gt, *, tq, n_sel, n_grp, unroll):
    P, dh = NSA_HPG, NSA_HEAD_DIM
    H = n_grp * P
    i = pl.program_id(2)
    qs = i * tq
    head_cols = [slice(h * tq, (h + 1) * tq) for h in range(H)]
    grp_of = [h // P for h in range(H)]

    qt = q_ref[...].astype(F32).T
    zpad = jnp.zeros((LANES - dh, tq), BF16)
    for h, cols in enumerate(head_cols):
        qh = qt[h * dh:(h + 1) * dh].astype(BF16)
        q4w[0:dh, cols] = qh
        q4w[dh:, cols] = zpad
        q4s[0:dh, cols] = qh

    t_q = qs + lax.broadcasted_iota(jnp.int32, (1, tq), 1)

    tk = tq
    n_pre = WINDOW // tk
    sub_i = lax.broadcasted_iota(jnp.int32, (SEL_BLOCK, tq), 0)

    def scores_of(k_ref, k0, q4):
        kts = [k_ref[n, pl.ds(k0, tk), :] for n in range(n_grp)]
        return [_dot(kts[grp_of[h]], q4[:, cols]) for h, cols in enumerate(head_cols)]

    def absorb(scores, vt_ref, k0, acc, m_ref, keep=None, shift=None):
        j0 = k0 // LANES
        vts = [jnp.concatenate([vt_ref[n, j0 + c] for c in range(tk // LANES)], axis=1) for n in range(n_grp)]
        n_slab = tk // SEL_BLOCK
        keeps = None if keep is None else [keep(k0 + b * SEL_BLOCK + sub_i) for b in range(n_slab)]
        for h, cols in enumerate(head_cols):
            slabs = []
            for b in range(n_slab):
                sb = scores[h][b * SEL_BLOCK:(b + 1) * SEL_BLOCK]
                if shift is not None:
                    sb = sb + shift
                if keeps is not None:
                    sb = jnp.where(keeps[b], sb, NEG_INF)
                slabs.append(sb)
            top = slabs[0]
            for sb in slabs[1:]:
                top = jnp.maximum(top, sb)
            m_old = m_ref[0:1, cols]
            m_new = jnp.maximum(m_old, jnp.max(top, axis=0, keepdims=True))
            alpha = jnp.exp2(m_old - m_new)
            pr = jnp.concatenate([jnp.exp2(sb - m_new).astype(BF16) for sb in slabs], axis=0)
            acc[:, cols] = acc[:, cols] * alpha + _dot(vts[grp_of[h]], pr)
            m_ref[0:1, cols] = m_new

    for m_ref, acc in ((m_s, acc_s), (m_w, acc_w)):
        m_ref[...] = jnp.full(m_ref.shape, NEG_INF, F32)
        acc[...] = jnp.zeros(acc.shape, F32)

    nc = kc_ref.shape[1]
    sc_all = [_dot(kc_ref[grp_of[h]], q4w[:, cols]) for h, cols in enumerate(head_cols)]
    win = []
    for d in range(n_pre, -1, -1):
        j = i - d
        k0 = pl.multiple_of(jnp.maximum(j, 0) * tk, tk)
        win.append((d, j, k0, scores_of(kw_ref, k0, q4w)))

    n_i = lax.broadcasted_iota(jnp.int32, (nc, tq), 0)
    cbias = jnp.where(n_i * CMP_STRIDE + (CMP_BLOCK - 1) <= t_q, 0.0, NEG_INF)
    has_c = t_q >= CMP_BLOCK - 1
    pcs = []
    for sc in sc_all:
        sc = sc + cbias
        ec = jnp.exp2(sc - jnp.max(sc, axis=0, keepdims=True))
        pcs.append(ec * jnp.where(has_c, 1.0 / jnp.sum(ec, axis=0, keepdims=True), 0.0))
    for h, cols in enumerate(head_cols):
        ocmp[:, cols] = _dot(vct_ref[grp_of[h]], pcs[h].astype(BF16))

    ovt = ovt_ref[...]
    m_i = lax.broadcasted_iota(jnp.int32, (n_sel, tq), 0)
    cur = jnp.right_shift(t_q, SEL_BLOCK.bit_length() - 1)
    forced = (m_i == 0) | (m_i == cur) | (m_i == cur - 1)
    in_past = m_i * SEL_BLOCK <= t_q
    sub8 = lax.broadcasted_iota(jnp.int32, (8, tq), 0)
    for n in range(n_grp):
        psum = (pcs[n * P] + pcs[n * P + 1]) + (pcs[n * P + 2] + pcs[n * P + 3])
        hi = psum.astype(BF16)
        r1 = psum - hi.astype(F32)
        mid = r1.astype(BF16)
        lo = (r1 - mid.astype(F32)).astype(BF16)
        imp = (_dot(ovt, hi) + _dot(ovt, mid) + _dot(ovt, lo))[0:n_sel]
        imp = jnp.where(forced, FORCE_SCORE, imp)
        imp_ref[n] = jnp.where(in_past, imp, NEG_INF)
        rank_ref[n] = jnp.zeros((n_sel, tq), F32)

    causal = lambda kpos: kpos <= t_q
    for d, j, k0, s_win in win:
        if d == n_pre:
            lowest = t_q - WINDOW + jnp.where(j >= 0, 0, NO_KEY)
            absorb(s_win, vwt_ref, k0, acc_w, m_w, keep=lambda kpos: kpos > lowest)
        elif d == 0:
            absorb(s_win, vwt_ref, k0, acc_w, m_w, keep=causal)
        else:
            absorb(s_win, vwt_ref, k0, acc_w, m_w, shift=jnp.where(j >= 0, 0.0, NEG_INF))

    n_rg = n_sel // 8
    last_rg = ((qs + tq - 1) // SEL_BLOCK) // 8
    for rp in range(n_rg):
        @pl.when(rp <= last_rg)
        def _(rp=rp):
            for n in range(n_grp):
                groups = [imp_ref[n, 8 * r:8 * r + 8, :] for r in range(n_rg)]
                ranks = [rank_ref[n, 8 * r:8 * r + 8, :] for r in range(n_rg)]
                for mp in range(8 * rp, 8 * rp + 8):
                    row = imp_ref[n, mp:mp + 1, :]
                    for r, vr in enumerate(groups):
                        if 8 * r > mp:
                            beats = row >= vr
                        elif 8 * r + 7 < mp:
                            beats = row > vr
                        else:
                            beats = (row > vr) | ((row == vr) & (sub8 + 8 * r > mp))
                        ranks[r] = ranks[r] + jnp.where(beats, 1.0, 0.0)
                for r in range(n_rg):
                    rank_ref[n, 8 * r:8 * r + 8, :] = ranks[r]

    for n in range(n_grp):
        sel_bias = jnp.where(rank_ref[n] < float(min(SEL_TOPN, n_sel)), 0.0, NEG_INF)
        if n_sel < LANES - dh:
            sel_bias = jnp.concatenate([sel_bias, jnp.zeros((LANES - dh - n_sel, tq), F32)], axis=0)
        for cols in head_cols[n * P:(n + 1) * P]:
            q4s[dh:, cols] = sel_bias.astype(BF16)

    def sel_tiles(k0s):
        sc = [scores_of(ks_ref, k0, q4s) for k0 in k0s]
        for k0, s in zip(k0s, sc):
            absorb(s, vst_ref, k0, acc_s, m_s)

    def main_body(a, c):
        sel_tiles([pl.multiple_of((a * unroll + u) * tk, tk) for u in range(unroll)])
        return c

    lax.fori_loop(0, i // unroll, main_body, 0)
    bit = unroll // 2
    while bit >= 1:
        first = i - jnp.bitwise_and(i, 2 * bit - 1)

        @pl.when(jnp.bitwise_and(i, bit) != 0)
        def _(first=first, bit=bit):
            sel_tiles([pl.multiple_of((first + u) * tk, tk) for u in range(bit)])

        bit //= 2

    k_diag = pl.multiple_of(qs, tk)
    s_diag = scores_of(ks_ref, k_diag, q4s)
    o_w = acc_w[0:dh, :] * (1.0 / acc_w[dh:dh + 1, :])
    gt[...] = gate_ref[...].T
    absorb(s_diag, vst_ref, k_diag, acc_s, m_s, keep=causal)
    o_s = acc_s[0:dh, :] * (1.0 / acc_s[dh:dh + 1, :])

    h0 = H * pl.program_id(1)
    outs = []
    for h, cols in enumerate(head_cols):
        outs.append(gt[pl.ds(h0 + h, 1), :] * ocmp[:, cols]
                    + gt[pl.ds(NSA_HEADS + h0 + h, 1), :] * o_s[:, cols]
                    + gt[pl.ds(2 * NSA_HEADS + h0 + h, 1), :] * o_w[:, cols])
    o_ref[...] = jnp.concatenate(outs, axis=0).T.astype(BF16)


def _nsa_attn(q, kc, vct, ks, vst, kw, vwt, gates, ovt, *, tq=256, n_grp=2, unroll=8):
    B, S, D = q.shape
    G, P, dh = NSA_KV_GROUPS, NSA_HPG, NSA_HEAD_DIM
    NC = kc.shape[2]
    n_sel = S // SEL_BLOCK
    assert tq % LANES == 0 and tq & (tq - 1) == 0 and WINDOW % tq == 0 and G % n_grp == 0
    assert n_sel % 16 == 0 and n_sel <= LANES - dh and SEL_BLOCK & (SEL_BLOCK - 1) == 0
    L = n_grp * P * tq
    qblk = pl.BlockSpec((None, tq, n_grp * P * dh), lambda b, g, i: (b, i, g))
    grp = lambda *tail: pl.BlockSpec((None, n_grp) + tail, lambda b, g, i: (b, g) + (0,) * len(tail))
    seq, seq_t = grp(S, LANES), grp(S // LANES, VT_ROWS, LANES)
    return pl.pallas_call(
        functools.partial(_nsa_attn_kernel, tq=tq, n_sel=n_sel, n_grp=n_grp, unroll=unroll),
        grid=(B, G // n_grp, S // tq),
        in_specs=[qblk, grp(NC, LANES), grp(dh, NC), seq, seq_t, seq, seq_t,
                  pl.BlockSpec((None, tq, LANES), lambda b, g, i: (b, i, 0)),
                  _const_spec(ovt.shape)],
        out_specs=qblk,
        out_shape=jax.ShapeDtypeStruct((B, S, D), BF16),
        scratch_shapes=[pltpu.VMEM((LANES, L), BF16),
                        pltpu.VMEM((LANES, L), BF16),
                        pltpu.VMEM((VT_ROWS, L), F32),
                        pltpu.VMEM((8, L), F32),
                        pltpu.VMEM((VT_ROWS, L), F32),
                        pltpu.VMEM((8, L), F32),
                        pltpu.VMEM((dh, L), F32),
                        pltpu.VMEM((n_grp, n_sel, tq), F32),
                        pltpu.VMEM((n_grp, n_sel, tq), F32),
                        pltpu.VMEM((LANES, tq), F32)],
        compiler_params=pltpu.CompilerParams(
            dimension_semantics=("parallel", "parallel", "arbitrary"), vmem_limit_bytes=VMEM_LIMIT),
        name="nsa_attn",
    )(q, kc, vct, ks, vst, kw, vwt, gates, ovt)


def _sgu_kernel(x_ref, g_ref, w_ref, lng_ref, lnb_ref, ws_ref, bs_ref, o_ref, *, width):
    T = SGU_CHUNK
    tm = x_ref.shape[0]
    xn = _rmsnorm(x_ref[...], g_ref[...]).astype(BF16)
    y = jax.nn.gelu(_dot(xn, w_ref[...]))
    u, v = y[:, :width], y[:, width:]
    mu = jnp.mean(v, axis=-1, keepdims=True)
    vc = v - mu
    var = jnp.mean(vc * vc, axis=-1, keepdims=True)
    v = (vc * lax.rsqrt(var + LN_EPS) * lng_ref[...] + lnb_ref[...]).astype(BF16)
    r_i = lax.broadcasted_iota(jnp.int32, (T, T), 0)
    c_i = lax.broadcasted_iota(jnp.int32, (T, T), 1)
    gd = width // SGU_GROUPS
    for gi in range(SGU_GROUPS):
        ws = jnp.where(r_i >= c_i, ws_ref[gi], 0.0).astype(BF16)
        cols = slice(gi * gd, (gi + 1) * gd)
        for c in range(tm // T):
            rws = slice(c * T, (c + 1) * T)
            sv = _dot(ws, v[rws, cols]) + bs_ref[gi]
            o_ref[rws, cols] = (u[rws, cols] * sv).astype(BF16)


def _sgu(x2, g, w_in, ln_g, ln_b, w_s, bs_b, *, tm=512):
    N, D = x2.shape
    W = w_in.shape[1] // 2
    tok = lambda w: pl.BlockSpec((tm, w), lambda i: (i, 0))
    return pl.pallas_call(
        functools.partial(_sgu_kernel, width=W),
        grid=(N // tm,),
        in_specs=[tok(D), _const_spec((1, D)), _const_spec(w_in.shape), _const_spec((1, W)), _const_spec((1, W)),
                  _const_spec(w_s.shape), _const_spec(bs_b.shape)],
        out_specs=tok(W),
        out_shape=jax.ShapeDtypeStruct((N, W), BF16),
        compiler_params=pltpu.CompilerParams(dimension_semantics=("parallel",), vmem_limit_bytes=VMEM_LIMIT),
        name="sgu",
    )(x2, g, w_in, ln_g, ln_b, w_s, bs_b)


def _pool_stage_start(lo, span):
    return -(-(lo + span) // 8) * 8


def _pool_kernel(x_ref, g_ref, w_ref, wg_ref, sc_ref, o_ref, zbuf, sum_a, sum_b, *, hist):
    tm = x_ref.shape[0]
    i = pl.program_id(1)
    rows = hist + tm

    @pl.when(i == 0)
    def _():
        zbuf[0:hist, :] = jnp.zeros((hist, zbuf.shape[1]), F32)

    xn = _rmsnorm(x_ref[...], g_ref[...]).astype(BF16)
    z = _dot(xn, w_ref[...])
    zbuf[hist:rows, :] = z
    t = i * tm + lax.broadcasted_iota(jnp.int32, (tm, 1), 0)
    gd = z.shape[1] // len(POOL_WINDOWS)
    src, span, lo = zbuf, 1, 0
    for s, w in enumerate(POOL_WINDOWS):
        assert w == 2 * span
        lo = _pool_stage_start(lo, span)
        dst = sum_a if s % 2 == 0 else sum_b
        c0 = s * gd
        dst[lo:rows, c0:] = src[lo:rows, c0:] + src[lo - span:rows - span, c0:]
        src, span = dst, w
    assert lo <= hist
    for gi, w in enumerate(POOL_WINDOWS):
        cols = slice(gi * gd, (gi + 1) * gd)
        win = (sum_a if gi % 2 == 0 else sum_b)[hist:rows, cols]
        cnt = jnp.minimum(t + 1, w).astype(F32)
        d = win / cnt - z[:, cols]
        y = _dot(d.astype(BF16), wg_ref[gi]) * sc_ref[:, cols]
        o_ref[:, cols] = y.astype(BF16)
    zbuf[0:hist, :] = zbuf[tm:rows, :]


def _pool(x, g, w_in, w_grp, scale, *, tm=512):
    B, S, D = x.shape
    PW = w_in.shape[1]
    hist, span = 0, 1
    for w in POOL_WINDOWS:
        hist, span = _pool_stage_start(hist, span), w
    assert tm >= hist
    tok = lambda w: pl.BlockSpec((None, tm, w), lambda b, i: (b, i, 0))
    return pl.pallas_call(
        functools.partial(_pool_kernel, hist=hist),
        grid=(B, S // tm),
        in_specs=[tok(D), _const_spec((1, D)), _const_spec(w_in.shape), _const_spec(w_grp.shape),
                  _const_spec((1, PW))],
        out_specs=tok(PW),
        out_shape=jax.ShapeDtypeStruct((B, S, PW), BF16),
        scratch_shapes=[pltpu.VMEM((hist + tm, PW), F32)] * 3,
        compiler_params=pltpu.CompilerParams(
            dimension_semantics=("arbitrary", "arbitrary"), vmem_limit_bytes=VMEM_LIMIT),
        name="pool",
    )(x, g, w_in, w_grp, scale)


def _post_kernel(x_ref, u_ref, wo_ref, g_ref, wgu_ref, wd_ref, fg_ref, o_ref, *, hidden, final):
    x1 = x_ref[...] + _dot(u_ref[...], wo_ref[...])
    xn = _rmsnorm(x1, g_ref[...]).astype(BF16)
    gate = _dot(xn, wgu_ref[:, :hidden])
    up = _dot(xn, wgu_ref[:, hidden:])
    a = (jax.nn.silu(gate) * up).astype(BF16)
    x2 = x1 + _dot(a, wd_ref[...])
    if final:
        x2 = _rmsnorm(x2, fg_ref[...])
    o_ref[...] = x2


def _post(x2, u2, w_out, g, w_gu_all, w_down_all, fg, *, layer, final, tm=256):
    N, D = x2.shape
    hidden = w_down_all.shape[1]
    tok = lambda w: pl.BlockSpec((tm, w), lambda i: (i, 0))
    of_layer = lambda a: pl.BlockSpec((None,) + a.shape[1:], lambda i: (layer, 0, 0), pipeline_mode=pl.Buffered(1))
    return pl.pallas_call(
        functools.partial(_post_kernel, hidden=hidden, final=final),
        grid=(N // tm,),
        in_specs=[tok(D), tok(u2.shape[1]), _const_spec(w_out.shape), _const_spec((1, D)),
                  of_layer(w_gu_all), of_layer(w_down_all), _const_spec((1, D))],
        out_specs=tok(D),
        out_shape=jax.ShapeDtypeStruct((N, D), F32),
        compiler_params=pltpu.CompilerParams(dimension_semantics=("parallel",), vmem_limit_bytes=VMEM_LIMIT),
        name="post",
    )(x2, u2, w_out, g, w_gu_all, w_down_all, fg)


def _rope_table(positions):
    inv_freq = ROPE_THETA ** (-jnp.arange(0, ROPE_DIM, 2, dtype=F32) / ROPE_DIM)
    ang = positions.astype(F32)[..., None] * inv_freq
    pad = jnp.zeros(ang.shape[:-1] + (NSA_HEAD_DIM - ROPE_DIM,), F32)
    return jnp.tile(jnp.concatenate([jnp.cos(ang), jnp.sin(ang), pad], axis=-1), (1, 1, LANES // NSA_HEAD_DIM))


def _permute_nsa_w_in(w):
    H, G, dh = NSA_HEADS, NSA_KV_GROUPS, NSA_HEAD_DIM
    base = H * dh
    seg = lambda k: w[:, base + k * G * dh: base + (k + 1) * G * dh]
    kc, vc, ks, vs, kw, vw = (seg(k) for k in range(6))
    gl = w[:, base + 6 * G * dh:]
    inter = lambda a, b: jnp.concatenate(
        [a.reshape(-1, G, 1, dh), b.reshape(-1, G, 1, dh)], axis=2).reshape(-1, 2 * G * dh)
    glp = jnp.pad(gl, ((0, 0), (0, LANES - gl.shape[1])))
    return jnp.concatenate([w[:, :base], inter(ks, vs), inter(kw, vw), inter(kc, vc), glp], axis=1).astype(BF16)


def _overlap_matrix_t(n_chunks, n_sel):
    n = np.arange(n_chunks)[None, :]
    m = np.arange(LANES)[:, None]
    start, end = n * CMP_STRIDE, n * CMP_STRIDE + CMP_BLOCK - 1
    ov = (start <= m * SEL_BLOCK + SEL_BLOCK - 1) & (end >= m * SEL_BLOCK) & (m < n_sel)
    return jnp.asarray(ov, BF16)


def _cmp_weights(k_w1, v_w1):
    dh = NSA_HEAD_DIM
    kw = k_w1.reshape(CMP_BLOCK, dh, -1)
    vw = v_w1.reshape(CMP_BLOCK, dh, -1)
    z = jnp.zeros_like(kw)
    w = jnp.concatenate([jnp.concatenate([kw, z], axis=2), jnp.concatenate([z, vw], axis=2)], axis=1).astype(BF16)
    w = w.reshape(CMP_BLOCK // 2, 2 * w.shape[1], w.shape[2])
    return w[:CMP_STRIDE // 2], w[CMP_STRIDE // 2:]


def _nsa_layer(x, rot_tab, g, w_in, pos_k, k_w1, k_w2, pos_v, v_w1, v_w2):
    B, S, D = x.shape
    dh = NSA_HEAD_DIM
    assert CMP_BLOCK == 2 * CMP_STRIDE
    q, ks, vst, kw, vwt, kcv, gates = _nsa_in(x, g.reshape(1, D), _permute_nsa_w_in(w_in), rot_tab)
    pos = jnp.concatenate([pos_k, pos_v], axis=1)
    wa, wb = _cmp_weights(k_w1, v_w1)
    pad_r = lambda w2: jnp.pad(w2, ((0, 0), (0, LANES - dh))).astype(BF16)
    pad_l = lambda w2: jnp.pad(w2, ((0, 0), (LANES - dh, 0))).astype(BF16)
    kc, vct = _nsa_cmp(kcv, pos[:CMP_STRIDE], pos[CMP_STRIDE:], wa, wb, pad_r(k_w2), pad_l(v_w2))
    ovt = _overlap_matrix_t(S // CMP_STRIDE, S // SEL_BLOCK)
    return _nsa_attn(q, kc, vct, ks, vst, kw, vwt, gates, ovt)


def kernel(x, positions, mix_norm, ffn_norm, final_norm, nsa_w_in, nsa_cmp_pos_k, nsa_cmp_k_w1, nsa_cmp_k_w2, nsa_cmp_pos_v, nsa_cmp_v_w1, nsa_cmp_v_w2, nsa_w_out, sgu_w_in, sgu_ln_g, sgu_ln_b, sgu_w_s, sgu_b_s, sgu_w_out, pool_w_in, pool_w_grp, pool_scale, pool_w_out, ffn_w_gate_up, ffn_w_down):
    B, S, D = x.shape
    depth = mix_norm.shape[0]
    rot_tab = _rope_table(positions)
    row = lambda v: v.reshape(1, -1)
    w_gu_all, w_down_all = ffn_w_gate_up.astype(BF16), ffn_w_down.astype(BF16)
    for i in range(depth):
        kind, j = i % N_MIXERS, i // N_MIXERS
        if kind == 0:
            u = _nsa_layer(x, rot_tab, mix_norm[i], nsa_w_in[j], nsa_cmp_pos_k[j], nsa_cmp_k_w1[j], nsa_cmp_k_w2[j],
                           nsa_cmp_pos_v[j], nsa_cmp_v_w1[j], nsa_cmp_v_w2[j])
            w_out = nsa_w_out[j]
        elif kind == 1:
            gd = sgu_w_in.shape[-1] // 2 // SGU_GROUPS
            bs_b = jnp.broadcast_to(sgu_b_s[j][:, :, None], sgu_b_s[j].shape + (gd,))
            u = _sgu(x.reshape(B * S, D), row(mix_norm[i]), sgu_w_in[j].astype(BF16), row(sgu_ln_g[j]),
                     row(sgu_ln_b[j]), sgu_w_s[j], bs_b)
            w_out = sgu_w_out[j]
        else:
            u = _pool(x, row(mix_norm[i]), pool_w_in[j].astype(BF16), pool_w_grp[j].astype(BF16),
                      row(pool_scale[j]))
            w_out = pool_w_out[j]
        x = _post(x.reshape(B * S, D), u.reshape(B * S, -1), w_out.astype(BF16), row(ffn_norm[i]),
                  w_gu_all, w_down_all, row(final_norm), layer=i, final=(i == depth - 1)).reshape(B, S, D)
    return x
```

```python
import functools

import jax
import jax.numpy as jnp
import numpy as np
from jax import lax
from jax.experimental import pallas as pl
from jax.experimental.pallas import tpu as pltpu

RMS_EPS = 1e-6
LN_EPS = 1e-5
NEG_INF = -1e30
FORCE_SCORE = 1e9
LOG2_E = 1.4426950408889634
NO_KEY = 1 << 30

N_MIXERS = 3
NSA_HEADS = 16
NSA_KV_GROUPS = 4
NSA_HPG = NSA_HEADS // NSA_KV_GROUPS
NSA_HEAD_DIM = 64
CMP_BLOCK = 32
CMP_STRIDE = 16
SEL_BLOCK = 64
SEL_TOPN = 16
WINDOW = 512
ROPE_THETA = 500000.0
ROPE_DIM = NSA_HEAD_DIM // 4
SGU_GROUPS = 8
SGU_CHUNK = 128
POOL_WINDOWS = (2, 4, 8, 16)

LANES = 128
VMEM_LIMIT = 56 * 1024 * 1024

BF16 = jnp.bfloat16
F32 = jnp.float32


def _const_spec(shape):
    nd = len(shape)
    return pl.BlockSpec(shape, lambda *_: (0,) * nd, pipeline_mode=pl.Buffered(1))


def _rmsnorm(x, g):
    ms = jnp.mean(x * x, axis=-1, keepdims=True)
    return x * lax.rsqrt(ms + RMS_EPS) * g


def _dot(a, b):
    return jnp.dot(a, b, preferred_element_type=F32)


VT_ROWS = NSA_HEAD_DIM + 16


def _rope(t, c, s1, s2):
    return t * c + pltpu.roll(t, LANES - ROPE_DIM // 2, axis=1) * s1 + pltpu.roll(t, ROPE_DIM // 2, axis=1) * s2


def _nsa_in_kernel(x_ref, g_ref, w_ref, rot_ref,
                   q_ref, ks_ref, vst_ref, kw_ref, vwt_ref, kcv_ref, gate_ref, *, d_model, n_groups):
    tm = x_ref.shape[0]
    dh = NSA_HEAD_DIM
    half = ROPE_DIM // 2
    xn = _rmsnorm(x_ref[...], g_ref[...]).astype(BF16)
    y = _dot(xn, w_ref[...])
    r = rot_ref[...]
    lane = lax.broadcasted_iota(jnp.int32, r.shape, 1)
    d = jnp.bitwise_and(lane, dh - 1)
    first, second = d < half, (d >= half) & (d < ROPE_DIM)
    c = jnp.where(first, r, jnp.where(second, pltpu.roll(r, half, axis=1), 1.0))
    s1 = jnp.where(first, -pltpu.roll(r, LANES - half, axis=1), 0.0)
    s2 = jnp.where(second, r, 0.0)
    khalf = lane < dh
    ck, s1k, s2k = jnp.where(khalf, c, 1.0), jnp.where(khalf, s1, 0.0), jnp.where(khalf, s2, 0.0)
    scale = dh ** -0.5 * LOG2_E
    for j in range(d_model // LANES):
        sl = slice(j * LANES, (j + 1) * LANES)
        q_ref[:, sl] = (_rope(y[:, sl], c, s1, s2) * scale).astype(BF16)
    ones = jnp.ones((VT_ROWS - dh, LANES), F32)
    tok = pl.program_id(1) * tm + lax.broadcasted_iota(jnp.int32, c.shape, 0)
    blk_onehot = jnp.where(lane - dh == tok // SEL_BLOCK, 1.0, 0.0)
    off = d_model
    for k_ref, vt_ref, tag_blocks in ((ks_ref, vst_ref, True), (kw_ref, vwt_ref, False)):
        for g in range(n_groups):
            slab = _rope(y[:, off + g * LANES: off + (g + 1) * LANES], ck, s1k, s2k)
            k_ref[g] = (jnp.where(khalf, slab, blk_onehot) if tag_blocks else slab).astype(BF16)
            for ch in range(tm // LANES):
                vt = slab[ch * LANES:(ch + 1) * LANES, :].T[dh:, :]
                vt_ref[g, ch] = jnp.concatenate([vt, ones], axis=0).astype(BF16)
        off += n_groups * LANES
    for g in range(n_groups):
        kcv_ref[g] = _rope(y[:, off + g * LANES: off + (g + 1) * LANES], ck, s1k, s2k)
    off += n_groups * LANES
    gate_ref[...] = jax.nn.sigmoid(y[:, off:off + LANES])


def _nsa_in(x, g, w_perm, rot_tab, *, tm=1024):
    B, S, D = x.shape
    G = NSA_KV_GROUPS
    n_out = w_perm.shape[1]
    tok = lambda w: pl.BlockSpec((None, tm, w), lambda b, i: (b, i, 0))
    grp = pl.BlockSpec((None, G, tm, LANES), lambda b, i: (b, 0, i, 0))
    grp_t = pl.BlockSpec((None, G, tm // LANES, VT_ROWS, LANES), lambda b, i: (b, 0, i, 0, 0))
    k_shape = jax.ShapeDtypeStruct((B, G, S, LANES), BF16)
    vt_shape = jax.ShapeDtypeStruct((B, G, S // LANES, VT_ROWS, LANES), BF16)
    return pl.pallas_call(
        functools.partial(_nsa_in_kernel, d_model=D, n_groups=G),
        grid=(B, S // tm),
        in_specs=[tok(D), _const_spec((1, D)), _const_spec((D, n_out)), tok(LANES)],
        out_specs=[tok(D), grp, grp_t, grp, grp_t, grp, tok(LANES)],
        out_shape=[jax.ShapeDtypeStruct((B, S, D), BF16), k_shape, vt_shape, k_shape, vt_shape,
                   jax.ShapeDtypeStruct((B, G, S, LANES), F32),
                   jax.ShapeDtypeStruct((B, S, LANES), F32)],
        compiler_params=pltpu.CompilerParams(
            dimension_semantics=("parallel", "parallel"), vmem_limit_bytes=VMEM_LIMIT),
        name="nsa_in",
    )(x, g, w_perm, rot_tab)


def _nsa_cmp_kernel(x_ref, pa_ref, pb_ref, wa_ref, wb_ref, wk2_ref, wv2_ref, kc_ref, vct_ref, *, n_chunks):
    acc_a = jnp.zeros((n_chunks, wa_ref.shape[-1]), F32)
    acc_b = jnp.zeros((n_chunks, wb_ref.shape[-1]), F32)
    for lp in range(CMP_STRIDE // 2):
        pair = (2 * lp, 2 * lp + 1)
        xs = [x_ref[pl.ds(l, n_chunks, stride=CMP_STRIDE), :] for l in pair]
        xa = jnp.concatenate([(x + pa_ref[l:l + 1, :]).astype(BF16) for x, l in zip(xs, pair)], axis=1)
        xb = jnp.concatenate([(x + pb_ref[l:l + 1, :]).astype(BF16) for x, l in zip(xs, pair)], axis=1)
        acc_a = acc_a + _dot(xa, wa_ref[lp])
        acc_b = acc_b + _dot(xb, wb_ref[lp])
    hid = jax.nn.gelu(acc_a + pltpu.roll(acc_b, n_chunks - 1, axis=0)).astype(BF16)
    hw = hid.shape[1] // 2
    kv = _dot(hid[:, :hw], wk2_ref[...]) + _dot(hid[:, hw:], wv2_ref[...])
    kc_ref[...] = kv.astype(BF16)
    vct_ref[...] = kv.T[NSA_HEAD_DIM:, :].astype(BF16)


def _nsa_cmp(kcv, pa, pb, wa, wb, wk2x, wv2x):
    B, G, S, _ = kcv.shape
    NC = S // CMP_STRIDE
    return pl.pallas_call(
        functools.partial(_nsa_cmp_kernel, n_chunks=NC),
        grid=(B, G),
        in_specs=[pl.BlockSpec((None, None, S, LANES), lambda b, g: (b, g, 0, 0)),
                  _const_spec(pa.shape), _const_spec(pb.shape), _const_spec(wa.shape), _const_spec(wb.shape),
                  _const_spec(wk2x.shape), _const_spec(wv2x.shape)],
        out_specs=[pl.BlockSpec((None, None, NC, LANES), lambda b, g: (b, g, 0, 0)),
                   pl.BlockSpec((None, None, NSA_HEAD_DIM, NC), lambda b, g: (b, g, 0, 0))],
        out_shape=[jax.ShapeDtypeStruct((B, G, NC, LANES), BF16),
                   jax.ShapeDtypeStruct((B, G, NSA_HEAD_DIM, NC), BF16)],
        compiler_params=pltpu.CompilerParams(
            dimension_semantics=("parallel", "parallel"), vmem_limit_bytes=VMEM_LIMIT),
        name="nsa_cmp",
    )(kcv, pa, pb, wa, wb, wk2x, wv2x)


def _nsa_attn_kernel(q_ref, kc_ref, vct_ref, ks_ref, vst_ref, kw_ref, vwt_ref, gate_ref, ovt_ref, o_ref,
                     q4w, q4s, acc_s, m_s, acc_w, m_w, ocmp, imp_ref, rank_ref, gt, *, tq, n_sel, n_grp, unroll):
    P, dh = NSA_HPG, NSA_HEAD_DIM
    H = n_grp * P
    i = pl.program_id(2)
    qs = i * tq
    head_cols = [slice(h * tq, (h + 1) * tq) for h in range(H)]
    grp_of = [h // P for h in range(H)]

    qt = q_ref[...].astype(F32).T
    zpad = jnp.zeros((LANES - dh, tq), BF16)
    for h, cols in enumerate(head_cols):
        qh = qt[h * dh:(h + 1) * dh].astype(BF16)
        q4w[0:dh, cols] = qh
        q4w[dh:, cols] = zpad
        q4s[0:dh, cols] = qh

    t_q = qs + lax.broadcasted_iota(jnp.int32, (1, tq), 1)

    tk = tq
    n_pre = WINDOW // tk
    sub_i = lax.broadcasted_iota(jnp.int32, (SEL_BLOCK, tq), 0)

    def scores_of(k_ref, k0, q4):
        kts = [k_ref[n, pl.ds(k0, tk), :] for n in range(n_grp)]
        return [_dot(kts[grp_of[h]], q4[:, cols]) for h, cols in enumerate(head_cols)]

    def absorb(scores, vt_ref, k0, acc, m_ref, keep=None, shift=None):
        j0 = k0 // LANES
        vts = [jnp.concatenate([vt_ref[n, j0 + c] for c in range(tk // LANES)], axis=1) for n in range(n_grp)]
        n_slab = tk // SEL_BLOCK
        keeps = None if keep is None else [keep(k0 + b * SEL_BLOCK + sub_i) for b in range(n_slab)]
        for h, cols in enumerate(head_cols):
            slabs = []
            for b in range(n_slab):
                sb = scores[h][b * SEL_BLOCK:(b + 1) * SEL_BLOCK]
                if shift is not None:
                    sb = sb + shift
                if keeps is not None:
                    sb = jnp.where(keeps[b], sb, NEG_INF)
                slabs.append(sb)
            top = slabs[0]
            for sb in slabs[1:]:
                top = jnp.maximum(top, sb)
            m_old = m_ref[0:1, cols]
            m_new = jnp.maximum(m_old, jnp.max(top, axis=0, keepdims=True))
            alpha = jnp.exp2(m_old - m_new)
            pr = jnp.concatenate([jnp.exp2(sb - m_new).astype(BF16) for sb in slabs], axis=0)
            acc[:, cols] = acc[:, cols] * alpha + _dot(vts[grp_of[h]], pr)
            m_ref[0:1, cols] = m_new

    for m_ref, acc in ((m_s, acc_s), (m_w, acc_w)):
        m_ref[...] = jnp.full(m_ref.shape, NEG_INF, F32)
        acc[...] = jnp.zeros(acc.shape, F32)

    nc = kc_ref.shape[1]
    sc_all = [_dot(kc_ref[grp_of[h]], q4w[:, cols]) for h, cols in enumerate(head_cols)]
    win = []
    for d in range(n_pre, -1, -1):
        j = i - d
        k0 = pl.multiple_of(jnp.maximum(j, 0) * tk, tk)
        win.append((d, j, k0, scores_of(kw_ref, k0, q4w)))

    n_i = lax.broadcasted_iota(jnp.int32, (nc, tq), 0)
    cbias = jnp.where(n_i * CMP_STRIDE + (CMP_BLOCK - 1) <= t_q, 0.0, NEG_INF)
    has_c = t_q >= CMP_BLOCK - 1
    pcs = []
    for sc in sc_all:
        sc = sc + cbias
        ec = jnp.exp2(sc - jnp.max(sc, axis=0, keepdims=True))
        pcs.append(ec * jnp.where(has_c, 1.0 / jnp.sum(ec, axis=0, keepdims=True), 0.0))
    for h, cols in enumerate(head_cols):
        ocmp[:, cols] = _dot(vct_ref[grp_of[h]], pcs[h].astype(BF16))

    ovt = ovt_ref[...]
    m_i = lax.broadcasted_iota(jnp.int32, (n_sel, tq), 0)
    cur = jnp.right_shift(t_q, SEL_BLOCK.bit_length() - 1)
    forced = (m_i == 0) | (m_i == cur) | (m_i == cur - 1)
    in_past = m_i * SEL_BLOCK <= t_q
    sub8 = lax.broadcasted_iota(jnp.int32, (8, tq), 0)
    for n in range(n_grp):
        psum = (pcs[n * P] + pcs[n * P + 1]) + (pcs[n * P + 2] + pcs[n * P + 3])
        hi = psum.astype(BF16)
        r1 = psum - hi.astype(F32)
        mid = r1.astype(BF16)
        lo = (r1 - mid.astype(F32)).astype(BF16)
        imp = (_dot(ovt, hi) + _dot(ovt, mid) + _dot(ovt, lo))[0:n_sel]
        imp = jnp.where(forced, FORCE_SCORE, imp)
        imp_ref[n] = jnp.where(in_past, imp, NEG_INF)
        rank_ref[n] = jnp.zeros((n_sel, tq), F32)

    causal = lambda kpos: kpos <= t_q
    for d, j, k0, s_win in win:
        if d == n_pre:
            lowest = t_q - WINDOW + jnp.where(j >= 0, 0, NO_KEY)
            absorb(s_win, vwt_ref, k0, acc_w, m_w, keep=lambda kpos: kpos > lowest)
        elif d == 0:
            absorb(s_win, vwt_ref, k0, acc_w, m_w, keep=causal)
        else:
            absorb(s_win, vwt_ref, k0, acc_w, m_w, shift=jnp.where(j >= 0, 0.0, NEG_INF))

    n_rg = n_sel // 8
    last_rg = ((qs + tq - 1) // SEL_BLOCK) // 8
    for rp in range(n_rg):
        @pl.when(rp <= last_rg)
        def _(rp=rp):
            for n in range(n_grp):
                groups = [imp_ref[n, 8 * r:8 * r + 8, :] for r in range(n_rg)]
                ranks = [rank_ref[n, 8 * r:8 * r + 8, :] for r in range(n_rg)]
                for mp in range(8 * rp, 8 * rp + 8):
                    row = imp_ref[n, mp:mp + 1, :]
                    for r, vr in enumerate(groups):
                        if 8 * r > mp:
                            beats = row >= vr
                        elif 8 * r + 7 < mp:
                            beats = row > vr
                        else:
                            beats = (row > vr) | ((row == vr) & (sub8 + 8 * r > mp))
                        ranks[r] = ranks[r] + jnp.where(beats, 1.0, 0.0)
                for r in range(n_rg):
                    rank_ref[n, 8 * r:8 * r + 8, :] = ranks[r]

    for n in range(n_grp):
        sel_bias = jnp.where(rank_ref[n] < float(min(SEL_TOPN, n_sel)), 0.0, NEG_INF)
        if n_sel < LANES - dh:
            sel_bias = jnp.concatenate([sel_bias, jnp.zeros((LANES - dh - n_sel, tq), F32)], axis=0)
        for cols in head_cols[n * P:(n + 1) * P]:
            q4s[dh:, cols] = sel_bias.astype(BF16)

    def sel_tiles(k0s):
        sc = [scores_of(ks_ref, k0, q4s) for k0 in k0s]
        for k0, s in zip(k0s, sc):
            absorb(s, vst_ref, k0, acc_s, m_s)

    def main_body(a, c):
        sel_tiles([pl.multiple_of((a * unroll + u) * tk, tk) for u in range(unroll)])
        return c

    lax.fori_loop(0, i // unroll, main_body, 0)
    bit = unroll // 2
    while bit >= 1:
        first = i - jnp.bitwise_and(i, 2 * bit - 1)

        @pl.when(jnp.bitwise_and(i, bit) != 0)
        def _(first=first, bit=bit):
            sel_tiles([pl.multiple_of((first + u) * tk, tk) for u in range(bit)])

        bit //= 2

    k_diag = pl.multiple_of(qs, tk)
    s_diag = scores_of(ks_ref, k_diag, q4s)
    o_w = acc_w[0:dh, :] * (1.0 / acc_w[dh:dh + 1, :])
    gt[...] = gate_ref[...].T
    absorb(s_diag, vst_ref, k_diag, acc_s, m_s, keep=causal)
    o_s = acc_s[0:dh, :] * (1.0 / acc_s[dh:dh + 1, :])

    h0 = H * pl.program_id(1)
    outs = []
    for h, cols in enumerate(head_cols):
        outs.append(gt[pl.ds(h0 + h, 1), :] * ocmp[:, cols]
                    + gt[pl.ds(NSA_HEADS + h0 + h, 1), :] * o_s[:, cols]
                    + gt[pl.ds(2 * NSA_HEADS + h0 + h, 1), :] * o_w[:, cols])
    o_ref[...] = jnp.concatenate(outs, axis=0).T.astype(BF16)


def _nsa_attn(q, kc, vct, ks, vst, kw, vwt, gates, ovt, *, tq=256, n_grp=2, unroll=8):
    B, S, D = q.shape
    G, P, dh = NSA_KV_GROUPS, NSA_HPG, NSA_HEAD_DIM
    NC = kc.shape[2]
    n_sel = S // SEL_BLOCK
    assert tq % LANES == 0 and tq & (tq - 1) == 0 and WINDOW % tq == 0 and G % n_grp == 0
    assert n_sel % 16 == 0 and n_sel <= LANES - dh and SEL_BLOCK & (SEL_BLOCK - 1) == 0
    L = n_grp * P * tq
    qblk = pl.BlockSpec((None, tq, n_grp * P * dh), lambda b, g, i: (b, i, g))
    grp = lambda *tail: pl.BlockSpec((None, n_grp) + tail, lambda b, g, i: (b, g) + (0,) * len(tail))
    seq, seq_t = grp(S, LANES), grp(S // LANES, VT_ROWS, LANES)
    return pl.pallas_call(
        functools.partial(_nsa_attn_kernel, tq=tq, n_sel=n_sel, n_grp=n_grp, unroll=unroll),
        grid=(B, G // n_grp, S // tq),
        in_specs=[qblk, grp(NC, LANES), grp(dh, NC), seq, seq_t, seq, seq_t,
                  pl.BlockSpec((None, tq, LANES), lambda b, g, i: (b, i, 0)),
                  _const_spec(ovt.shape)],
        out_specs=qblk,
        out_shape=jax.ShapeDtypeStruct((B, S, D), BF16),
        scratch_shapes=[pltpu.VMEM((LANES, L), BF16),
                        pltpu.VMEM((LANES, L), BF16),
                        pltpu.VMEM((VT_ROWS, L), F32),
                        pltpu.VMEM((8, L), F32),
                        pltpu.VMEM((VT_ROWS, L), F32),
                        pltpu.VMEM((8, L), F32),
                        pltpu.VMEM((dh, L), F32),
                        pltpu.VMEM((n_grp, n_sel, tq), F32),
                        pltpu.VMEM((n_grp, n_sel, tq), F32),
                        pltpu.VMEM((LANES, tq), F32)],
        compiler_params=pltpu.CompilerParams(
            dimension_semantics=("parallel", "parallel", "arbitrary"), vmem_limit_bytes=VMEM_LIMIT),
        name="nsa_attn",
    )(q, kc, vct, ks, vst, kw, vwt, gates, ovt)


def _sgu_kernel(x_ref, g_ref, w_ref, lng_ref, lnb_ref, ws_ref, bs_ref, o_ref, *, width):
    T = SGU_CHUNK
    tm = x_ref.shape[0]
    xn = _rmsnorm(x_ref[...], g_ref[...]).astype(BF16)
    y = jax.nn.gelu(_dot(xn, w_ref[...]))
    u, v = y[:, :width], y[:, width:]
    mu = jnp.mean(v, axis=-1, keepdims=True)
    vc = v - mu
    var = jnp.mean(vc * vc, axis=-1, keepdims=True)
    v = (vc * lax.rsqrt(var + LN_EPS) * lng_ref[...] + lnb_ref[...]).astype(BF16)
    r_i = lax.broadcasted_iota(jnp.int32, (T, T), 0)
    c_i = lax.broadcasted_iota(jnp.int32, (T, T), 1)
    gd = width // SGU_GROUPS
    for gi in range(SGU_GROUPS):
        ws = jnp.where(r_i >= c_i, ws_ref[gi], 0.0).astype(BF16)
        cols = slice(gi * gd, (gi + 1) * gd)
        for c in range(tm // T):
            rws = slice(c * T, (c + 1) * T)
            sv = _dot(ws, v[rws, cols]) + bs_ref[gi]
            o_ref[rws, cols] = (u[rws, cols] * sv).astype(BF16)


def _sgu(x2, g, w_in, ln_g, ln_b, w_s, bs_b, *, tm=1024):
    N, D = x2.shape
    W = w_in.shape[1] // 2
    tok = lambda w: pl.BlockSpec((tm, w), lambda i: (i, 0))
    return pl.pallas_call(
        functools.partial(_sgu_kernel, width=W),
        grid=(N // tm,),
        in_specs=[tok(D), _const_spec((1, D)), _const_spec(w_in.shape), _const_spec((1, W)), _const_spec((1, W)),
                  _const_spec(w_s.shape), _const_spec(bs_b.shape)],
        out_specs=tok(W),
        out_shape=jax.ShapeDtypeStruct((N, W), BF16),
        compiler_params=pltpu.CompilerParams(dimension_semantics=("parallel",), vmem_limit_bytes=VMEM_LIMIT),
        name="sgu",
    )(x2, g, w_in, ln_g, ln_b, w_s, bs_b)


def _pool_stage_start(lo, span):
    return -(-(lo + span) // 8) * 8


def _pool_kernel(x_ref, g_ref, w_ref, wg_ref, sc_ref, o_ref, zbuf, sum_a, sum_b, *, hist):
    tm = x_ref.shape[0]
    i = pl.program_id(1)
    rows = hist + tm

    @pl.when(i == 0)
    def _():
        zbuf[0:hist, :] = jnp.zeros((hist, zbuf.shape[1]), F32)

    xn = _rmsnorm(x_ref[...], g_ref[...]).astype(BF16)
    z = _dot(xn, w_ref[...])
    zbuf[hist:rows, :] = z
    t = i * tm + lax.broadcasted_iota(jnp.int32, (tm, 1), 0)
    gd = z.shape[1] // len(POOL_WINDOWS)
    src, span, lo = zbuf, 1, 0
    for s, w in enumerate(POOL_WINDOWS):
        assert w == 2 * span
        lo = _pool_stage_start(lo, span)
        dst = sum_a if s % 2 == 0 else sum_b
        c0 = s * gd
        dst[lo:rows, c0:] = src[lo:rows, c0:] + src[lo - span:rows - span, c0:]
        src, span = dst, w
    assert lo <= hist
    for gi, w in enumerate(POOL_WINDOWS):
        cols = slice(gi * gd, (gi + 1) * gd)
        win = (sum_a if gi % 2 == 0 else sum_b)[hist:rows, cols]
        cnt = jnp.minimum(t + 1, w).astype(F32)
        d = win / cnt - z[:, cols]
        y = _dot(d.astype(BF16), wg_ref[gi]) * sc_ref[:, cols]
        o_ref[:, cols] = y.astype(BF16)
    zbuf[0:hist, :] = zbuf[tm:rows, :]


def _pool(x, g, w_in, w_grp, scale, *, tm=1024):
    B, S, D = x.shape
    PW = w_in.shape[1]
    hist, span = 0, 1
    for w in POOL_WINDOWS:
        hist, span = _pool_stage_start(hist, span), w
    assert tm >= hist
    tok = lambda w: pl.BlockSpec((None, tm, w), lambda b, i: (b, i, 0))
    return pl.pallas_call(
        functools.partial(_pool_kernel, hist=hist),
        grid=(B, S // tm),
        in_specs=[tok(D), _const_spec((1, D)), _const_spec(w_in.shape), _const_spec(w_grp.shape),
                  _const_spec((1, PW))],
        out_specs=tok(PW),
        out_shape=jax.ShapeDtypeStruct((B, S, PW), BF16),
        scratch_shapes=[pltpu.VMEM((hist + tm, PW), F32)] * 3,
        compiler_params=pltpu.CompilerParams(
            dimension_semantics=("arbitrary", "arbitrary"), vmem_limit_bytes=VMEM_LIMIT),
        name="pool",
    )(x, g, w_in, w_grp, scale)


def _post_kernel(x_ref, u_ref, wo_ref, g_ref, wgu_ref, wd_ref, fg_ref, o_ref, *, hidden, final):
    x1 = x_ref[...] + _dot(u_ref[...], wo_ref[...])
    xn = _rmsnorm(x1, g_ref[...]).astype(BF16)
    gate = _dot(xn, wgu_ref[:, :hidden])
    up = _dot(xn, wgu_ref[:, hidden:])
    a = (jax.nn.silu(gate) * up).astype(BF16)
    x2 = x1 + _dot(a, wd_ref[...])
    if final:
        x2 = _rmsnorm(x2, fg_ref[...])
    o_ref[...] = x2


def _post(x2, u2, w_out, g, w_gu_all, w_down_all, fg, *, layer, final, tm=512):
    N, D = x2.shape
    hidden = w_down_all.shape[1]
    tok = lambda w: pl.BlockSpec((tm, w), lambda i: (i, 0))
    of_layer = lambda a: pl.BlockSpec((None,) + a.shape[1:], lambda i: (layer, 0, 0), pipeline_mode=pl.Buffered(1))
    return pl.pallas_call(
        functools.partial(_post_kernel, hidden=hidden, final=final),
        grid=(N // tm,),
        in_specs=[tok(D), tok(u2.shape[1]), _const_spec(w_out.shape), _const_spec((1, D)),
                  of_layer(w_gu_all), of_layer(w_down_all), _const_spec((1, D))],
        out_specs=tok(D),
        out_shape=jax.ShapeDtypeStruct((N, D), F32),
        compiler_params=pltpu.CompilerParams(dimension_semantics=("parallel",), vmem_limit_bytes=VMEM_LIMIT),
        name="post",
    )(x2, u2, w_out, g, w_gu_all, w_down_all, fg)


def _rope_table(positions):
    inv_freq = ROPE_THETA ** (-jnp.arange(0, ROPE_DIM, 2, dtype=F32) / ROPE_DIM)
    ang = positions.astype(F32)[..., None] * inv_freq
    pad = jnp.zeros(ang.shape[:-1] + (NSA_HEAD_DIM - ROPE_DIM,), F32)
    return jnp.tile(jnp.concatenate([jnp.cos(ang), jnp.sin(ang), pad], axis=-1), (1, 1, LANES // NSA_HEAD_DIM))


def _permute_nsa_w_in(w):
    H, G, dh = NSA_HEADS, NSA_KV_GROUPS, NSA_HEAD_DIM
    base = H * dh
    seg = lambda k: w[:, base + k * G * dh: base + (k + 1) * G * dh]
    kc, vc, ks, vs, kw, vw = (seg(k) for k in range(6))
    gl = w[:, base + 6 * G * dh:]
    inter = lambda a, b: jnp.concatenate(
        [a.reshape(-1, G, 1, dh), b.reshape(-1, G, 1, dh)], axis=2).reshape(-1, 2 * G * dh)
    glp = jnp.pad(gl, ((0, 0), (0, LANES - gl.shape[1])))
    return jnp.concatenate([w[:, :base], inter(ks, vs), inter(kw, vw), inter(kc, vc), glp], axis=1).astype(BF16)


def _overlap_matrix_t(n_chunks, n_sel):
    n = np.arange(n_chunks)[None, :]
    m = np.arange(LANES)[:, None]
    start, end = n * CMP_STRIDE, n * CMP_STRIDE + CMP_BLOCK - 1
    ov = (start <= m * SEL_BLOCK + SEL_BLOCK - 1) & (end >= m * SEL_BLOCK) & (m < n_sel)
    return jnp.asarray(ov, BF16)


def _cmp_weights(k_w1, v_w1):
    dh = NSA_HEAD_DIM
    kw = k_w1.reshape(CMP_BLOCK, dh, -1)
    vw = v_w1.reshape(CMP_BLOCK, dh, -1)
    z = jnp.zeros_like(kw)
    w = jnp.concatenate([jnp.concatenate([kw, z], axis=2), jnp.concatenate([z, vw], axis=2)], axis=1).astype(BF16)
    w = w.reshape(CMP_BLOCK // 2, 2 * w.shape[1], w.shape[2])
    return w[:CMP_STRIDE // 2], w[CMP_STRIDE // 2:]


def _nsa_layer(x, rot_tab, g, w_in, pos_k, k_w1, k_w2, pos_v, v_w1, v_w2):
    B, S, D = x.shape
    dh = NSA_HEAD_DIM
    assert CMP_BLOCK == 2 * CMP_STRIDE
    q, ks, vst, kw, vwt, kcv, gates = _nsa_in(x, g.reshape(1, D), _permute_nsa_w_in(w_in), rot_tab)
    pos = jnp.concatenate([pos_k, pos_v], axis=1)
    wa, wb = _cmp_weights(k_w1, v_w1)
    pad_r = lambda w2: jnp.pad(w2, ((0, 0), (0, LANES - dh))).astype(BF16)
    pad_l = lambda w2: jnp.pad(w2, ((0, 0), (LANES - dh, 0))).astype(BF16)
    kc, vct = _nsa_cmp(kcv, pos[:CMP_STRIDE], pos[CMP_STRIDE:], wa, wb, pad_r(k_w2), pad_l(v_w2))
    ovt = _overlap_matrix_t(S // CMP_STRIDE, S // SEL_BLOCK)
    return _nsa_attn(q, kc, vct, ks, vst, kw, vwt, gates, ovt)


def kernel(x, positions, mix_norm, ffn_norm, final_norm, nsa_w_in, nsa_cmp_pos_k, nsa_cmp_k_w1, nsa_cmp_k_w2, nsa_cmp_pos_v, nsa_cmp_v_w1, nsa_cmp_v_w2, nsa_w_out, sgu_w_in, sgu_ln_g, sgu_ln_b, sgu_w_s, sgu_b_s, sgu_w_out, pool_w_in, pool_w_grp, pool_scale, pool_w_out, ffn_w_gate_up, ffn_w_down):
    B, S, D = x.shape
    depth = mix_norm.shape[0]
    rot_tab = _rope_table(positions)
    row = lambda v: v.reshape(1, -1)
    w_gu_all, w_down_all = ffn_w_gate_up.astype(BF16), ffn_w_down.astype(BF16)
    for i in range(depth):
        kind, j = i % N_MIXERS, i // N_MIXERS
        if kind == 0:
            u = _nsa_layer(x, rot_tab, mix_norm[i], nsa_w_in[j], nsa_cmp_pos_k[j], nsa_cmp_k_w1[j], nsa_cmp_k_w2[j],
                           nsa_cmp_pos_v[j], nsa_cmp_v_w1[j], nsa_cmp_v_w2[j])
            w_out = nsa_w_out[j]
        elif kind == 1:
            gd = sgu_w_in.shape[-1] // 2 // SGU_GROUPS
            bs_b = jnp.broadcast_to(sgu_b_s[j][:, :, None], sgu_b_s[j].shape + (gd,))
            u = _sgu(x.reshape(B * S, D), row(mix_norm[i]), sgu_w_in[j].astype(BF16), row(sgu_ln_g[j]),
                     row(sgu_ln_b[j]), sgu_w_s[j], bs_b)
            w_out = sgu_w_out[j]
        else:
            u = _pool(x, row(mix_norm[i]), pool_w_in[j].astype(BF16), pool_w_grp[j].astype(BF16),
                      row(pool_scale[j]))
            w_out = pool_w_out[j]
        x = _post(x.reshape(B * S, D), u.reshape(B * S, -1), w_out.astype(BF16), row(ffn_norm[i]),
                  w_gu_all, w_down_all, row(final_norm), layer=i, final=(i == depth - 1)).reshape(B, S, D)
    return x
```

```python
import functools

import jax
import jax.numpy as jnp
import numpy as np
from jax import lax
from jax.experimental import pallas as pl
from jax.experimental.pallas import tpu as pltpu

RMS_EPS = 1e-6
LN_EPS = 1e-5
NEG_INF = -1e30
FORCE_SCORE = 1e9
LOG2_E = 1.4426950408889634
NO_KEY = 1 << 30

N_MIXERS = 3
NSA_HEADS = 16
NSA_KV_GROUPS = 4
NSA_HPG = NSA_HEADS // NSA_KV_GROUPS
NSA_HEAD_DIM = 64
CMP_BLOCK = 32
CMP_STRIDE = 16
SEL_BLOCK = 64
SEL_TOPN = 16
WINDOW = 512
ROPE_THETA = 500000.0
ROPE_DIM = NSA_HEAD_DIM // 4
SGU_GROUPS = 8
SGU_CHUNK = 128
POOL_WINDOWS = (2, 4, 8, 16)

LANES = 128
VMEM_LIMIT = 56 * 1024 * 1024

BF16 = jnp.bfloat16
F32 = jnp.float32


def _const_spec(shape):
    nd = len(shape)
    return pl.BlockSpec(shape, lambda *_: (0,) * nd, pipeline_mode=pl.Buffered(1))


def _rmsnorm(x, g):
    ms = jnp.mean(x * x, axis=-1, keepdims=True)
    return x * lax.rsqrt(ms + RMS_EPS) * g


def _dot(a, b):
    return jnp.dot(a, b, preferred_element_type=F32)


VT_ROWS = NSA_HEAD_DIM + 16


def _rope(t, c, s1, s2):
    return t * c + pltpu.roll(t, LANES - ROPE_DIM // 2, axis=1) * s1 + pltpu.roll(t, ROPE_DIM // 2, axis=1) * s2


def _nsa_in_kernel(x_ref, g_ref, w_ref, rot_ref,
                   q_ref, ks_ref, vst_ref, kw_ref, vwt_ref, kcv_ref, gate_ref, *, d_model, n_groups):
    tm = x_ref.shape[0]
    dh = NSA_HEAD_DIM
    half = ROPE_DIM // 2
    xn = _rmsnorm(x_ref[...], g_ref[...]).astype(BF16)
    y = _dot(xn, w_ref[...])
    r = rot_ref[...]
    lane = lax.broadcasted_iota(jnp.int32, r.shape, 1)
    d = jnp.bitwise_and(lane, dh - 1)
    first, second = d < half, (d >= half) & (d < ROPE_DIM)
    c = jnp.where(first, r, jnp.where(second, pltpu.roll(r, half, axis=1), 1.0))
    s1 = jnp.where(first, -pltpu.roll(r, LANES - half, axis=1), 0.0)
    s2 = jnp.where(second, r, 0.0)
    khalf = lane < dh
    ck, s1k, s2k = jnp.where(khalf, c, 1.0), jnp.where(khalf, s1, 0.0), jnp.where(khalf, s2, 0.0)
    scale = dh ** -0.5 * LOG2_E
    for j in range(d_model // LANES):
        sl = slice(j * LANES, (j + 1) * LANES)
        q_ref[:, sl] = (_rope(y[:, sl], c, s1, s2) * scale).astype(BF16)
    ones = jnp.ones((VT_ROWS - dh, LANES), F32)
    tok = pl.program_id(1) * tm + lax.broadcasted_iota(jnp.int32, c.shape, 0)
    blk_onehot = jnp.where(lane - dh == tok // SEL_BLOCK, 1.0, 0.0)
    off = d_model
    for k_ref, vt_ref, tag_blocks in ((ks_ref, vst_ref, True), (kw_ref, vwt_ref, False)):
        for g in range(n_groups):
            slab = _rope(y[:, off + g * LANES: off + (g + 1) * LANES], ck, s1k, s2k)
            k_ref[g] = (jnp.where(khalf, slab, blk_onehot) if tag_blocks else slab).astype(BF16)
            for ch in range(tm // LANES):
                vt = slab[ch * LANES:(ch + 1) * LANES, :].T[dh:, :]
                vt_ref[g, ch] = jnp.concatenate([vt, ones], axis=0).astype(BF16)
        off += n_groups * LANES
    for g in range(n_groups):
        kcv_ref[g] = _rope(y[:, off + g * LANES: off + (g + 1) * LANES], ck, s1k, s2k)
    off += n_groups * LANES
    gate_ref[...] = jax.nn.sigmoid(y[:, off:off + LANES])


def _nsa_in(x, g, w_perm, rot_tab, *, tm=1024):
    B, S, D = x.shape
    G = NSA_KV_GROUPS
    n_out = w_perm.shape[1]
    tok = lambda w: pl.BlockSpec((None, tm, w), lambda b, i: (b, i, 0))
    grp = pl.BlockSpec((None, G, tm, LANES), lambda b, i: (b, 0, i, 0))
    grp_t = pl.BlockSpec((None, G, tm // LANES, VT_ROWS, LANES), lambda b, i: (b, 0, i, 0, 0))
    k_shape = jax.ShapeDtypeStruct((B, G, S, LANES), BF16)
    vt_shape = jax.ShapeDtypeStruct((B, G, S // LANES, VT_ROWS, LANES), BF16)
    return pl.pallas_call(
        functools.partial(_nsa_in_kernel, d_model=D, n_groups=G),
        grid=(B, S // tm),
        in_specs=[tok(D), _const_spec((1, D)), _const_spec((D, n_out)), tok(LANES)],
        out_specs=[tok(D), grp, grp_t, grp, grp_t, grp, tok(LANES)],
        out_shape=[jax.ShapeDtypeStruct((B, S, D), BF16), k_shape, vt_shape, k_shape, vt_shape,
                   jax.ShapeDtypeStruct((B, G, S, LANES), F32),
                   jax.ShapeDtypeStruct((B, S, LANES), F32)],
        compiler_params=pltpu.CompilerParams(
            dimension_semantics=("parallel", "parallel"), vmem_limit_bytes=VMEM_LIMIT),
        name="nsa_in",
    )(x, g, w_perm, rot_tab)


def _nsa_cmp_kernel(x_ref, pa_ref, pb_ref, wa_ref, wb_ref, wk2_ref, wv2_ref, kc_ref, vct_ref, *, n_chunks):
    acc_a = jnp.zeros((n_chunks, wa_ref.shape[-1]), F32)
    acc_b = jnp.zeros((n_chunks, wb_ref.shape[-1]), F32)
    for lp in range(CMP_STRIDE // 2):
        pair = (2 * lp, 2 * lp + 1)
        xs = [x_ref[pl.ds(l, n_chunks, stride=CMP_STRIDE), :] for l in pair]
        xa = jnp.concatenate([(x + pa_ref[l:l + 1, :]).astype(BF16) for x, l in zip(xs, pair)], axis=1)
        xb = jnp.concatenate([(x + pb_ref[l:l + 1, :]).astype(BF16) for x, l in zip(xs, pair)], axis=1)
        acc_a = acc_a + _dot(xa, wa_ref[lp])
        acc_b = acc_b + _dot(xb, wb_ref[lp])
    hid = jax.nn.gelu(acc_a + pltpu.roll(acc_b, n_chunks - 1, axis=0)).astype(BF16)
    hw = hid.shape[1] // 2
    kv = _dot(hid[:, :hw], wk2_ref[...]) + _dot(hid[:, hw:], wv2_ref[...])
    kc_ref[...] = kv.astype(BF16)
    vct_ref[...] = kv.T[NSA_HEAD_DIM:, :].astype(BF16)


def _nsa_cmp(kcv, pa, pb, wa, wb, wk2x, wv2x):
    B, G, S, _ = kcv.shape
    NC = S // CMP_STRIDE
    return pl.pallas_call(
        functools.partial(_nsa_cmp_kernel, n_chunks=NC),
        grid=(B, G),
        in_specs=[pl.BlockSpec((None, None, S, LANES), lambda b, g: (b, g, 0, 0)),
                  _const_spec(pa.shape), _const_spec(pb.shape), _const_spec(wa.shape), _const_spec(wb.shape),
                  _const_spec(wk2x.shape), _const_spec(wv2x.shape)],
        out_specs=[pl.BlockSpec((None, None, NC, LANES), lambda b, g: (b, g, 0, 0)),
                   pl.BlockSpec((None, None, NSA_HEAD_DIM, NC), lambda b, g: (b, g, 0, 0))],
        out_shape=[jax.ShapeDtypeStruct((B, G, NC, LANES), BF16),
                   jax.ShapeDtypeStruct((B, G, NSA_HEAD_DIM, NC), BF16)],
        compiler_params=pltpu.CompilerParams(
            dimension_semantics=("parallel", "parallel"), vmem_limit_bytes=VMEM_LIMIT),
        name="nsa_cmp",
    )(kcv, pa, pb, wa, wb, wk2x, wv2x)


def _nsa_attn_kernel(q_ref, kc_ref, vct_ref, ks_ref, vst_ref, kw_ref, vwt_ref, gate_ref, ovt_ref, o_ref,
                     q4w, q4s, acc_s, m_s, acc_w, m_w, ocmp, imp_ref, rank_ref, gt, *, tq, n_sel, n_grp, unroll):
    P, dh = NSA_HPG, NSA_HEAD_DIM
    H = n_grp * P
    i = pl.program_id(2)
    qs = i * tq
    head_cols = [slice(h * tq, (h + 1) * tq) for h in range(H)]
    grp_of = [h // P for h in range(H)]

    qt = q_ref[...].astype(F32).T
    zpad = jnp.zeros((LANES - dh, tq), BF16)
    for h, cols in enumerate(head_cols):
        qh = qt[h * dh:(h + 1) * dh].astype(BF16)
        q4w[0:dh, cols] = qh
        q4w[dh:, cols] = zpad
        q4s[0:dh, cols] = qh

    t_q = qs + lax.broadcasted_iota(jnp.int32, (1, tq), 1)

    tk = tq
    n_pre = WINDOW // tk
    sub_i = lax.broadcasted_iota(jnp.int32, (SEL_BLOCK, tq), 0)

    def scores_of(k_ref, k0, q4):
        kts = [k_ref[n, pl.ds(k0, tk), :] for n in range(n_grp)]
        return [_dot(kts[grp_of[h]], q4[:, cols]) for h, cols in enumerate(head_cols)]

    def absorb(scores, vt_ref, k0, acc, m_ref, keep=None, shift=None, heads=None):
        heads = range(H) if heads is None else heads
        j0 = k0 // LANES
        vts = [jnp.concatenate([vt_ref[n, j0 + c] for c in range(tk // LANES)], axis=1) for n in range(n_grp)]
        n_slab = tk // SEL_BLOCK
        keeps = None if keep is None else [keep(k0 + b * SEL_BLOCK + sub_i) for b in range(n_slab)]
        for h in heads:
            cols = head_cols[h]
            slabs = []
            for b in range(n_slab):
                sb = scores[h][b * SEL_BLOCK:(b + 1) * SEL_BLOCK]
                if shift is not None:
                    sb = sb + shift
                if keeps is not None:
                    sb = jnp.where(keeps[b], sb, NEG_INF)
                slabs.append(sb)
            top = slabs[0]
            for sb in slabs[1:]:
                top = jnp.maximum(top, sb)
            m_old = m_ref[0:1, cols]
            m_new = jnp.maximum(m_old, jnp.max(top, axis=0, keepdims=True))
            alpha = jnp.exp2(m_old - m_new)
            pr = jnp.concatenate([jnp.exp2(sb - m_new).astype(BF16) for sb in slabs], axis=0)
            acc[:, cols] = acc[:, cols] * alpha + _dot(vts[grp_of[h]], pr)
            m_ref[0:1, cols] = m_new

    for m_ref, acc in ((m_s, acc_s), (m_w, acc_w)):
        m_ref[...] = jnp.full(m_ref.shape, NEG_INF, F32)
        acc[...] = jnp.zeros(acc.shape, F32)

    nc = kc_ref.shape[1]
    sc_all = [_dot(kc_ref[grp_of[h]], q4w[:, cols]) for h, cols in enumerate(head_cols)]
    win = []
    for d in range(n_pre, -1, -1):
        j = i - d
        k0 = pl.multiple_of(jnp.maximum(j, 0) * tk, tk)
        win.append((d, j, k0, scores_of(kw_ref, k0, q4w)))

    n_i = lax.broadcasted_iota(jnp.int32, (nc, tq), 0)
    cbias = jnp.where(n_i * CMP_STRIDE + (CMP_BLOCK - 1) <= t_q, 0.0, NEG_INF)
    has_c = t_q >= CMP_BLOCK - 1
    pcs = []
    for sc in sc_all:
        sc = sc + cbias
        ec = jnp.exp2(sc - jnp.max(sc, axis=0, keepdims=True))
        pcs.append(ec * jnp.where(has_c, 1.0 / jnp.sum(ec, axis=0, keepdims=True), 0.0))
    for h, cols in enumerate(head_cols):
        ocmp[:, cols] = _dot(vct_ref[grp_of[h]], pcs[h].astype(BF16))

    ovt = ovt_ref[...]
    m_i = lax.broadcasted_iota(jnp.int32, (n_sel, tq), 0)
    cur = jnp.right_shift(t_q, SEL_BLOCK.bit_length() - 1)
    forced = (m_i == 0) | (m_i == cur) | (m_i == cur - 1)
    in_past = m_i * SEL_BLOCK <= t_q
    sub8 = lax.broadcasted_iota(jnp.int32, (8, tq), 0)
    for n in range(n_grp):
        psum = (pcs[n * P] + pcs[n * P + 1]) + (pcs[n * P + 2] + pcs[n * P + 3])
        hi = psum.astype(BF16)
        r1 = psum - hi.astype(F32)
        mid = r1.astype(BF16)
        lo = (r1 - mid.astype(F32)).astype(BF16)
        imp = (_dot(ovt, hi) + _dot(ovt, mid) + _dot(ovt, lo))[0:n_sel]
        imp = jnp.where(forced, FORCE_SCORE, imp)
        imp_ref[n] = jnp.where(in_past, imp, NEG_INF)
        rank_ref[n] = jnp.zeros((n_sel, tq), F32)

    causal = lambda kpos: kpos <= t_q
    for d, j, k0, s_win in win:
        if d == n_pre:
            lowest = t_q - WINDOW + jnp.where(j >= 0, 0, NO_KEY)
            absorb(s_win, vwt_ref, k0, acc_w, m_w, keep=lambda kpos: kpos > lowest)
        elif d == 0:
            absorb(s_win, vwt_ref, k0, acc_w, m_w, keep=causal)
        else:
            absorb(s_win, vwt_ref, k0, acc_w, m_w, shift=jnp.where(j >= 0, 0.0, NEG_INF))

    n_rg = n_sel // 8
    last_rg = ((qs + tq - 1) // SEL_BLOCK) // 8
    for rp in range(n_rg):
        @pl.when(rp <= last_rg)
        def _(rp=rp):
            for n in range(n_grp):
                groups = [imp_ref[n, 8 * r:8 * r + 8, :] for r in range(n_rg)]
                ranks = [rank_ref[n, 8 * r:8 * r + 8, :] for r in range(n_rg)]
                for mp in range(8 * rp, 8 * rp + 8):
                    row = imp_ref[n, mp:mp + 1, :]
                    for r, vr in enumerate(groups):
                        if 8 * r > mp:
                            beats = row >= vr
                        elif 8 * r + 7 < mp:
                            beats = row > vr
                        else:
                            beats = (row > vr) | ((row == vr) & (sub8 + 8 * r > mp))
                        ranks[r] = ranks[r] + jnp.where(beats, 1.0, 0.0)
                for r in range(n_rg):
                    rank_ref[n, 8 * r:8 * r + 8, :] = ranks[r]

    for n in range(n_grp):
        sel_bias = jnp.where(rank_ref[n] < float(min(SEL_TOPN, n_sel)), 0.0, NEG_INF)
        if n_sel < LANES - dh:
            sel_bias = jnp.concatenate([sel_bias, jnp.zeros((LANES - dh - n_sel, tq), F32)], axis=0)
        for cols in head_cols[n * P:(n + 1) * P]:
            q4s[dh:, cols] = sel_bias.astype(BF16)

    def sel_tiles(k0s):
        sc = [scores_of(ks_ref, k0, q4s) for k0 in k0s]
        for h in range(H):
            for k0, s in zip(k0s, sc):
                absorb(s, vst_ref, k0, acc_s, m_s, heads=(h,))

    def main_body(a, c):
        sel_tiles([pl.multiple_of((a * unroll + u) * tk, tk) for u in range(unroll)])
        return c

    lax.fori_loop(0, i // unroll, main_body, 0)
    bit = unroll // 2
    while bit >= 1:
        first = i - jnp.bitwise_and(i, 2 * bit - 1)

        @pl.when(jnp.bitwise_and(i, bit) != 0)
        def _(first=first, bit=bit):
            sel_tiles([pl.multiple_of((first + u) * tk, tk) for u in range(bit)])

        bit //= 2

    k_diag = pl.multiple_of(qs, tk)
    s_diag = scores_of(ks_ref, k_diag, q4s)
    o_w = acc_w[0:dh, :] * (1.0 / acc_w[dh:dh + 1, :])
    gt[...] = gate_ref[...].T
    absorb(s_diag, vst_ref, k_diag, acc_s, m_s, keep=causal)
    o_s = acc_s[0:dh, :] * (1.0 / acc_s[dh:dh + 1, :])

    h0 = H * pl.program_id(1)
    outs = []
    for h, cols in enumerate(head_cols):
        outs.append(gt[pl.ds(h0 + h, 1), :] * ocmp[:, cols]
                    + gt[pl.ds(NSA_HEADS + h0 + h, 1), :] * o_s[:, cols]
                    + gt[pl.ds(2 * NSA_HEADS + h0 + h, 1), :] * o_w[:, cols])
    o_ref[...] = jnp.concatenate(outs, axis=0).T.astype(BF16)


def _nsa_attn(q, kc, vct, ks, vst, kw, vwt, gates, ovt, *, tq=256, n_grp=4, unroll=2):
    B, S, D = q.shape
    G, P, dh = NSA_KV_GROUPS, NSA_HPG, NSA_HEAD_DIM
    NC = kc.shape[2]
    n_sel = S // SEL_BLOCK
    assert tq % LANES == 0 and tq & (tq - 1) == 0 and WINDOW % tq == 0 and G % n_grp == 0
    assert n_sel % 16 == 0 and n_sel <= LANES - dh and SEL_BLOCK & (SEL_BLOCK - 1) == 0
    L = n_grp * P * tq
    qblk = pl.BlockSpec((None, tq, n_grp * P * dh), lambda b, g, i: (b, i, g))
    grp = lambda *tail: pl.BlockSpec((None, n_grp) + tail, lambda b, g, i: (b, g) + (0,) * len(tail),
                                     pipeline_mode=pl.Buffered(1))
    seq, seq_t = grp(S, LANES), grp(S // LANES, VT_ROWS, LANES)
    return pl.pallas_call(
        functools.partial(_nsa_attn_kernel, tq=tq, n_sel=n_sel, n_grp=n_grp, unroll=unroll),
        grid=(B, G // n_grp, S // tq),
        in_specs=[qblk, grp(NC, LANES), grp(dh, NC), seq, seq_t, seq, seq_t,
                  pl.BlockSpec((None, tq, LANES), lambda b, g, i: (b, i, 0)),
                  _const_spec(ovt.shape)],
        out_specs=qblk,
        out_shape=jax.ShapeDtypeStruct((B, S, D), BF16),
        scratch_shapes=[pltpu.VMEM((LANES, L), BF16),
                        pltpu.VMEM((LANES, L), BF16),
                        pltpu.VMEM((VT_ROWS, L), F32),
                        pltpu.VMEM((8, L), F32),
                        pltpu.VMEM((VT_ROWS, L), F32),
                        pltpu.VMEM((8, L), F32),
                        pltpu.VMEM((dh, L), F32),
                        pltpu.VMEM((n_grp, n_sel, tq), F32),
                        pltpu.VMEM((n_grp, n_sel, tq), F32),
                        pltpu.VMEM((LANES, tq), F32)],
        compiler_params=pltpu.CompilerParams(
            dimension_semantics=("parallel", "parallel", "arbitrary"), vmem_limit_bytes=VMEM_LIMIT),
        name="nsa_attn",
    )(q, kc, vct, ks, vst, kw, vwt, gates, ovt)


def _sgu_kernel(x_ref, g_ref, w_ref, lng_ref, lnb_ref, ws_ref, bs_ref, o_ref, *, width):
    T = SGU_CHUNK
    tm = x_ref.shape[0]
    xn = _rmsnorm(x_ref[...], g_ref[...]).astype(BF16)
    y = jax.nn.gelu(_dot(xn, w_ref[...]))
    u, v = y[:, :width], y[:, width:]
    mu = jnp.mean(v, axis=-1, keepdims=True)
    vc = v - mu
    var = jnp.mean(vc * vc, axis=-1, keepdims=True)
    v = (vc * lax.rsqrt(var + LN_EPS) * lng_ref[...] + lnb_ref[...]).astype(BF16)
    r_i = lax.broadcasted_iota(jnp.int32, (T, T), 0)
    c_i = lax.broadcasted_iota(jnp.int32, (T, T), 1)
    gd = width // SGU_GROUPS
    for gi in range(SGU_GROUPS):
        ws = jnp.where(r_i >= c_i, ws_ref[gi], 0.0).astype(BF16)
        cols = slice(gi * gd, (gi + 1) * gd)
        for c in range(tm // T):
            rws = slice(c * T, (c + 1) * T)
            sv = _dot(ws, v[rws, cols]) + bs_ref[gi]
            o_ref[rws, cols] = (u[rws, cols] * sv).astype(BF16)


def _sgu(x2, g, w_in, ln_g, ln_b, w_s, bs_b, *, tm=1024):
    N, D = x2.shape
    W = w_in.shape[1] // 2
    tok = lambda w: pl.BlockSpec((tm, w), lambda i: (i, 0))
    return pl.pallas_call(
        functools.partial(_sgu_kernel, width=W),
        grid=(N // tm,),
        in_specs=[tok(D), _const_spec((1, D)), _const_spec(w_in.shape), _const_spec((1, W)), _const_spec((1, W)),
                  _const_spec(w_s.shape), _const_spec(bs_b.shape)],
        out_specs=tok(W),
        out_shape=jax.ShapeDtypeStruct((N, W), BF16),
        compiler_params=pltpu.CompilerParams(dimension_semantics=("parallel",), vmem_limit_bytes=VMEM_LIMIT),
        name="sgu",
    )(x2, g, w_in, ln_g, ln_b, w_s, bs_b)


def _pool_stage_start(lo, span):
    return -(-(lo + span) // 8) * 8


def _pool_kernel(x_ref, g_ref, w_ref, wg_ref, sc_ref, o_ref, zbuf, sum_a, sum_b, *, hist):
    tm = x_ref.shape[0]
    i = pl.program_id(1)
    rows = hist + tm

    @pl.when(i == 0)
    def _():
        zbuf[0:hist, :] = jnp.zeros((hist, zbuf.shape[1]), F32)

    xn = _rmsnorm(x_ref[...], g_ref[...]).astype(BF16)
    z = _dot(xn, w_ref[...])
    zbuf[hist:rows, :] = z
    t = i * tm + lax.broadcasted_iota(jnp.int32, (tm, 1), 0)
    gd = z.shape[1] // len(POOL_WINDOWS)
    src, span, lo = zbuf, 1, 0
    for s, w in enumerate(POOL_WINDOWS):
        assert w == 2 * span
        lo = _pool_stage_start(lo, span)
        dst = sum_a if s % 2 == 0 else sum_b
        c0 = s * gd
        dst[lo:rows, c0:] = src[lo:rows, c0:] + src[lo - span:rows - span, c0:]
        src, span = dst, w
    assert lo <= hist
    for gi, w in enumerate(POOL_WINDOWS):
        cols = slice(gi * gd, (gi + 1) * gd)
        win = (sum_a if gi % 2 == 0 else sum_b)[hist:rows, cols]
        cnt = jnp.minimum(t + 1, w).astype(F32)
        d = win / cnt - z[:, cols]
        y = _dot(d.astype(BF16), wg_ref[gi]) * sc_ref[:, cols]
        o_ref[:, cols] = y.astype(BF16)
    zbuf[0:hist, :] = zbuf[tm:rows, :]


def _pool(x, g, w_in, w_grp, scale, *, tm=1024):
    B, S, D = x.shape
    PW = w_in.shape[1]
    hist, span = 0, 1
    for w in POOL_WINDOWS:
        hist, span = _pool_stage_start(hist, span), w
    assert tm >= hist
    tok = lambda w: pl.BlockSpec((None, tm, w), lambda b, i: (b, i, 0))
    return pl.pallas_call(
        functools.partial(_pool_kernel, hist=hist),
        grid=(B, S // tm),
        in_specs=[tok(D), _const_spec((1, D)), _const_spec(w_in.shape), _const_spec(w_grp.shape),
                  _const_spec((1, PW))],
        out_specs=tok(PW),
        out_shape=jax.ShapeDtypeStruct((B, S, PW), BF16),
        scratch_shapes=[pltpu.VMEM((hist + tm, PW), F32)] * 3,
        compiler_params=pltpu.CompilerParams(
            dimension_semantics=("arbitrary", "arbitrary"), vmem_limit_bytes=VMEM_LIMIT),
        name="pool",
    )(x, g, w_in, w_grp, scale)


def _post_kernel(x_ref, u_ref, wo_ref, g_ref, wgu_ref, wd_ref, fg_ref, o_ref, *, hidden, final):
    x1 = x_ref[...] + _dot(u_ref[...], wo_ref[...])
    xn = _rmsnorm(x1, g_ref[...]).astype(BF16)
    gate = _dot(xn, wgu_ref[:, :hidden])
    up = _dot(xn, wgu_ref[:, hidden:])
    a = (jax.nn.silu(gate) * up).astype(BF16)
    x2 = x1 + _dot(a, wd_ref[...])
    if final:
        x2 = _rmsnorm(x2, fg_ref[...])
    o_ref[...] = x2


def _post(x2, u2, w_out, g, w_gu_all, w_down_all, fg, *, layer, final, tm=512):
    N, D = x2.shape
    hidden = w_down_all.shape[1]
    tok = lambda w: pl.BlockSpec((tm, w), lambda i: (i, 0))
    of_layer = lambda a: pl.BlockSpec((None,) + a.shape[1:], lambda i: (layer, 0, 0), pipeline_mode=pl.Buffered(1))
    return pl.pallas_call(
        functools.partial(_post_kernel, hidden=hidden, final=final),
        grid=(N // tm,),
        in_specs=[tok(D), tok(u2.shape[1]), _const_spec(w_out.shape), _const_spec((1, D)),
                  of_layer(w_gu_all), of_layer(w_down_all), _const_spec((1, D))],
        out_specs=tok(D),
        out_shape=jax.ShapeDtypeStruct((N, D), F32),
        compiler_params=pltpu.CompilerParams(dimension_semantics=("parallel",), vmem_limit_bytes=VMEM_LIMIT),
        name="post",
    )(x2, u2, w_out, g, w_gu_all, w_down_all, fg)


def _rope_table(positions):
    inv_freq = ROPE_THETA ** (-jnp.arange(0, ROPE_DIM, 2, dtype=F32) / ROPE_DIM)
    ang = positions.astype(F32)[..., None] * inv_freq
    pad = jnp.zeros(ang.shape[:-1] + (NSA_HEAD_DIM - ROPE_DIM,), F32)
    return jnp.tile(jnp.concatenate([jnp.cos(ang), jnp.sin(ang), pad], axis=-1), (1, 1, LANES // NSA_HEAD_DIM))


def _permute_nsa_w_in(w):
    H, G, dh = NSA_HEADS, NSA_KV_GROUPS, NSA_HEAD_DIM
    base = H * dh
    seg = lambda k: w[:, base + k * G * dh: base + (k + 1) * G * dh]
    kc, vc, ks, vs, kw, vw = (seg(k) for k in range(6))
    gl = w[:, base + 6 * G * dh:]
    inter = lambda a, b: jnp.concatenate(
        [a.reshape(-1, G, 1, dh), b.reshape(-1, G, 1, dh)], axis=2).reshape(-1, 2 * G * dh)
    glp = jnp.pad(gl, ((0, 0), (0, LANES - gl.shape[1])))
    return jnp.concatenate([w[:, :base], inter(ks, vs), inter(kw, vw), inter(kc, vc), glp], axis=1).astype(BF16)


def _overlap_matrix_t(n_chunks, n_sel):
    n = np.arange(n_chunks)[None, :]
    m = np.arange(LANES)[:, None]
    start, end = n * CMP_STRIDE, n * CMP_STRIDE + CMP_BLOCK - 1
    ov = (start <= m * SEL_BLOCK + SEL_BLOCK - 1) & (end >= m * SEL_BLOCK) & (m < n_sel)
    return jnp.asarray(ov, BF16)


def _cmp_weights(k_w1, v_w1):
    dh = NSA_HEAD_DIM
    kw = k_w1.reshape(CMP_BLOCK, dh, -1)
    vw = v_w1.reshape(CMP_BLOCK, dh, -1)
    z = jnp.zeros_like(kw)
    w = jnp.concatenate([jnp.concatenate([kw, z], axis=2), jnp.concatenate([z, vw], axis=2)], axis=1).astype(BF16)
    w = w.reshape(CMP_BLOCK // 2, 2 * w.shape[1], w.shape[2])
    return w[:CMP_STRIDE // 2], w[CMP_STRIDE // 2:]


def _nsa_layer(x, rot_tab, g, w_in, pos_k, k_w1, k_w2, pos_v, v_w1, v_w2):
    B, S, D = x.shape
    dh = NSA_HEAD_DIM
    assert CMP_BLOCK == 2 * CMP_STRIDE
    q, ks, vst, kw, vwt, kcv, gates = _nsa_in(x, g.reshape(1, D), _permute_nsa_w_in(w_in), rot_tab)
    pos = jnp.concatenate([pos_k, pos_v], axis=1)
    wa, wb = _cmp_weights(k_w1, v_w1)
    pad_r = lambda w2: jnp.pad(w2, ((0, 0), (0, LANES - dh))).astype(BF16)
    pad_l = lambda w2: jnp.pad(w2, ((0, 0), (LANES - dh, 0))).astype(BF16)
    kc, vct = _nsa_cmp(kcv, pos[:CMP_STRIDE], pos[CMP_STRIDE:], wa, wb, pad_r(k_w2), pad_l(v_w2))
    ovt = _overlap_matrix_t(S // CMP_STRIDE, S // SEL_BLOCK)
    return _nsa_attn(q, kc, vct, ks, vst, kw, vwt, gates, ovt)


def kernel(x, positions, mix_norm, ffn_norm, final_norm, nsa_w_in, nsa_cmp_pos_k, nsa_cmp_k_w1, nsa_cmp_k_w2, nsa_cmp_pos_v, nsa_cmp_v_w1, nsa_cmp_v_w2, nsa_w_out, sgu_w_in, sgu_ln_g, sgu_ln_b, sgu_w_s, sgu_b_s, sgu_w_out, pool_w_in, pool_w_grp, pool_scale, pool_w_out, ffn_w_gate_up, ffn_w_down):
    B, S, D = x.shape
    depth = mix_norm.shape[0]
    rot_tab = _rope_table(positions)
    row = lambda v: v.reshape(1, -1)
    w_gu_all, w_down_all = ffn_w_gate_up.astype(BF16), ffn_w_down.astype(BF16)
    for i in range(depth):
        kind, j = i % N_MIXERS, i // N_MIXERS
        if kind == 0:
            u = _nsa_layer(x, rot_tab, mix_norm[i], nsa_w_in[j], nsa_cmp_pos_k[j], nsa_cmp_k_w1[j], nsa_cmp_k_w2[j],
                           nsa_cmp_pos_v[j], nsa_cmp_v_w1[j], nsa_cmp_v_w2[j])
            w_out = nsa_w_out[j]
        elif kind == 1:
            gd = sgu_w_in.shape[-1] // 2 // SGU_GROUPS
            bs_b = jnp.broadcast_to(sgu_b_s[j][:, :, None], sgu_b_s[j].shape + (gd,))
            u = _sgu(x.reshape(B * S, D), row(mix_norm[i]), sgu_w_in[j].astype(BF16), row(sgu_ln_g[j]),
                     row(sgu_ln_b[j]), sgu_w_s[j], bs_b)
            w_out = sgu_w_out[j]
        else:
            u = _pool(x, row(mix_norm[i]), pool_w_in[j].astype(BF16), pool_w_grp[j].astype(BF16),
                      row(pool_scale[j]))
            w_out = pool_w_out[j]
        x = _post(x.reshape(B * S, D), u.reshape(B * S, -1), w_out.astype(BF16), row(ffn_norm[i]),
                  w_gu_all, w_down_all, row(final_norm), layer=i, final=(i == depth - 1)).reshape(B, S, D)
    return x
```

```python
import functools

import jax
import jax.numpy as jnp
import numpy as np
from jax import lax
from jax.experimental import pallas as pl
from jax.experimental.pallas import tpu as pltpu

RMS_EPS = 1e-6
LN_EPS = 1e-5
NEG_INF = -1e30
FORCE_SCORE = 1e9
LOG2_E = 1.4426950408889634
NO_KEY = 1 << 30

N_MIXERS = 3
NSA_HEADS = 16
NSA_KV_GROUPS = 4
NSA_HPG = NSA_HEADS // NSA_KV_GROUPS
NSA_HEAD_DIM = 64
CMP_BLOCK = 32
CMP_STRIDE = 16
SEL_BLOCK = 64
SEL_TOPN = 16
WINDOW = 512
ROPE_THETA = 500000.0
ROPE_DIM = NSA_HEAD_DIM // 4
SGU_GROUPS = 8
SGU_CHUNK = 128
POOL_WINDOWS = (2, 4, 8, 16)

LANES = 128
VMEM_LIMIT = 56 * 1024 * 1024

BF16 = jnp.bfloat16
F32 = jnp.float32


def _const_spec(shape):
    nd = len(shape)
    return pl.BlockSpec(shape, lambda *_: (0,) * nd, pipeline_mode=pl.Buffered(1))


def _rmsnorm(x, g):
    ms = jnp.mean(x * x, axis=-1, keepdims=True)
    return x * lax.rsqrt(ms + RMS_EPS) * g


def _dot(a, b):
    return jnp.dot(a, b, preferred_element_type=F32)


VT_ROWS = NSA_HEAD_DIM + 16


def _rope(t, c, s1, s2):
    return t * c + pltpu.roll(t, LANES - ROPE_DIM // 2, axis=1) * s1 + pltpu.roll(t, ROPE_DIM // 2, axis=1) * s2


def _nsa_in_kernel(x_ref, g_ref, w_ref, rot_ref,
                   q_ref, ks_ref, vst_ref, kw_ref, vwt_ref, kcv_ref, gate_ref, *, d_model, n_groups):
    tm = x_ref.shape[0]
    dh = NSA_HEAD_DIM
    half = ROPE_DIM // 2
    xn = _rmsnorm(x_ref[...], g_ref[...]).astype(BF16)
    y = _dot(xn, w_ref[...])
    r = rot_ref[...]
    lane = lax.broadcasted_iota(jnp.int32, r.shape, 1)
    d = jnp.bitwise_and(lane, dh - 1)
    first, second = d < half, (d >= half) & (d < ROPE_DIM)
    c = jnp.where(first, r, jnp.where(second, pltpu.roll(r, half, axis=1), 1.0))
    s1 = jnp.where(first, -pltpu.roll(r, LANES - half, axis=1), 0.0)
    s2 = jnp.where(second, r, 0.0)
    khalf = lane < dh
    ck, s1k, s2k = jnp.where(khalf, c, 1.0), jnp.where(khalf, s1, 0.0), jnp.where(khalf, s2, 0.0)
    scale = dh ** -0.5 * LOG2_E
    for j in range(d_model // LANES):
        sl = slice(j * LANES, (j + 1) * LANES)
        q_ref[:, sl] = (_rope(y[:, sl], c, s1, s2) * scale).astype(BF16)
    ones = jnp.ones((VT_ROWS - dh, LANES), F32)
    tok = pl.program_id(1) * tm + lax.broadcasted_iota(jnp.int32, c.shape, 0)
    blk_onehot = jnp.where(lane - dh == tok // SEL_BLOCK, 1.0, 0.0)
    off = d_model
    for k_ref, vt_ref, tag_blocks in ((ks_ref, vst_ref, True), (kw_ref, vwt_ref, False)):
        for g in range(n_groups):
            slab = _rope(y[:, off + g * LANES: off + (g + 1) * LANES], ck, s1k, s2k)
            k_ref[g] = (jnp.where(khalf, slab, blk_onehot) if tag_blocks else slab).astype(BF16)
            for ch in range(tm // LANES):
                vt = slab[ch * LANES:(ch + 1) * LANES, :].T[dh:, :]
                vt_ref[g, ch] = jnp.concatenate([vt, ones], axis=0).astype(BF16)
        off += n_groups * LANES
    for g in range(n_groups):
        kcv_ref[g] = _rope(y[:, off + g * LANES: off + (g + 1) * LANES], ck, s1k, s2k)
    off += n_groups * LANES
    gate_ref[...] = jax.nn.sigmoid(y[:, off:off + LANES])


def _nsa_in(x, g, w_perm, rot_tab, *, tm=1024):
    B, S, D = x.shape
    G = NSA_KV_GROUPS
    n_out = w_perm.shape[1]
    tok = lambda w: pl.BlockSpec((None, tm, w), lambda b, i: (b, i, 0))
    grp = pl.BlockSpec((None, G, tm, LANES), lambda b, i: (b, 0, i, 0))
    grp_t = pl.BlockSpec((None, G, tm // LANES, VT_ROWS, LANES), lambda b, i: (b, 0, i, 0, 0))
    k_shape = jax.ShapeDtypeStruct((B, G, S, LANES), BF16)
    vt_shape = jax.ShapeDtypeStruct((B, G, S // LANES, VT_ROWS, LANES), BF16)
    return pl.pallas_call(
        functools.partial(_nsa_in_kernel, d_model=D, n_groups=G),
        grid=(B, S // tm),
        in_specs=[tok(D), _const_spec((1, D)), _const_spec((D, n_out)), tok(LANES)],
        out_specs=[tok(D), grp, grp_t, grp, grp_t, grp, tok(LANES)],
        out_shape=[jax.ShapeDtypeStruct((B, S, D), BF16), k_shape, vt_shape, k_shape, vt_shape,
                   jax.ShapeDtypeStruct((B, G, S, LANES), F32),
                   jax.ShapeDtypeStruct((B, S, LANES), F32)],
        compiler_params=pltpu.CompilerParams(
            dimension_semantics=("parallel", "parallel"), vmem_limit_bytes=VMEM_LIMIT),
        name="nsa_in",
    )(x, g, w_perm, rot_tab)


def _nsa_cmp_kernel(x_ref, pa_ref, pb_ref, wa_ref, wb_ref, wk2_ref, wv2_ref, kc_ref, vct_ref, *, n_chunks):
    acc_a = jnp.zeros((n_chunks, wa_ref.shape[-1]), F32)
    acc_b = jnp.zeros((n_chunks, wb_ref.shape[-1]), F32)
    for lp in range(CMP_STRIDE // 2):
        pair = (2 * lp, 2 * lp + 1)
        xs = [x_ref[pl.ds(l, n_chunks, stride=CMP_STRIDE), :] for l in pair]
        xa = jnp.concatenate([(x + pa_ref[l:l + 1, :]).astype(BF16) for x, l in zip(xs, pair)], axis=1)
        xb = jnp.concatenate([(x + pb_ref[l:l + 1, :]).astype(BF16) for x, l in zip(xs, pair)], axis=1)
        acc_a = acc_a + _dot(xa, wa_ref[lp])
        acc_b = acc_b + _dot(xb, wb_ref[lp])
    hid = jax.nn.gelu(acc_a + pltpu.roll(acc_b, n_chunks - 1, axis=0)).astype(BF16)
    hw = hid.shape[1] // 2
    kv = _dot(hid[:, :hw], wk2_ref[...]) + _dot(hid[:, hw:], wv2_ref[...])
    kc_ref[...] = kv.astype(BF16)
    vct_ref[...] = kv.T[NSA_HEAD_DIM:, :].astype(BF16)


def _nsa_cmp(kcv, pa, pb, wa, wb, wk2x, wv2x):
    B, G, S, _ = kcv.shape
    NC = S // CMP_STRIDE
    return pl.pallas_call(
        functools.partial(_nsa_cmp_kernel, n_chunks=NC),
        grid=(B, G),
        in_specs=[pl.BlockSpec((None, None, S, LANES), lambda b, g: (b, g, 0, 0)),
                  _const_spec(pa.shape), _const_spec(pb.shape), _const_spec(wa.shape), _const_spec(wb.shape),
                  _const_spec(wk2x.shape), _const_spec(wv2x.shape)],
        out_specs=[pl.BlockSpec((None, None, NC, LANES), lambda b, g: (b, g, 0, 0)),
                   pl.BlockSpec((None, None, NSA_HEAD_DIM, NC), lambda b, g: (b, g, 0, 0))],
        out_shape=[jax.ShapeDtypeStruct((B, G, NC, LANES), BF16),
                   jax.ShapeDtypeStruct((B, G, NSA_HEAD_DIM, NC), BF16)],
        compiler_params=pltpu.CompilerParams(
            dimension_semantics=("parallel", "parallel"), vmem_limit_bytes=VMEM_LIMIT),
        name="nsa_cmp",
    )(kcv, pa, pb, wa, wb, wk2x, wv2x)


def _nsa_attn_kernel(q_ref, kc_ref, vct_ref, ks_ref, vst_ref, kw_ref, vwt_ref, gate_ref, ovt_ref, o_ref,
                     q4w, q4s, acc_s, m_s, acc_w, m_w, ocmp, imp_ref, rank_ref, gt, *, tq, n_sel, n_grp, unroll):
    P, dh = NSA_HPG, NSA_HEAD_DIM
    H = n_grp * P
    i = pl.program_id(2)
    qs = i * tq
    head_cols = [slice(h * tq, (h + 1) * tq) for h in range(H)]
    grp_of = [h // P for h in range(H)]

    qt = q_ref[...].astype(F32).T
    zpad = jnp.zeros((LANES - dh, tq), BF16)
    for h, cols in enumerate(head_cols):
        qh = qt[h * dh:(h + 1) * dh].astype(BF16)
        q4w[0:dh, cols] = qh
        q4w[dh:, cols] = zpad
        q4s[0:dh, cols] = qh

    t_q = qs + lax.broadcasted_iota(jnp.int32, (1, tq), 1)

    tk = tq
    n_pre = WINDOW // tk
    sub_i = lax.broadcasted_iota(jnp.int32, (SEL_BLOCK, tq), 0)

    def scores_of(k_ref, k0, q4):
        kts = [k_ref[n, pl.ds(k0, tk), :] for n in range(n_grp)]
        return [_dot(kts[grp_of[h]], q4[:, cols]) for h, cols in enumerate(head_cols)]

    def absorb(scores, vt_ref, k0, acc, m_ref, keep=None, shift=None, heads=None):
        heads = range(H) if heads is None else heads
        j0 = k0 // LANES
        vts = [jnp.concatenate([vt_ref[n, j0 + c] for c in range(tk // LANES)], axis=1) for n in range(n_grp)]
        n_slab = tk // SEL_BLOCK
        keeps = None if keep is None else [keep(k0 + b * SEL_BLOCK + sub_i) for b in range(n_slab)]
        for h in heads:
            cols = head_cols[h]
            slabs = []
            for b in range(n_slab):
                sb = scores[h][b * SEL_BLOCK:(b + 1) * SEL_BLOCK]
                if shift is not None:
                    sb = sb + shift
                if keeps is not None:
                    sb = jnp.where(keeps[b], sb, NEG_INF)
                slabs.append(sb)
            top = slabs[0]
            for sb in slabs[1:]:
                top = jnp.maximum(top, sb)
            m_old = m_ref[0:1, cols]
            m_new = jnp.maximum(m_old, jnp.max(top, axis=0, keepdims=True))
            alpha = jnp.exp2(m_old - m_new)
            pr = jnp.concatenate([jnp.exp2(sb - m_new).astype(BF16) for sb in slabs], axis=0)
            acc[:, cols] = acc[:, cols] * alpha + _dot(vts[grp_of[h]], pr)
            m_ref[0:1, cols] = m_new

    for m_ref, acc in ((m_s, acc_s), (m_w, acc_w)):
        m_ref[...] = jnp.full(m_ref.shape, NEG_INF, F32)
        acc[...] = jnp.zeros(acc.shape, F32)

    nc = kc_ref.shape[1]
    sc_all = [_dot(kc_ref[grp_of[h]], q4w[:, cols]) for h, cols in enumerate(head_cols)]
    win = []
    for d in range(n_pre, -1, -1):
        j = i - d
        k0 = pl.multiple_of(jnp.maximum(j, 0) * tk, tk)
        win.append((d, j, k0, scores_of(kw_ref, k0, q4w)))

    n_i = lax.broadcasted_iota(jnp.int32, (nc, tq), 0)
    cbias = jnp.where(n_i * CMP_STRIDE + (CMP_BLOCK - 1) <= t_q, 0.0, NEG_INF)
    has_c = t_q >= CMP_BLOCK - 1
    pcs = []
    for sc in sc_all:
        sc = sc + cbias
        ec = jnp.exp2(sc - jnp.max(sc, axis=0, keepdims=True))
        pcs.append(ec * jnp.where(has_c, 1.0 / jnp.sum(ec, axis=0, keepdims=True), 0.0))
    for h, cols in enumerate(head_cols):
        ocmp[:, cols] = _dot(vct_ref[grp_of[h]], pcs[h].astype(BF16))

    ovt = ovt_ref[...]
    m_i = lax.broadcasted_iota(jnp.int32, (n_sel, tq), 0)
    cur = jnp.right_shift(t_q, SEL_BLOCK.bit_length() - 1)
    forced = (m_i == 0) | (m_i == cur) | (m_i == cur - 1)
    in_past = m_i * SEL_BLOCK <= t_q
    sub8 = lax.broadcasted_iota(jnp.int32, (8, tq), 0)
    for n in range(n_grp):
        psum = (pcs[n * P] + pcs[n * P + 1]) + (pcs[n * P + 2] + pcs[n * P + 3])
        hi = psum.astype(BF16)
        r1 = psum - hi.astype(F32)
        mid = r1.astype(BF16)
        lo = (r1 - mid.astype(F32)).astype(BF16)
        imp = (_dot(ovt, hi) + _dot(ovt, mid) + _dot(ovt, lo))[0:n_sel]
        imp = jnp.where(forced, FORCE_SCORE, imp)
        imp_ref[n] = jnp.where(in_past, imp, NEG_INF)
        rank_ref[n] = jnp.zeros((n_sel, tq), F32)

    causal = lambda kpos: kpos <= t_q
    for d, j, k0, s_win in win:
        if d == n_pre:
            lowest = t_q - WINDOW + jnp.where(j >= 0, 0, NO_KEY)
            absorb(s_win, vwt_ref, k0, acc_w, m_w, keep=lambda kpos: kpos > lowest)
        elif d == 0:
            absorb(s_win, vwt_ref, k0, acc_w, m_w, keep=causal)
        else:
            absorb(s_win, vwt_ref, k0, acc_w, m_w, shift=jnp.where(j >= 0, 0.0, NEG_INF))

    n_rg = n_sel // 8
    last_rg = ((qs + tq - 1) // SEL_BLOCK) // 8
    for rp in range(n_rg):
        @pl.when(rp <= last_rg)
        def _(rp=rp):
            for n in range(n_grp):
                groups = [imp_ref[n, 8 * r:8 * r + 8, :] for r in range(n_rg)]
                ranks = [rank_ref[n, 8 * r:8 * r + 8, :] for r in range(n_rg)]
                for mp in range(8 * rp, 8 * rp + 8):
                    row = imp_ref[n, mp:mp + 1, :]
                    for r, vr in enumerate(groups):
                        if 8 * r > mp:
                            beats = row >= vr
                        elif 8 * r + 7 < mp:
                            beats = row > vr
                        else:
                            beats = (row > vr) | ((row == vr) & (sub8 + 8 * r > mp))
                        ranks[r] = ranks[r] + jnp.where(beats, 1.0, 0.0)
                for r in range(n_rg):
                    rank_ref[n, 8 * r:8 * r + 8, :] = ranks[r]

    for n in range(n_grp):
        sel_bias = jnp.where(rank_ref[n] < float(min(SEL_TOPN, n_sel)), 0.0, NEG_INF)
        if n_sel < LANES - dh:
            sel_bias = jnp.concatenate([sel_bias, jnp.zeros((LANES - dh - n_sel, tq), F32)], axis=0)
        for cols in head_cols[n * P:(n + 1) * P]:
            q4s[dh:, cols] = sel_bias.astype(BF16)

    def sel_tiles(k0s):
        sc = [scores_of(ks_ref, k0, q4s) for k0 in k0s]
        for h in range(H):
            for k0, s in zip(k0s, sc):
                absorb(s, vst_ref, k0, acc_s, m_s, heads=(h,))

    def main_body(a, c):
        sel_tiles([pl.multiple_of((a * unroll + u) * tk, tk) for u in range(unroll)])
        return c

    lax.fori_loop(0, i // unroll, main_body, 0)
    bit = unroll // 2
    while bit >= 1:
        first = i - jnp.bitwise_and(i, 2 * bit - 1)

        @pl.when(jnp.bitwise_and(i, bit) != 0)
        def _(first=first, bit=bit):
            sel_tiles([pl.multiple_of((first + u) * tk, tk) for u in range(bit)])

        bit //= 2

    k_diag = pl.multiple_of(qs, tk)
    s_diag = scores_of(ks_ref, k_diag, q4s)
    o_w = acc_w[0:dh, :] * (1.0 / acc_w[dh:dh + 1, :])
    gt[...] = gate_ref[...].T
    absorb(s_diag, vst_ref, k_diag, acc_s, m_s, keep=causal)
    o_s = acc_s[0:dh, :] * (1.0 / acc_s[dh:dh + 1, :])

    h0 = H * pl.program_id(1)
    outs = []
    for h, cols in enumerate(head_cols):
        outs.append(gt[pl.ds(h0 + h, 1), :] * ocmp[:, cols]
                    + gt[pl.ds(NSA_HEADS + h0 + h, 1), :] * o_s[:, cols]
                    + gt[pl.ds(2 * NSA_HEADS + h0 + h, 1), :] * o_w[:, cols])
    o_ref[...] = jnp.concatenate(outs, axis=0).T.astype(BF16)


def _nsa_attn(q, kc, vct, ks, vst, kw, vwt, gates, ovt, *, tq=256, n_grp=4, unroll=4):
    B, S, D = q.shape
    G, P, dh = NSA_KV_GROUPS, NSA_HPG, NSA_HEAD_DIM
    NC = kc.shape[2]
    n_sel = S // SEL_BLOCK
    assert tq % LANES == 0 and tq & (tq - 1) == 0 and WINDOW % tq == 0 and G % n_grp == 0
    assert n_sel % 16 == 0 and n_sel <= LANES - dh and SEL_BLOCK & (SEL_BLOCK - 1) == 0
    L = n_grp * P * tq
    qblk = pl.BlockSpec((None, tq, n_grp * P * dh), lambda b, g, i: (b, i, g))
    grp = lambda *tail: pl.BlockSpec((None, n_grp) + tail, lambda b, g, i: (b, g) + (0,) * len(tail),
                                     pipeline_mode=pl.Buffered(1))
    seq, seq_t = grp(S, LANES), grp(S // LANES, VT_ROWS, LANES)
    return pl.pallas_call(
        functools.partial(_nsa_attn_kernel, tq=tq, n_sel=n_sel, n_grp=n_grp, unroll=unroll),
        grid=(B, G // n_grp, S // tq),
        in_specs=[qblk, grp(NC, LANES), grp(dh, NC), seq, seq_t, seq, seq_t,
                  pl.BlockSpec((None, tq, LANES), lambda b, g, i: (b, i, 0)),
                  _const_spec(ovt.shape)],
        out_specs=qblk,
        out_shape=jax.ShapeDtypeStruct((B, S, D), BF16),
        scratch_shapes=[pltpu.VMEM((LANES, L), BF16),
                        pltpu.VMEM((LANES, L), BF16),
                        pltpu.VMEM((VT_ROWS, L), F32),
                        pltpu.VMEM((8, L), F32),
                        pltpu.VMEM((VT_ROWS, L), F32),
                        pltpu.VMEM((8, L), F32),
                        pltpu.VMEM((dh, L), F32),
                        pltpu.VMEM((n_grp, n_sel, tq), F32),
                        pltpu.VMEM((n_grp, n_sel, tq), F32),
                        pltpu.VMEM((LANES, tq), F32)],
        compiler_params=pltpu.CompilerParams(
            dimension_semantics=("parallel", "parallel", "arbitrary"), vmem_limit_bytes=VMEM_LIMIT),
        name="nsa_attn",
    )(q, kc, vct, ks, vst, kw, vwt, gates, ovt)


def _sgu_kernel(x_ref, g_ref, w_ref, lng_ref, lnb_ref, ws_ref, bs_ref, o_ref, *, width):
    T = SGU_CHUNK
    tm = x_ref.shape[0]
    xn = _rmsnorm(x_ref[...], g_ref[...]).astype(BF16)
    y = jax.nn.gelu(_dot(xn, w_ref[...]))
    u, v = y[:, :width], y[:, width:]
    mu = jnp.mean(v, axis=-1, keepdims=True)
    vc = v - mu
    var = jnp.mean(vc * vc, axis=-1, keepdims=True)
    v = (vc * lax.rsqrt(var + LN_EPS) * lng_ref[...] + lnb_ref[...]).astype(BF16)
    r_i = lax.broadcasted_iota(jnp.int32, (T, T), 0)
    c_i = lax.broadcasted_iota(jnp.int32, (T, T), 1)
    gd = width // SGU_GROUPS
    for gi in range(SGU_GROUPS):
        ws = jnp.where(r_i >= c_i, ws_ref[gi], 0.0).astype(BF16)
        cols = slice(gi * gd, (gi + 1) * gd)
        for c in range(tm // T):
            rws = slice(c * T, (c + 1) * T)
            sv = _dot(ws, v[rws, cols]) + bs_ref[gi]
            o_ref[rws, cols] = (u[rws, cols] * sv).astype(BF16)


def _sgu(x2, g, w_in, ln_g, ln_b, w_s, bs_b, *, tm=1024):
    N, D = x2.shape
    W = w_in.shape[1] // 2
    tok = lambda w: pl.BlockSpec((tm, w), lambda i: (i, 0))
    return pl.pallas_call(
        functools.partial(_sgu_kernel, width=W),
        grid=(N // tm,),
        in_specs=[tok(D), _const_spec((1, D)), _const_spec(w_in.shape), _const_spec((1, W)), _const_spec((1, W)),
                  _const_spec(w_s.shape), _const_spec(bs_b.shape)],
        out_specs=tok(W),
        out_shape=jax.ShapeDtypeStruct((N, W), BF16),
        compiler_params=pltpu.CompilerParams(dimension_semantics=("parallel",), vmem_limit_bytes=VMEM_LIMIT),
        name="sgu",
    )(x2, g, w_in, ln_g, ln_b, w_s, bs_b)


def _pool_stage_start(lo, span):
    return -(-(lo + span) // 8) * 8


def _pool_kernel(x_ref, g_ref, w_ref, wg_ref, sc_ref, o_ref, zbuf, sum_a, sum_b, *, hist):
    tm = x_ref.shape[0]
    i = pl.program_id(1)
    rows = hist + tm

    @pl.when(i == 0)
    def _():
        zbuf[0:hist, :] = jnp.zeros((hist, zbuf.shape[1]), F32)

    xn = _rmsnorm(x_ref[...], g_ref[...]).astype(BF16)
    z = _dot(xn, w_ref[...])
    zbuf[hist:rows, :] = z
    t = i * tm + lax.broadcasted_iota(jnp.int32, (tm, 1), 0)
    gd = z.shape[1] // len(POOL_WINDOWS)
    src, span, lo = zbuf, 1, 0
    for s, w in enumerate(POOL_WINDOWS):
        assert w == 2 * span
        lo = _pool_stage_start(lo, span)
        dst = sum_a if s % 2 == 0 else sum_b
        c0 = s * gd
        dst[lo:rows, c0:] = src[lo:rows, c0:] + src[lo - span:rows - span, c0:]
        src, span = dst, w
    assert lo <= hist
    for gi, w in enumerate(POOL_WINDOWS):
        cols = slice(gi * gd, (gi + 1) * gd)
        win = (sum_a if gi % 2 == 0 else sum_b)[hist:rows, cols]
        cnt = jnp.minimum(t + 1, w).astype(F32)
        d = win / cnt - z[:, cols]
        y = _dot(d.astype(BF16), wg_ref[gi]) * sc_ref[:, cols]
        o_ref[:, cols] = y.astype(BF16)
    zbuf[0:hist, :] = zbuf[tm:rows, :]


def _pool(x, g, w_in, w_grp, scale, *, tm=1024):
    B, S, D = x.shape
    PW = w_in.shape[1]
    hist, span = 0, 1
    for w in POOL_WINDOWS:
        hist, span = _pool_stage_start(hist, span), w
    assert tm >= hist
    tok = lambda w: pl.BlockSpec((None, tm, w), lambda b, i: (b, i, 0))
    return pl.pallas_call(
        functools.partial(_pool_kernel, hist=hist),
        grid=(B, S // tm),
        in_specs=[tok(D), _const_spec((1, D)), _const_spec(w_in.shape), _const_spec(w_grp.shape),
                  _const_spec((1, PW))],
        out_specs=tok(PW),
        out_shape=jax.ShapeDtypeStruct((B, S, PW), BF16),
        scratch_shapes=[pltpu.VMEM((hist + tm, PW), F32)] * 3,
        compiler_params=pltpu.CompilerParams(
            dimension_semantics=("arbitrary", "arbitrary"), vmem_limit_bytes=VMEM_LIMIT),
        name="pool",
    )(x, g, w_in, w_grp, scale)


def _post_kernel(x_ref, u_ref, wo_ref, g_ref, wgu_ref, wd_ref, fg_ref, o_ref, *, hidden, final):
    x1 = x_ref[...] + _dot(u_ref[...], wo_ref[...])
    xn = _rmsnorm(x1, g_ref[...]).astype(BF16)
    gate = _dot(xn, wgu_ref[:, :hidden])
    up = _dot(xn, wgu_ref[:, hidden:])
    a = (jax.nn.silu(gate) * up).astype(BF16)
    x2 = x1 + _dot(a, wd_ref[...])
    if final:
        x2 = _rmsnorm(x2, fg_ref[...])
    o_ref[...] = x2


def _post(x2, u2, w_out, g, w_gu_all, w_down_all, fg, *, layer, final, tm=512):
    N, D = x2.shape
    hidden = w_down_all.shape[1]
    tok = lambda w: pl.BlockSpec((tm, w), lambda i: (i, 0))
    of_layer = lambda a: pl.BlockSpec((None,) + a.shape[1:], lambda i: (layer, 0, 0), pipeline_mode=pl.Buffered(1))
    return pl.pallas_call(
        functools.partial(_post_kernel, hidden=hidden, final=final),
        grid=(N // tm,),
        in_specs=[tok(D), tok(u2.shape[1]), _const_spec(w_out.shape), _const_spec((1, D)),
                  of_layer(w_gu_all), of_layer(w_down_all), _const_spec((1, D))],
        out_specs=tok(D),
        out_shape=jax.ShapeDtypeStruct((N, D), F32),
        compiler_params=pltpu.CompilerParams(dimension_semantics=("parallel",), vmem_limit_bytes=VMEM_LIMIT),
        name="post",
    )(x2, u2, w_out, g, w_gu_all, w_down_all, fg)


def _rope_table(positions):
    inv_freq = ROPE_THETA ** (-jnp.arange(0, ROPE_DIM, 2, dtype=F32) / ROPE_DIM)
    ang = positions.astype(F32)[..., None] * inv_freq
    pad = jnp.zeros(ang.shape[:-1] + (NSA_HEAD_DIM - ROPE_DIM,), F32)
    return jnp.tile(jnp.concatenate([jnp.cos(ang), jnp.sin(ang), pad], axis=-1), (1, 1, LANES // NSA_HEAD_DIM))


def _permute_nsa_w_in(w):
    H, G, dh = NSA_HEADS, NSA_KV_GROUPS, NSA_HEAD_DIM
    base = H * dh
    seg = lambda k: w[:, base + k * G * dh: base + (k + 1) * G * dh]
    kc, vc, ks, vs, kw, vw = (seg(k) for k in range(6))
    gl = w[:, base + 6 * G * dh:]
    inter = lambda a, b: jnp.concatenate(
        [a.reshape(-1, G, 1, dh), b.reshape(-1, G, 1, dh)], axis=2).reshape(-1, 2 * G * dh)
    glp = jnp.pad(gl, ((0, 0), (0, LANES - gl.shape[1])))
    return jnp.concatenate([w[:, :base], inter(ks, vs), inter(kw, vw), inter(kc, vc), glp], axis=1).astype(BF16)


def _overlap_matrix_t(n_chunks, n_sel):
    n = np.arange(n_chunks)[None, :]
    m = np.arange(LANES)[:, None]
    start, end = n * CMP_STRIDE, n * CMP_STRIDE + CMP_BLOCK - 1
    ov = (start <= m * SEL_BLOCK + SEL_BLOCK - 1) & (end >= m * SEL_BLOCK) & (m < n_sel)
    return jnp.asarray(ov, BF16)


def _cmp_weights(k_w1, v_w1):
    dh = NSA_HEAD_DIM
    kw = k_w1.reshape(CMP_BLOCK, dh, -1)
    vw = v_w1.reshape(CMP_BLOCK, dh, -1)
    z = jnp.zeros_like(kw)
    w = jnp.concatenate([jnp.concatenate([kw, z], axis=2), jnp.concatenate([z, vw], axis=2)], axis=1).astype(BF16)
    w = w.reshape(CMP_BLOCK // 2, 2 * w.shape[1], w.shape[2])
    return w[:CMP_STRIDE // 2], w[CMP_STRIDE // 2:]


def _nsa_layer(x, rot_tab, g, w_in, pos_k, k_w1, k_w2, pos_v, v_w1, v_w2):
    B, S, D = x.shape
    dh = NSA_HEAD_DIM
    assert CMP_BLOCK == 2 * CMP_STRIDE
    q, ks, vst, kw, vwt, kcv, gates = _nsa_in(x, g.reshape(1, D), _permute_nsa_w_in(w_in), rot_tab)
    pos = jnp.concatenate([pos_k, pos_v], axis=1)
    wa, wb = _cmp_weights(k_w1, v_w1)
    pad_r = lambda w2: jnp.pad(w2, ((0, 0), (0, LANES - dh))).astype(BF16)
    pad_l = lambda w2: jnp.pad(w2, ((0, 0), (LANES - dh, 0))).astype(BF16)
    kc, vct = _nsa_cmp(kcv, pos[:CMP_STRIDE], pos[CMP_STRIDE:], wa, wb, pad_r(k_w2), pad_l(v_w2))
    ovt = _overlap_matrix_t(S // CMP_STRIDE, S // SEL_BLOCK)
    return _nsa_attn(q, kc, vct, ks, vst, kw, vwt, gates, ovt)


def kernel(x, positions, mix_norm, ffn_norm, final_norm, nsa_w_in, nsa_cmp_pos_k, nsa_cmp_k_w1, nsa_cmp_k_w2, nsa_cmp_pos_v, nsa_cmp_v_w1, nsa_cmp_v_w2, nsa_w_out, sgu_w_in, sgu_ln_g, sgu_ln_b, sgu_w_s, sgu_b_s, sgu_w_out, pool_w_in, pool_w_grp, pool_scale, pool_w_out, ffn_w_gate_up, ffn_w_down):
    B, S, D = x.shape
    depth = mix_norm.shape[0]
    rot_tab = _rope_table(positions)
    row = lambda v: v.reshape(1, -1)
    w_gu_all, w_down_all = ffn_w_gate_up.astype(BF16), ffn_w_down.astype(BF16)
    for i in range(depth):
        kind, j = i % N_MIXERS, i // N_MIXERS
        if kind == 0:
            u = _nsa_layer(x, rot_tab, mix_norm[i], nsa_w_in[j], nsa_cmp_pos_k[j], nsa_cmp_k_w1[j], nsa_cmp_k_w2[j],
                           nsa_cmp_pos_v[j], nsa_cmp_v_w1[j], nsa_cmp_v_w2[j])
            w_out = nsa_w_out[j]
        elif kind == 1:
            gd = sgu_w_in.shape[-1] // 2 // SGU_GROUPS
            bs_b = jnp.broadcast_to(sgu_b_s[j][:, :, None], sgu_b_s[j].shape + (gd,))
            u = _sgu(x.reshape(B * S, D), row(mix_norm[i]), sgu_w_in[j].astype(BF16), row(sgu_ln_g[j]),
                     row(sgu_ln_b[j]), sgu_w_s[j], bs_b)
            w_out = sgu_w_out[j]
        else:
            u = _pool(x, row(mix_norm[i]), pool_w_in[j].astype(BF16), pool_w_grp[j].astype(BF16),
                      row(pool_scale[j]))
            w_out = pool_w_out[j]
        x = _post(x.reshape(B * S, D), u.reshape(B * S, -1), w_out.astype(BF16), row(ffn_norm[i]),
                  w_gu_all, w_down_all, row(final_norm), layer=i, final=(i == depth - 1)).reshape(B, S, D)
    return x
```

```python
import functools

import jax
import jax.numpy as jnp
import numpy as np
from jax import lax
from jax.experimental import pallas as pl
from jax.experimental.pallas import tpu as pltpu

RMS_EPS = 1e-6
LN_EPS = 1e-5
NEG_INF = -1e30
FORCE_SCORE = 1e9
LOG2_E = 1.4426950408889634
NO_KEY = 1 << 30

N_MIXERS = 3
NSA_HEADS = 16
NSA_KV_GROUPS = 4
NSA_HPG = NSA_HEADS // NSA_KV_GROUPS
NSA_HEAD_DIM = 64
CMP_BLOCK = 32
CMP_STRIDE = 16
SEL_BLOCK = 64
SEL_TOPN = 16
WINDOW = 512
ROPE_THETA = 500000.0
ROPE_DIM = NSA_HEAD_DIM // 4
SGU_GROUPS = 8
SGU_CHUNK = 128
POOL_WINDOWS = (2, 4, 8, 16)

LANES = 128
VMEM_LIMIT = 56 * 1024 * 1024

BF16 = jnp.bfloat16
F32 = jnp.float32


def _const_spec(shape):
    nd = len(shape)
    return pl.BlockSpec(shape, lambda *_: (0,) * nd, pipeline_mode=pl.Buffered(1))


def _rmsnorm(x, g):
    ms = jnp.mean(x * x, axis=-1, keepdims=True)
    return x * lax.rsqrt(ms + RMS_EPS) * g


def _dot(a, b):
    return jnp.dot(a, b, preferred_element_type=F32)


VT_ROWS = NSA_HEAD_DIM + 16


def _rope(t, c, s1, s2):
    return t * c + pltpu.roll(t, LANES - ROPE_DIM // 2, axis=1) * s1 + pltpu.roll(t, ROPE_DIM // 2, axis=1) * s2


def _nsa_in_kernel(x_ref, g_ref, w_ref, rot_ref,
                   q_ref, ks_ref, vst_ref, kw_ref, vwt_ref, kcv_ref, gate_ref, *, d_model, n_groups):
    tm = x_ref.shape[0]
    dh = NSA_HEAD_DIM
    half = ROPE_DIM // 2
    xn = _rmsnorm(x_ref[...], g_ref[...]).astype(BF16)
    y = _dot(xn, w_ref[...])
    r = rot_ref[...]
    lane = lax.broadcasted_iota(jnp.int32, r.shape, 1)
    d = jnp.bitwise_and(lane, dh - 1)
    first, second = d < half, (d >= half) & (d < ROPE_DIM)
    c = jnp.where(first, r, jnp.where(second, pltpu.roll(r, half, axis=1), 1.0))
    s1 = jnp.where(first, -pltpu.roll(r, LANES - half, axis=1), 0.0)
    s2 = jnp.where(second, r, 0.0)
    khalf = lane < dh
    ck, s1k, s2k = jnp.where(khalf, c, 1.0), jnp.where(khalf, s1, 0.0), jnp.where(khalf, s2, 0.0)
    scale = dh ** -0.5 * LOG2_E
    for j in range(d_model // LANES):
        sl = slice(j * LANES, (j + 1) * LANES)
        q_ref[:, sl] = (_rope(y[:, sl], c, s1, s2) * scale).astype(BF16)
    ones = jnp.ones((VT_ROWS - dh, LANES), F32)
    tok = pl.program_id(1) * tm + lax.broadcasted_iota(jnp.int32, c.shape, 0)
    blk_onehot = jnp.where(lane - dh == tok // SEL_BLOCK, 1.0, 0.0)
    off = d_model
    for k_ref, vt_ref, tag_blocks in ((ks_ref, vst_ref, True), (kw_ref, vwt_ref, False)):
        for g in range(n_groups):
            slab = _rope(y[:, off + g * LANES: off + (g + 1) * LANES], ck, s1k, s2k)
            k_ref[g] = (jnp.where(khalf, slab, blk_onehot) if tag_blocks else slab).astype(BF16)
            for ch in range(tm // LANES):
                vt = slab[ch * LANES:(ch + 1) * LANES, :].T[dh:, :]
                vt_ref[g, ch] = jnp.concatenate([vt, ones], axis=0).astype(BF16)
        off += n_groups * LANES
    for g in range(n_groups):
        kcv_ref[g] = _rope(y[:, off + g * LANES: off + (g + 1) * LANES], ck, s1k, s2k)
    off += n_groups * LANES
    gate_ref[...] = jax.nn.sigmoid(y[:, off:off + LANES])


def _nsa_in(x, g, w_perm, rot_tab, *, tm=1024):
    B, S, D = x.shape
    G = NSA_KV_GROUPS
    n_out = w_perm.shape[1]
    tok = lambda w: pl.BlockSpec((None, tm, w), lambda b, i: (b, i, 0))
    grp = pl.BlockSpec((None, G, tm, LANES), lambda b, i: (b, 0, i, 0))
    grp_t = pl.BlockSpec((None, G, tm // LANES, VT_ROWS, LANES), lambda b, i: (b, 0, i, 0, 0))
    k_shape = jax.ShapeDtypeStruct((B, G, S, LANES), BF16)
    vt_shape = jax.ShapeDtypeStruct((B, G, S // LANES, VT_ROWS, LANES), BF16)
    return pl.pallas_call(
        functools.partial(_nsa_in_kernel, d_model=D, n_groups=G),
        grid=(B, S // tm),
        in_specs=[tok(D), _const_spec((1, D)), _const_spec((D, n_out)), tok(LANES)],
        out_specs=[tok(D), grp, grp_t, grp, grp_t, grp, tok(LANES)],
        out_shape=[jax.ShapeDtypeStruct((B, S, D), BF16), k_shape, vt_shape, k_shape, vt_shape,
                   jax.ShapeDtypeStruct((B, G, S, LANES), F32),
                   jax.ShapeDtypeStruct((B, S, LANES), F32)],
        compiler_params=pltpu.CompilerParams(
            dimension_semantics=("parallel", "parallel"), vmem_limit_bytes=VMEM_LIMIT),
        name="nsa_in",
    )(x, g, w_perm, rot_tab)


def _nsa_cmp_kernel(x_ref, pa_ref, pb_ref, wa_ref, wb_ref, wk2_ref, wv2_ref, kc_ref, vct_ref, *, n_chunks):
    acc_a = jnp.zeros((n_chunks, wa_ref.shape[-1]), F32)
    acc_b = jnp.zeros((n_chunks, wb_ref.shape[-1]), F32)
    for lp in range(CMP_STRIDE // 2):
        pair = (2 * lp, 2 * lp + 1)
        xs = [x_ref[pl.ds(l, n_chunks, stride=CMP_STRIDE), :] for l in pair]
        xa = jnp.concatenate([(x + pa_ref[l:l + 1, :]).astype(BF16) for x, l in zip(xs, pair)], axis=1)
        xb = jnp.concatenate([(x + pb_ref[l:l + 1, :]).astype(BF16) for x, l in zip(xs, pair)], axis=1)
        acc_a = acc_a + _dot(xa, wa_ref[lp])
        acc_b = acc_b + _dot(xb, wb_ref[lp])
    hid = jax.nn.gelu(acc_a + pltpu.roll(acc_b, n_chunks - 1, axis=0)).astype(BF16)
    hw = hid.shape[1] // 2
    kv = _dot(hid[:, :hw], wk2_ref[...]) + _dot(hid[:, hw:], wv2_ref[...])
    kc_ref[...] = kv.astype(BF16)
    vct_ref[...] = kv.T[NSA_HEAD_DIM:, :].astype(BF16)


def _nsa_cmp(kcv, pa, pb, wa, wb, wk2x, wv2x):
    B, G, S, _ = kcv.shape
    NC = S // CMP_STRIDE
    return pl.pallas_call(
        functools.partial(_nsa_cmp_kernel, n_chunks=NC),
        grid=(B, G),
        in_specs=[pl.BlockSpec((None, None, S, LANES), lambda b, g: (b, g, 0, 0)),
                  _const_spec(pa.shape), _const_spec(pb.shape), _const_spec(wa.shape), _const_spec(wb.shape),
                  _const_spec(wk2x.shape), _const_spec(wv2x.shape)],
        out_specs=[pl.BlockSpec((None, None, NC, LANES), lambda b, g: (b, g, 0, 0)),
                   pl.BlockSpec((None, None, NSA_HEAD_DIM, NC), lambda b, g: (b, g, 0, 0))],
        out_shape=[jax.ShapeDtypeStruct((B, G, NC, LANES), BF16),
                   jax.ShapeDtypeStruct((B, G, NSA_HEAD_DIM, NC), BF16)],
        compiler_params=pltpu.CompilerParams(
            dimension_semantics=("parallel", "parallel"), vmem_limit_bytes=VMEM_LIMIT),
        name="nsa_cmp",
    )(kcv, pa, pb, wa, wb, wk2x, wv2x)


def _nsa_attn_kernel(q_ref, kc_ref, vct_ref, ks_ref, vst_ref, kw_ref, vwt_ref, gate_ref, ovt_ref, o_ref,
                     q4w, q4s, acc_s, m_s, acc_w, m_w, ocmp, imp_ref, rank_ref, gt, *, tq, n_sel, n_grp, unroll):
    P, dh = NSA_HPG, NSA_HEAD_DIM
    H = n_grp * P
    i = pl.program_id(2)
    qs = i * tq
    head_cols = [slice(h * tq, (h + 1) * tq) for h in range(H)]
    grp_of = [h // P for h in range(H)]

    qt = q_ref[...].astype(F32).T
    zpad = jnp.zeros((LANES - dh, tq), BF16)
    for h, cols in enumerate(head_cols):
        qh = qt[h * dh:(h + 1) * dh].astype(BF16)
        q4w[0:dh, cols] = qh
        q4w[dh:, cols] = zpad
        q4s[0:dh, cols] = qh

    t_q = qs + lax.broadcasted_iota(jnp.int32, (1, tq), 1)

    tk = tq
    n_pre = WINDOW // tk
    sub_i = lax.broadcasted_iota(jnp.int32, (SEL_BLOCK, tq), 0)

    def scores_of(k_ref, k0, q4):
        kts = [k_ref[n, pl.ds(k0, tk), :] for n in range(n_grp)]
        return [_dot(kts[grp_of[h]], q4[:, cols]) for h, cols in enumerate(head_cols)]

    def absorb(scores, vt_ref, k0, acc, m_ref, keep=None, shift=None, heads=None):
        heads = range(H) if heads is None else heads
        j0 = k0 // LANES
        vts = [jnp.concatenate([vt_ref[n, j0 + c] for c in range(tk // LANES)], axis=1) for n in range(n_grp)]
        n_slab = tk // SEL_BLOCK
        keeps = None if keep is None else [keep(k0 + b * SEL_BLOCK + sub_i) for b in range(n_slab)]
        for h in heads:
            cols = head_cols[h]
            slabs = []
            for b in range(n_slab):
                sb = scores[h][b * SEL_BLOCK:(b + 1) * SEL_BLOCK]
                if shift is not None:
                    sb = sb + shift
                if keeps is not None:
                    sb = jnp.where(keeps[b], sb, NEG_INF)
                slabs.append(sb)
            top = slabs[0]
            for sb in slabs[1:]:
                top = jnp.maximum(top, sb)
            m_old = m_ref[0:1, cols]
            m_new = jnp.maximum(m_old, jnp.max(top, axis=0, keepdims=True))
            alpha = jnp.exp2(m_old - m_new)
            pr = jnp.concatenate([jnp.exp2(sb - m_new).astype(BF16) for sb in slabs], axis=0)
            acc[:, cols] = acc[:, cols] * alpha + _dot(vts[grp_of[h]], pr)
            m_ref[0:1, cols] = m_new

    for m_ref, acc in ((m_s, acc_s), (m_w, acc_w)):
        m_ref[...] = jnp.full(m_ref.shape, NEG_INF, F32)
        acc[...] = jnp.zeros(acc.shape, F32)

    nc = kc_ref.shape[1]
    sc_all = [_dot(kc_ref[grp_of[h]], q4w[:, cols]) for h, cols in enumerate(head_cols)]
    win = []
    for d in range(n_pre, -1, -1):
        j = i - d
        k0 = pl.multiple_of(jnp.maximum(j, 0) * tk, tk)
        win.append((d, j, k0, scores_of(kw_ref, k0, q4w)))

    n_i = lax.broadcasted_iota(jnp.int32, (nc, tq), 0)
    cbias = jnp.where(n_i * CMP_STRIDE + (CMP_BLOCK - 1) <= t_q, 0.0, NEG_INF)
    has_c = t_q >= CMP_BLOCK - 1
    pcs = []
    for sc in sc_all:
        sc = sc + cbias
        ec = jnp.exp2(sc - jnp.max(sc, axis=0, keepdims=True))
        pcs.append(ec * jnp.where(has_c, 1.0 / jnp.sum(ec, axis=0, keepdims=True), 0.0))
    for h, cols in enumerate(head_cols):
        ocmp[:, cols] = _dot(vct_ref[grp_of[h]], pcs[h].astype(BF16))

    ovt = ovt_ref[...]
    m_i = lax.broadcasted_iota(jnp.int32, (n_sel, tq), 0)
    cur = jnp.right_shift(t_q, SEL_BLOCK.bit_length() - 1)
    forced = (m_i == 0) | (m_i == cur) | (m_i == cur - 1)
    in_past = m_i * SEL_BLOCK <= t_q
    sub8 = lax.broadcasted_iota(jnp.int32, (8, tq), 0)
    for n in range(n_grp):
        psum = (pcs[n * P] + pcs[n * P + 1]) + (pcs[n * P + 2] + pcs[n * P + 3])
        hi = psum.astype(BF16)
        r1 = psum - hi.astype(F32)
        mid = r1.astype(BF16)
        lo = (r1 - mid.astype(F32)).astype(BF16)
        imp = (_dot(ovt, hi) + _dot(ovt, mid) + _dot(ovt, lo))[0:n_sel]
        imp = jnp.where(forced, FORCE_SCORE, imp)
        imp_ref[n] = jnp.where(in_past, imp, NEG_INF)
        rank_ref[n] = jnp.zeros((n_sel, tq), F32)

    causal = lambda kpos: kpos <= t_q
    for d, j, k0, s_win in win:
        if d == n_pre:
            lowest = t_q - WINDOW + jnp.where(j >= 0, 0, NO_KEY)
            absorb(s_win, vwt_ref, k0, acc_w, m_w, keep=lambda kpos: kpos > lowest)
        elif d == 0:
            absorb(s_win, vwt_ref, k0, acc_w, m_w, keep=causal)
        else:
            absorb(s_win, vwt_ref, k0, acc_w, m_w, shift=jnp.where(j >= 0, 0.0, NEG_INF))

    n_rg = n_sel // 8
    last_rg = ((qs + tq - 1) // SEL_BLOCK) // 8
    for rp in range(n_rg):
        @pl.when(rp <= last_rg)
        def _(rp=rp):
            for n in range(n_grp):
                groups = [imp_ref[n, 8 * r:8 * r + 8, :] for r in range(n_rg)]
                ranks = [rank_ref[n, 8 * r:8 * r + 8, :] for r in range(n_rg)]
                for mp in range(8 * rp, 8 * rp + 8):
                    row = imp_ref[n, mp:mp + 1, :]
                    for r, vr in enumerate(groups):
                        if 8 * r > mp:
                            beats = row >= vr
                        elif 8 * r + 7 < mp:
                            beats = row > vr
                        else:
                            beats = (row > vr) | ((row == vr) & (sub8 + 8 * r > mp))
                        ranks[r] = ranks[r] + jnp.where(beats, 1.0, 0.0)
                for r in range(n_rg):
                    rank_ref[n, 8 * r:8 * r + 8, :] = ranks[r]

    for n in range(n_grp):
        sel_bias = jnp.where(rank_ref[n] < float(min(SEL_TOPN, n_sel)), 0.0, NEG_INF)
        if n_sel < LANES - dh:
            sel_bias = jnp.concatenate([sel_bias, jnp.zeros((LANES - dh - n_sel, tq), F32)], axis=0)
        for cols in head_cols[n * P:(n + 1) * P]:
            q4s[dh:, cols] = sel_bias.astype(BF16)

    def sel_tiles(k0s):
        sc = [scores_of(ks_ref, k0, q4s) for k0 in k0s]
        for h in range(H):
            for k0, s in zip(k0s, sc):
                absorb(s, vst_ref, k0, acc_s, m_s, heads=(h,))

    def main_body(a, c):
        sel_tiles([pl.multiple_of((a * unroll + u) * tk, tk) for u in range(unroll)])
        return c

    lax.fori_loop(0, i // unroll, main_body, 0)
    bit = unroll // 2
    while bit >= 1:
        first = i - jnp.bitwise_and(i, 2 * bit - 1)

        @pl.when(jnp.bitwise_and(i, bit) != 0)
        def _(first=first, bit=bit):
            sel_tiles([pl.multiple_of((first + u) * tk, tk) for u in range(bit)])

        bit //= 2

    k_diag = pl.multiple_of(qs, tk)
    s_diag = scores_of(ks_ref, k_diag, q4s)
    o_w = acc_w[0:dh, :] * (1.0 / acc_w[dh:dh + 1, :])
    gt[...] = gate_ref[...].T
    absorb(s_diag, vst_ref, k_diag, acc_s, m_s, keep=causal)
    o_s = acc_s[0:dh, :] * (1.0 / acc_s[dh:dh + 1, :])

    h0 = H * pl.program_id(1)
    outs = []
    for h, cols in enumerate(head_cols):
        outs.append(gt[pl.ds(h0 + h, 1), :] * ocmp[:, cols]
                    + gt[pl.ds(NSA_HEADS + h0 + h, 1), :] * o_s[:, cols]
                    + gt[pl.ds(2 * NSA_HEADS + h0 + h, 1), :] * o_w[:, cols])
    o_ref[...] = jnp.concatenate(outs, axis=0).T.astype(BF16)


def _nsa_attn(q, kc, vct, ks, vst, kw, vwt, gates, ovt, *, tq=256, n_grp=4, unroll=4):
    B, S, D = q.shape
    G, P, dh = NSA_KV_GROUPS, NSA_HPG, NSA_HEAD_DIM
    NC = kc.shape[2]
    n_sel = S // SEL_BLOCK
    assert tq % LANES == 0 and tq & (tq - 1) == 0 and WINDOW % tq == 0 and G % n_grp == 0
    assert n_sel % 16 == 0 and n_sel <= LANES - dh and SEL_BLOCK & (SEL_BLOCK - 1) == 0
    L = n_grp * P * tq
    qblk = pl.BlockSpec((None, tq, n_grp * P * dh), lambda b, g, i: (b, i, g))
    grp = lambda *tail: pl.BlockSpec((None, n_grp) + tail, lambda b, g, i: (b, g) + (0,) * len(tail),
                                     pipeline_mode=pl.Buffered(1))
    seq, seq_t = grp(S, LANES), grp(S // LANES, VT_ROWS, LANES)
    return pl.pallas_call(
        functools.partial(_nsa_attn_kernel, tq=tq, n_sel=n_sel, n_grp=n_grp, unroll=unroll),
        grid=(B, G // n_grp, S // tq),
        in_specs=[qblk, grp(NC, LANES), grp(dh, NC), seq, seq_t, seq, seq_t,
                  pl.BlockSpec((None, tq, LANES), lambda b, g, i: (b, i, 0)),
                  _const_spec(ovt.shape)],
        out_specs=qblk,
        out_shape=jax.ShapeDtypeStruct((B, S, D), BF16),
        scratch_shapes=[pltpu.VMEM((LANES, L), BF16),
                        pltpu.VMEM((LANES, L), BF16),
                        pltpu.VMEM((VT_ROWS, L), F32),
                        pltpu.VMEM((8, L), F32),
                        pltpu.VMEM((VT_ROWS, L), F32),
                        pltpu.VMEM((8, L), F32),
                        pltpu.VMEM((dh, L), F32),
                        pltpu.VMEM((n_grp, n_sel, tq), F32),
                        pltpu.VMEM((n_grp, n_sel, tq), F32),
                        pltpu.VMEM((LANES, tq), F32)],
        compiler_params=pltpu.CompilerParams(
            dimension_semantics=("parallel", "parallel", "arbitrary"), vmem_limit_bytes=VMEM_LIMIT),
        name="nsa_attn",
    )(q, kc, vct, ks, vst, kw, vwt, gates, ovt)


def _sgu_kernel(x_ref, g_ref, w_ref, lng_ref, lnb_ref, ws_ref, bs_ref, o_ref, *, width):
    T = SGU_CHUNK
    tm = x_ref.shape[0]
    xn = _rmsnorm(x_ref[...], g_ref[...]).astype(BF16)
    y = jax.nn.gelu(_dot(xn, w_ref[...]))
    u, v = y[:, :width], y[:, width:]
    mu = jnp.mean(v, axis=-1, keepdims=True)
    vc = v - mu
    var = jnp.mean(vc * vc, axis=-1, keepdims=True)
    v = (vc * lax.rsqrt(var + LN_EPS) * lng_ref[...] + lnb_ref[...]).astype(BF16)
    r_i = lax.broadcasted_iota(jnp.int32, (T, T), 0)
    c_i = lax.broadcasted_iota(jnp.int32, (T, T), 1)
    gd = width // SGU_GROUPS
    in_a = lax.broadcasted_iota(jnp.int32, (T, 2 * gd), 1) < gd
    zero = jnp.zeros((T, 2 * gd), BF16)
    for gp in range(SGU_GROUPS // 2):
        ws = jnp.concatenate([jnp.where(r_i >= c_i, ws_ref[2 * gp + k], 0.0) for k in range(2)],
                             axis=1).astype(BF16)
        bias = jnp.concatenate([bs_ref[2 * gp], bs_ref[2 * gp + 1]], axis=1)
        cols = slice(2 * gp * gd, (2 * gp + 2) * gd)
        for c in range(tm // T):
            rws = slice(c * T, (c + 1) * T)
            vv = v[rws, cols]
            diag = jnp.concatenate([jnp.where(in_a, vv, zero), jnp.where(in_a, zero, vv)], axis=0)
            sv = _dot(ws, diag) + bias
            o_ref[rws, cols] = (u[rws, cols] * sv).astype(BF16)


def _sgu(x2, g, w_in, ln_g, ln_b, w_s, bs_b, *, tm=1024):
    N, D = x2.shape
    W = w_in.shape[1] // 2
    tok = lambda w: pl.BlockSpec((tm, w), lambda i: (i, 0))
    return pl.pallas_call(
        functools.partial(_sgu_kernel, width=W),
        grid=(N // tm,),
        in_specs=[tok(D), _const_spec((1, D)), _const_spec(w_in.shape), _const_spec((1, W)), _const_spec((1, W)),
                  _const_spec(w_s.shape), _const_spec(bs_b.shape)],
        out_specs=tok(W),
        out_shape=jax.ShapeDtypeStruct((N, W), BF16),
        compiler_params=pltpu.CompilerParams(dimension_semantics=("parallel",), vmem_limit_bytes=VMEM_LIMIT),
        name="sgu",
    )(x2, g, w_in, ln_g, ln_b, w_s, bs_b)


def _pool_stage_start(lo, span):
    return -(-(lo + span) // 8) * 8


def _pool_kernel(x_ref, g_ref, w_ref, wg_ref, sc_ref, o_ref, zbuf, sum_a, sum_b, *, hist):
    tm = x_ref.shape[0]
    i = pl.program_id(1)
    rows = hist + tm

    @pl.when(i == 0)
    def _():
        zbuf[0:hist, :] = jnp.zeros((hist, zbuf.shape[1]), F32)

    xn = _rmsnorm(x_ref[...], g_ref[...]).astype(BF16)
    z = _dot(xn, w_ref[...])
    zbuf[hist:rows, :] = z
    t = i * tm + lax.broadcasted_iota(jnp.int32, (tm, 1), 0)
    gd = z.shape[1] // len(POOL_WINDOWS)
    src, span, lo = zbuf, 1, 0
    for s, w in enumerate(POOL_WINDOWS):
        assert w == 2 * span
        lo = _pool_stage_start(lo, span)
        dst = sum_a if s % 2 == 0 else sum_b
        c0 = s * gd
        dst[lo:rows, c0:] = src[lo:rows, c0:] + src[lo - span:rows - span, c0:]
        src, span = dst, w
    assert lo <= hist
    for gi, w in enumerate(POOL_WINDOWS):
        cols = slice(gi * gd, (gi + 1) * gd)
        win = (sum_a if gi % 2 == 0 else sum_b)[hist:rows, cols]
        cnt = jnp.minimum(t + 1, w).astype(F32)
        d = win / cnt - z[:, cols]
        y = _dot(d.astype(BF16), wg_ref[gi]) * sc_ref[:, cols]
        o_ref[:, cols] = y.astype(BF16)
    zbuf[0:hist, :] = zbuf[tm:rows, :]


def _pool(x, g, w_in, w_grp, scale, *, tm=1024):
    B, S, D = x.shape
    PW = w_in.shape[1]
    hist, span = 0, 1
    for w in POOL_WINDOWS:
        hist, span = _pool_stage_start(hist, span), w
    assert tm >= hist
    tok = lambda w: pl.BlockSpec((None, tm, w), lambda b, i: (b, i, 0))
    return pl.pallas_call(
        functools.partial(_pool_kernel, hist=hist),
        grid=(B, S // tm),
        in_specs=[tok(D), _const_spec((1, D)), _const_spec(w_in.shape), _const_spec(w_grp.shape),
                  _const_spec((1, PW))],
        out_specs=tok(PW),
        out_shape=jax.ShapeDtypeStruct((B, S, PW), BF16),
        scratch_shapes=[pltpu.VMEM((hist + tm, PW), F32)] * 3,
        compiler_params=pltpu.CompilerParams(
            dimension_semantics=("arbitrary", "arbitrary"), vmem_limit_bytes=VMEM_LIMIT),
        name="pool",
    )(x, g, w_in, w_grp, scale)


def _post_kernel(x_ref, u_ref, wo_ref, g_ref, wgu_ref, wd_ref, fg_ref, o_ref, *, hidden, final):
    x1 = x_ref[...] + _dot(u_ref[...], wo_ref[...])
    xn = _rmsnorm(x1, g_ref[...]).astype(BF16)
    gate = _dot(xn, wgu_ref[:, :hidden])
    up = _dot(xn, wgu_ref[:, hidden:])
    a = (jax.nn.silu(gate) * up).astype(BF16)
    x2 = x1 + _dot(a, wd_ref[...])
    if final:
        x2 = _rmsnorm(x2, fg_ref[...])
    o_ref[...] = x2


def _post(x2, u2, w_out, g, w_gu_all, w_down_all, fg, *, layer, final, tm=512):
    N, D = x2.shape
    hidden = w_down_all.shape[1]
    tok = lambda w: pl.BlockSpec((tm, w), lambda i: (i, 0))
    of_layer = lambda a: pl.BlockSpec((None,) + a.shape[1:], lambda i: (layer, 0, 0), pipeline_mode=pl.Buffered(1))
    return pl.pallas_call(
        functools.partial(_post_kernel, hidden=hidden, final=final),
        grid=(N // tm,),
        in_specs=[tok(D), tok(u2.shape[1]), _const_spec(w_out.shape), _const_spec((1, D)),
                  of_layer(w_gu_all), of_layer(w_down_all), _const_spec((1, D))],
        out_specs=tok(D),
        out_shape=jax.ShapeDtypeStruct((N, D), F32),
        compiler_params=pltpu.CompilerParams(dimension_semantics=("parallel",), vmem_limit_bytes=VMEM_LIMIT),
        name="post",
    )(x2, u2, w_out, g, w_gu_all, w_down_all, fg)


def _rope_table(positions):
    inv_freq = ROPE_THETA ** (-jnp.arange(0, ROPE_DIM, 2, dtype=F32) / ROPE_DIM)
    ang = positions.astype(F32)[..., None] * inv_freq
    pad = jnp.zeros(ang.shape[:-1] + (NSA_HEAD_DIM - ROPE_DIM,), F32)
    return jnp.tile(jnp.concatenate([jnp.cos(ang), jnp.sin(ang), pad], axis=-1), (1, 1, LANES // NSA_HEAD_DIM))


def _permute_nsa_w_in(w):
    H, G, dh = NSA_HEADS, NSA_KV_GROUPS, NSA_HEAD_DIM
    base = H * dh
    seg = lambda k: w[:, base + k * G * dh: base + (k + 1) * G * dh]
    kc, vc, ks, vs, kw, vw = (seg(k) for k in range(6))
    gl = w[:, base + 6 * G * dh:]
    inter = lambda a, b: jnp.concatenate(
        [a.reshape(-1, G, 1, dh), b.reshape(-1, G, 1, dh)], axis=2).reshape(-1, 2 * G * dh)
    glp = jnp.pad(gl, ((0, 0), (0, LANES - gl.shape[1])))
    return jnp.concatenate([w[:, :base], inter(ks, vs), inter(kw, vw), inter(kc, vc), glp], axis=1).astype(BF16)


def _overlap_matrix_t(n_chunks, n_sel):
    n = np.arange(n_chunks)[None, :]
    m = np.arange(LANES)[:, None]
    start, end = n * CMP_STRIDE, n * CMP_STRIDE + CMP_BLOCK - 1
    ov = (start <= m * SEL_BLOCK + SEL_BLOCK - 1) & (end >= m * SEL_BLOCK) & (m < n_sel)
    return jnp.asarray(ov, BF16)


def _cmp_weights(k_w1, v_w1):
    dh = NSA_HEAD_DIM
    kw = k_w1.reshape(CMP_BLOCK, dh, -1)
    vw = v_w1.reshape(CMP_BLOCK, dh, -1)
    z = jnp.zeros_like(kw)
    w = jnp.concatenate([jnp.concatenate([kw, z], axis=2), jnp.concatenate([z, vw], axis=2)], axis=1).astype(BF16)
    w = w.reshape(CMP_BLOCK // 2, 2 * w.shape[1], w.shape[2])
    return w[:CMP_STRIDE // 2], w[CMP_STRIDE // 2:]


def _nsa_layer(x, rot_tab, g, w_in, pos_k, k_w1, k_w2, pos_v, v_w1, v_w2):
    B, S, D = x.shape
    dh = NSA_HEAD_DIM
    assert CMP_BLOCK == 2 * CMP_STRIDE
    q, ks, vst, kw, vwt, kcv, gates = _nsa_in(x, g.reshape(1, D), _permute_nsa_w_in(w_in), rot_tab)
    pos = jnp.concatenate([pos_k, pos_v], axis=1)
    wa, wb = _cmp_weights(k_w1, v_w1)
    pad_r = lambda w2: jnp.pad(w2, ((0, 0), (0, LANES - dh))).astype(BF16)
    pad_l = lambda w2: jnp.pad(w2, ((0, 0), (LANES - dh, 0))).astype(BF16)
    kc, vct = _nsa_cmp(kcv, pos[:CMP_STRIDE], pos[CMP_STRIDE:], wa, wb, pad_r(k_w2), pad_l(v_w2))
    ovt = _overlap_matrix_t(S // CMP_STRIDE, S // SEL_BLOCK)
    return _nsa_attn(q, kc, vct, ks, vst, kw, vwt, gates, ovt)


def kernel(x, positions, mix_norm, ffn_norm, final_norm, nsa_w_in, nsa_cmp_pos_k, nsa_cmp_k_w1, nsa_cmp_k_w2, nsa_cmp_pos_v, nsa_cmp_v_w1, nsa_cmp_v_w2, nsa_w_out, sgu_w_in, sgu_ln_g, sgu_ln_b, sgu_w_s, sgu_b_s, sgu_w_out, pool_w_in, pool_w_grp, pool_scale, pool_w_out, ffn_w_gate_up, ffn_w_down):
    B, S, D = x.shape
    depth = mix_norm.shape[0]
    rot_tab = _rope_table(positions)
    row = lambda v: v.reshape(1, -1)
    w_gu_all, w_down_all = ffn_w_gate_up.astype(BF16), ffn_w_down.astype(BF16)
    for i in range(depth):
        kind, j = i % N_MIXERS, i // N_MIXERS
        if kind == 0:
            u = _nsa_layer(x, rot_tab, mix_norm[i], nsa_w_in[j], nsa_cmp_pos_k[j], nsa_cmp_k_w1[j], nsa_cmp_k_w2[j],
                           nsa_cmp_pos_v[j], nsa_cmp_v_w1[j], nsa_cmp_v_w2[j])
            w_out = nsa_w_out[j]
        elif kind == 1:
            gd = sgu_w_in.shape[-1] // 2 // SGU_GROUPS
            bs_b = jnp.broadcast_to(sgu_b_s[j][:, :, None], sgu_b_s[j].shape + (gd,))
            u = _sgu(x.reshape(B * S, D), row(mix_norm[i]), sgu_w_in[j].astype(BF16), row(sgu_ln_g[j]),
                     row(sgu_ln_b[j]), sgu_w_s[j], bs_b)
            w_out = sgu_w_out[j]
        else:
            u = _pool(x, row(mix_norm[i]), pool_w_in[j].astype(BF16), pool_w_grp[j].astype(BF16),
                      row(pool_scale[j]))
            w_out = pool_w_out[j]
        x = _post(x.reshape(B * S, D), u.reshape(B * S, -1), w_out.astype(BF16), row(ffn_norm[i]),
                  w_gu_all, w_down_all, row(final_norm), layer=i, final=(i == depth - 1)).reshape(B, S, D)
    return x
```

```python
import functools

import jax
import jax.numpy as jnp
import numpy as np
from jax import lax
from jax.experimental import pallas as pl
from jax.experimental.pallas import tpu as pltpu

RMS_EPS = 1e-6
LN_EPS = 1e-5
NEG_INF = -1e30
FORCE_SCORE = 1e9
LOG2_E = 1.4426950408889634
NO_KEY = 1 << 30

N_MIXERS = 3
NSA_HEADS = 16
NSA_KV_GROUPS = 4
NSA_HPG = NSA_HEADS // NSA_KV_GROUPS
NSA_HEAD_DIM = 64
CMP_BLOCK = 32
CMP_STRIDE = 16
SEL_BLOCK = 64
SEL_TOPN = 16
WINDOW = 512
ROPE_THETA = 500000.0
ROPE_DIM = NSA_HEAD_DIM // 4
SGU_GROUPS = 8
SGU_CHUNK = 128
POOL_WINDOWS = (2, 4, 8, 16)

LANES = 128
VMEM_LIMIT = 56 * 1024 * 1024

BF16 = jnp.bfloat16
F32 = jnp.float32


def _const_spec(shape):
    nd = len(shape)
    return pl.BlockSpec(shape, lambda *_: (0,) * nd, pipeline_mode=pl.Buffered(1))


def _rmsnorm(x, g):
    ms = jnp.mean(x * x, axis=-1, keepdims=True)
    return x * lax.rsqrt(ms + RMS_EPS) * g


def _dot(a, b):
    return jnp.dot(a, b, preferred_element_type=F32)


VT_ROWS = NSA_HEAD_DIM + 16


def _rope(t, c, s1, s2):
    return t * c + pltpu.roll(t, LANES - ROPE_DIM // 2, axis=1) * s1 + pltpu.roll(t, ROPE_DIM // 2, axis=1) * s2


def _nsa_in_kernel(x_ref, g_ref, w_ref, rot_ref,
                   q_ref, ks_ref, vst_ref, kw_ref, vwt_ref, kcv_ref, gate_ref, *, d_model, n_groups):
    tm = x_ref.shape[0]
    dh = NSA_HEAD_DIM
    half = ROPE_DIM // 2
    xn = _rmsnorm(x_ref[...], g_ref[...]).astype(BF16)
    y = _dot(xn, w_ref[...])
    r = rot_ref[...]
    lane = lax.broadcasted_iota(jnp.int32, r.shape, 1)
    d = jnp.bitwise_and(lane, dh - 1)
    first, second = d < half, (d >= half) & (d < ROPE_DIM)
    c = jnp.where(first, r, jnp.where(second, pltpu.roll(r, half, axis=1), 1.0))
    s1 = jnp.where(first, -pltpu.roll(r, LANES - half, axis=1), 0.0)
    s2 = jnp.where(second, r, 0.0)
    khalf = lane < dh
    ck, s1k, s2k = jnp.where(khalf, c, 1.0), jnp.where(khalf, s1, 0.0), jnp.where(khalf, s2, 0.0)
    scale = dh ** -0.5 * LOG2_E
    for j in range(d_model // LANES):
        sl = slice(j * LANES, (j + 1) * LANES)
        q_ref[:, sl] = (_rope(y[:, sl], c, s1, s2) * scale).astype(BF16)
    ones = jnp.ones((VT_ROWS - dh, LANES), F32)
    tok = pl.program_id(1) * tm + lax.broadcasted_iota(jnp.int32, c.shape, 0)
    blk_onehot = jnp.where(lane - dh == tok // SEL_BLOCK, 1.0, 0.0)
    off = d_model
    for k_ref, vt_ref, tag_blocks in ((ks_ref, vst_ref, True), (kw_ref, vwt_ref, False)):
        for g in range(n_groups):
            slab = _rope(y[:, off + g * LANES: off + (g + 1) * LANES], ck, s1k, s2k)
            k_ref[g] = (jnp.where(khalf, slab, blk_onehot) if tag_blocks else slab).astype(BF16)
            for ch in range(tm // LANES):
                vt = slab[ch * LANES:(ch + 1) * LANES, :].T[dh:, :]
                vt_ref[g, ch] = jnp.concatenate([vt, ones], axis=0).astype(BF16)
        off += n_groups * LANES
    for g in range(n_groups):
        kcv_ref[g] = _rope(y[:, off + g * LANES: off + (g + 1) * LANES], ck, s1k, s2k)
    off += n_groups * LANES
    gate_ref[...] = jax.nn.sigmoid(y[:, off:off + LANES])


def _nsa_in(x, g, w_perm, rot_tab, *, tm=1024):
    B, S, D = x.shape
    G = NSA_KV_GROUPS
    n_out = w_perm.shape[1]
    tok = lambda w: pl.BlockSpec((None, tm, w), lambda b, i: (b, i, 0))
    grp = pl.BlockSpec((None, G, tm, LANES), lambda b, i: (b, 0, i, 0))
    grp_t = pl.BlockSpec((None, G, tm // LANES, VT_ROWS, LANES), lambda b, i: (b, 0, i, 0, 0))
    k_shape = jax.ShapeDtypeStruct((B, G, S, LANES), BF16)
    vt_shape = jax.ShapeDtypeStruct((B, G, S // LANES, VT_ROWS, LANES), BF16)
    return pl.pallas_call(
        functools.partial(_nsa_in_kernel, d_model=D, n_groups=G),
        grid=(B, S // tm),
        in_specs=[tok(D), _const_spec((1, D)), _const_spec((D, n_out)), tok(LANES)],
        out_specs=[tok(D), grp, grp_t, grp, grp_t, grp, tok(LANES)],
        out_shape=[jax.ShapeDtypeStruct((B, S, D), BF16), k_shape, vt_shape, k_shape, vt_shape,
                   jax.ShapeDtypeStruct((B, G, S, LANES), F32),
                   jax.ShapeDtypeStruct((B, S, LANES), F32)],
        compiler_params=pltpu.CompilerParams(
            dimension_semantics=("parallel", "parallel"), vmem_limit_bytes=VMEM_LIMIT),
        name="nsa_in",
    )(x, g, w_perm, rot_tab)


def _nsa_cmp_kernel(x_ref, pa_ref, pb_ref, wa_ref, wb_ref, wk2_ref, wv2_ref, kc_ref, vct_ref, *, n_chunks):
    n_grp = x_ref.shape[0]
    rows = n_grp * n_chunks
    acc_a = jnp.zeros((rows, wa_ref.shape[-1]), F32)
    acc_b = jnp.zeros((rows, wb_ref.shape[-1]), F32)
    for lp in range(CMP_STRIDE // 2):
        pair = (2 * lp, 2 * lp + 1)
        xs = [jnp.concatenate([x_ref[g, pl.ds(l, n_chunks, stride=CMP_STRIDE), :] for g in range(n_grp)], axis=0)
              for l in pair]
        xa = jnp.concatenate([(x + pa_ref[l:l + 1, :]).astype(BF16) for x, l in zip(xs, pair)], axis=1)
        xb = jnp.concatenate([(x + pb_ref[l:l + 1, :]).astype(BF16) for x, l in zip(xs, pair)], axis=1)
        acc_a = acc_a + _dot(xa, wa_ref[lp])
        acc_b = acc_b + _dot(xb, wb_ref[lp])
    hid = jax.nn.gelu(acc_a + pltpu.roll(acc_b, rows - 1, axis=0)).astype(BF16)
    hw = hid.shape[1] // 2
    kv = _dot(hid[:, :hw], wk2_ref[...]) + _dot(hid[:, hw:], wv2_ref[...])
    for g in range(n_grp):
        kv_g = kv[g * n_chunks:(g + 1) * n_chunks]
        kc_ref[g] = kv_g.astype(BF16)
        vct_ref[g] = kv_g.T[NSA_HEAD_DIM:, :].astype(BF16)


def _nsa_cmp(kcv, pa, pb, wa, wb, wk2x, wv2x):
    B, G, S, _ = kcv.shape
    NC = S // CMP_STRIDE
    return pl.pallas_call(
        functools.partial(_nsa_cmp_kernel, n_chunks=NC),
        grid=(B,),
        in_specs=[pl.BlockSpec((None, G, S, LANES), lambda b: (b, 0, 0, 0)),
                  _const_spec(pa.shape), _const_spec(pb.shape), _const_spec(wa.shape), _const_spec(wb.shape),
                  _const_spec(wk2x.shape), _const_spec(wv2x.shape)],
        out_specs=[pl.BlockSpec((None, G, NC, LANES), lambda b: (b, 0, 0, 0)),
                   pl.BlockSpec((None, G, NSA_HEAD_DIM, NC), lambda b: (b, 0, 0, 0))],
        out_shape=[jax.ShapeDtypeStruct((B, G, NC, LANES), BF16),
                   jax.ShapeDtypeStruct((B, G, NSA_HEAD_DIM, NC), BF16)],
        compiler_params=pltpu.CompilerParams(dimension_semantics=("parallel",), vmem_limit_bytes=VMEM_LIMIT),
        name="nsa_cmp",
    )(kcv, pa, pb, wa, wb, wk2x, wv2x)


def _nsa_attn_kernel(q_ref, kc_ref, vct_ref, ks_ref, vst_ref, kw_ref, vwt_ref, gate_ref, ovt_ref, o_ref,
                     q4w, q4s, acc_s, m_s, acc_w, m_w, ocmp, imp_ref, rank_ref, gt, *, tq, n_sel, n_grp, unroll):
    P, dh = NSA_HPG, NSA_HEAD_DIM
    H = n_grp * P
    i = pl.program_id(2)
    qs = i * tq
    head_cols = [slice(h * tq, (h + 1) * tq) for h in range(H)]
    grp_of = [h // P for h in range(H)]

    qt = q_ref[...].astype(F32).T
    zpad = jnp.zeros((LANES - dh, tq), BF16)
    for h, cols in enumerate(head_cols):
        qh = qt[h * dh:(h + 1) * dh].astype(BF16)
        q4w[0:dh, cols] = qh
        q4w[dh:, cols] = zpad
        q4s[0:dh, cols] = qh

    t_q = qs + lax.broadcasted_iota(jnp.int32, (1, tq), 1)

    tk = tq
    n_pre = WINDOW // tk
    sub_i = lax.broadcasted_iota(jnp.int32, (SEL_BLOCK, tq), 0)

    def scores_of(k_ref, k0, q4):
        kts = [k_ref[n, pl.ds(k0, tk), :] for n in range(n_grp)]
        return [_dot(kts[grp_of[h]], q4[:, cols]) for h, cols in enumerate(head_cols)]

    def absorb(scores, vt_ref, k0, acc, m_ref, keep=None, shift=None, heads=None):
        heads = range(H) if heads is None else heads
        j0 = k0 // LANES
        vts = [jnp.concatenate([vt_ref[n, j0 + c] for c in range(tk // LANES)], axis=1) for n in range(n_grp)]
        n_slab = tk // SEL_BLOCK
        keeps = None if keep is None else [keep(k0 + b * SEL_BLOCK + sub_i) for b in range(n_slab)]
        for h in heads:
            cols = head_cols[h]
            slabs = []
            for b in range(n_slab):
                sb = scores[h][b * SEL_BLOCK:(b + 1) * SEL_BLOCK]
                if shift is not None:
                    sb = sb + shift
                if keeps is not None:
                    sb = jnp.where(keeps[b], sb, NEG_INF)
                slabs.append(sb)
            top = slabs[0]
            for sb in slabs[1:]:
                top = jnp.maximum(top, sb)
            m_old = m_ref[0:1, cols]
            m_new = jnp.maximum(m_old, jnp.max(top, axis=0, keepdims=True))
            alpha = jnp.exp2(m_old - m_new)
            pr = jnp.concatenate([jnp.exp2(sb - m_new).astype(BF16) for sb in slabs], axis=0)
            acc[:, cols] = acc[:, cols] * alpha + _dot(vts[grp_of[h]], pr)
            m_ref[0:1, cols] = m_new

    for m_ref, acc in ((m_s, acc_s), (m_w, acc_w)):
        m_ref[...] = jnp.full(m_ref.shape, NEG_INF, F32)
        acc[...] = jnp.zeros(acc.shape, F32)

    nc = kc_ref.shape[1]
    sc_all = [_dot(kc_ref[grp_of[h]], q4w[:, cols]) for h, cols in enumerate(head_cols)]
    win = []
    for d in range(n_pre, -1, -1):
        j = i - d
        k0 = pl.multiple_of(jnp.maximum(j, 0) * tk, tk)
        win.append((d, j, k0, scores_of(kw_ref, k0, q4w)))

    n_i = lax.broadcasted_iota(jnp.int32, (nc, tq), 0)
    cbias = jnp.where(n_i * CMP_STRIDE + (CMP_BLOCK - 1) <= t_q, 0.0, NEG_INF)
    has_c = t_q >= CMP_BLOCK - 1
    pcs = []
    for sc in sc_all:
        sc = sc + cbias
        ec = jnp.exp2(sc - jnp.max(sc, axis=0, keepdims=True))
        pcs.append(ec * jnp.where(has_c, 1.0 / jnp.sum(ec, axis=0, keepdims=True), 0.0))
    for h, cols in enumerate(head_cols):
        ocmp[:, cols] = _dot(vct_ref[grp_of[h]], pcs[h].astype(BF16))

    ovt = ovt_ref[...]
    m_i = lax.broadcasted_iota(jnp.int32, (n_sel, tq), 0)
    cur = jnp.right_shift(t_q, SEL_BLOCK.bit_length() - 1)
    forced = (m_i == 0) | (m_i == cur) | (m_i == cur - 1)
    in_past = m_i * SEL_BLOCK <= t_q
    sub8 = lax.broadcasted_iota(jnp.int32, (8, tq), 0)
    for n in range(n_grp):
        psum = (pcs[n * P] + pcs[n * P + 1]) + (pcs[n * P + 2] + pcs[n * P + 3])
        hi = psum.astype(BF16)
        r1 = psum - hi.astype(F32)
        mid = r1.astype(BF16)
        lo = (r1 - mid.astype(F32)).astype(BF16)
        imp = (_dot(ovt, hi) + _dot(ovt, mid) + _dot(ovt, lo))[0:n_sel]
        imp = jnp.where(forced, FORCE_SCORE, imp)
        imp_ref[n] = jnp.where(in_past, imp, NEG_INF)
        rank_ref[n] = jnp.zeros((n_sel, tq), F32)

    causal = lambda kpos: kpos <= t_q
    for d, j, k0, s_win in win:
        if d == n_pre:
            lowest = t_q - WINDOW + jnp.where(j >= 0, 0, NO_KEY)
            absorb(s_win, vwt_ref, k0, acc_w, m_w, keep=lambda kpos: kpos > lowest)
        elif d == 0:
            absorb(s_win, vwt_ref, k0, acc_w, m_w, keep=causal)
        else:
            absorb(s_win, vwt_ref, k0, acc_w, m_w, shift=jnp.where(j >= 0, 0.0, NEG_INF))

    n_rg = n_sel // 8
    last_rg = ((qs + tq - 1) // SEL_BLOCK) // 8
    for rp in range(n_rg):
        @pl.when(rp <= last_rg)
        def _(rp=rp):
            for n in range(n_grp):
                groups = [imp_ref[n, 8 * r:8 * r + 8, :] for r in range(n_rg)]
                ranks = [rank_ref[n, 8 * r:8 * r + 8, :] for r in range(n_rg)]
                for mp in range(8 * rp, 8 * rp + 8):
                    row = imp_ref[n, mp:mp + 1, :]
                    for r, vr in enumerate(groups):
                        if 8 * r > mp:
                            beats = row >= vr
                        elif 8 * r + 7 < mp:
                            beats = row > vr
                        else:
                            beats = (row > vr) | ((row == vr) & (sub8 + 8 * r > mp))
                        ranks[r] = ranks[r] + jnp.where(beats, 1.0, 0.0)
                for r in range(n_rg):
                    rank_ref[n, 8 * r:8 * r + 8, :] = ranks[r]

    for n in range(n_grp):
        sel_bias = jnp.where(rank_ref[n] < float(min(SEL_TOPN, n_sel)), 0.0, NEG_INF)
        if n_sel < LANES - dh:
            sel_bias = jnp.concatenate([sel_bias, jnp.zeros((LANES - dh - n_sel, tq), F32)], axis=0)
        for cols in head_cols[n * P:(n + 1) * P]:
            q4s[dh:, cols] = sel_bias.astype(BF16)

    def sel_tiles(k0s):
        sc = [scores_of(ks_ref, k0, q4s) for k0 in k0s]
        for h in range(H):
            for k0, s in zip(k0s, sc):
                absorb(s, vst_ref, k0, acc_s, m_s, heads=(h,))

    def main_body(a, c):
        sel_tiles([pl.multiple_of((a * unroll + u) * tk, tk) for u in range(unroll)])
        return c

    lax.fori_loop(0, i // unroll, main_body, 0)
    bit = unroll // 2
    while bit >= 1:
        first = i - jnp.bitwise_and(i, 2 * bit - 1)

        @pl.when(jnp.bitwise_and(i, bit) != 0)
        def _(first=first, bit=bit):
            sel_tiles([pl.multiple_of((first + u) * tk, tk) for u in range(bit)])

        bit //= 2

    k_diag = pl.multiple_of(qs, tk)
    s_diag = scores_of(ks_ref, k_diag, q4s)
    o_w = acc_w[0:dh, :] * (1.0 / acc_w[dh:dh + 1, :])
    gt[...] = gate_ref[...].T
    absorb(s_diag, vst_ref, k_diag, acc_s, m_s, keep=causal)
    o_s = acc_s[0:dh, :] * (1.0 / acc_s[dh:dh + 1, :])

    h0 = H * pl.program_id(1)
    outs = []
    for h, cols in enumerate(head_cols):
        outs.append(gt[pl.ds(h0 + h, 1), :] * ocmp[:, cols]
                    + gt[pl.ds(NSA_HEADS + h0 + h, 1), :] * o_s[:, cols]
                    + gt[pl.ds(2 * NSA_HEADS + h0 + h, 1), :] * o_w[:, cols])
    o_ref[...] = jnp.concatenate(outs, axis=0).T.astype(BF16)


def _nsa_attn(q, kc, vct, ks, vst, kw, vwt, gates, ovt, *, tq=256, n_grp=4, unroll=4):
    B, S, D = q.shape
    G, P, dh = NSA_KV_GROUPS, NSA_HPG, NSA_HEAD_DIM
    NC = kc.shape[2]
    n_sel = S // SEL_BLOCK
    assert tq % LANES == 0 and tq & (tq - 1) == 0 and WINDOW % tq == 0 and G % n_grp == 0
    assert n_sel % 16 == 0 and n_sel <= LANES - dh and SEL_BLOCK & (SEL_BLOCK - 1) == 0
    L = n_grp * P * tq
    qblk = pl.BlockSpec((None, tq, n_grp * P * dh), lambda b, g, i: (b, i, g))
    grp = lambda *tail: pl.BlockSpec((None, n_grp) + tail, lambda b, g, i: (b, g) + (0,) * len(tail),
                                     pipeline_mode=pl.Buffered(1))
    seq, seq_t = grp(S, LANES), grp(S // LANES, VT_ROWS, LANES)
    return pl.pallas_call(
        functools.partial(_nsa_attn_kernel, tq=tq, n_sel=n_sel, n_grp=n_grp, unroll=unroll),
        grid=(B, G // n_grp, S // tq),
        in_specs=[qblk, grp(NC, LANES), grp(dh, NC), seq, seq_t, seq, seq_t,
                  pl.BlockSpec((None, tq, LANES), lambda b, g, i: (b, i, 0)),
                  _const_spec(ovt.shape)],
        out_specs=qblk,
        out_shape=jax.ShapeDtypeStruct((B, S, D), BF16),
        scratch_shapes=[pltpu.VMEM((LANES, L), BF16),
                        pltpu.VMEM((LANES, L), BF16),
                        pltpu.VMEM((VT_ROWS, L), F32),
                        pltpu.VMEM((8, L), F32),
                        pltpu.VMEM((VT_ROWS, L), F32),
                        pltpu.VMEM((8, L), F32),
                        pltpu.VMEM((dh, L), F32),
                        pltpu.VMEM((n_grp, n_sel, tq), F32),
                        pltpu.VMEM((n_grp, n_sel, tq), F32),
                        pltpu.VMEM((LANES, tq), F32)],
        compiler_params=pltpu.CompilerParams(
            dimension_semantics=("parallel", "parallel", "arbitrary"), vmem_limit_bytes=VMEM_LIMIT),
        name="nsa_attn",
    )(q, kc, vct, ks, vst, kw, vwt, gates, ovt)


def _sgu_kernel(x_ref, g_ref, w_ref, lng_ref, lnb_ref, ws_ref, bs_ref, o_ref, *, width):
    T = SGU_CHUNK
    tm = x_ref.shape[0]
    xn = _rmsnorm(x_ref[...], g_ref[...]).astype(BF16)
    y = jax.nn.gelu(_dot(xn, w_ref[...]))
    u, v = y[:, :width], y[:, width:]
    mu = jnp.mean(v, axis=-1, keepdims=True)
    vc = v - mu
    var = jnp.mean(vc * vc, axis=-1, keepdims=True)
    v = (vc * lax.rsqrt(var + LN_EPS) * lng_ref[...] + lnb_ref[...]).astype(BF16)
    r_i = lax.broadcasted_iota(jnp.int32, (T, T), 0)
    c_i = lax.broadcasted_iota(jnp.int32, (T, T), 1)
    gd = width // SGU_GROUPS
    for gi in range(SGU_GROUPS):
        ws = jnp.where(r_i >= c_i, ws_ref[gi], 0.0).astype(BF16)
        cols = slice(gi * gd, (gi + 1) * gd)
        for c in range(tm // T):
            rws = slice(c * T, (c + 1) * T)
            sv = _dot(ws, v[rws, cols]) + bs_ref[gi]
            o_ref[rws, cols] = (u[rws, cols] * sv).astype(BF16)


def _sgu(x2, g, w_in, ln_g, ln_b, w_s, bs_b, *, tm=1024):
    N, D = x2.shape
    W = w_in.shape[1] // 2
    tok = lambda w: pl.BlockSpec((tm, w), lambda i: (i, 0))
    return pl.pallas_call(
        functools.partial(_sgu_kernel, width=W),
        grid=(N // tm,),
        in_specs=[tok(D), _const_spec((1, D)), _const_spec(w_in.shape), _const_spec((1, W)), _const_spec((1, W)),
                  _const_spec(w_s.shape), _const_spec(bs_b.shape)],
        out_specs=tok(W),
        out_shape=jax.ShapeDtypeStruct((N, W), BF16),
        compiler_params=pltpu.CompilerParams(dimension_semantics=("parallel",), vmem_limit_bytes=VMEM_LIMIT),
        name="sgu",
    )(x2, g, w_in, ln_g, ln_b, w_s, bs_b)


def _pool_stage_start(lo, span):
    return -(-(lo + span) // 8) * 8


def _pool_kernel(x_ref, g_ref, w_ref, wg_ref, sc_ref, o_ref, zbuf, sum_a, sum_b, *, hist):
    tm = x_ref.shape[0]
    i = pl.program_id(1)
    rows = hist + tm

    @pl.when(i == 0)
    def _():
        zbuf[0:hist, :] = jnp.zeros((hist, zbuf.shape[1]), F32)

    xn = _rmsnorm(x_ref[...], g_ref[...]).astype(BF16)
    z = _dot(xn, w_ref[...])
    zbuf[hist:rows, :] = z
    t = i * tm + lax.broadcasted_iota(jnp.int32, (tm, 1), 0)
    gd = z.shape[1] // len(POOL_WINDOWS)
    src, span, lo = zbuf, 1, 0
    for s, w in enumerate(POOL_WINDOWS):
        assert w == 2 * span
        lo = _pool_stage_start(lo, span)
        dst = sum_a if s % 2 == 0 else sum_b
        c0 = s * gd
        dst[lo:rows, c0:] = src[lo:rows, c0:] + src[lo - span:rows - span, c0:]
        src, span = dst, w
    assert lo <= hist
    for gi, w in enumerate(POOL_WINDOWS):
        cols = slice(gi * gd, (gi + 1) * gd)
        win = (sum_a if gi % 2 == 0 else sum_b)[hist:rows, cols]
        cnt = jnp.minimum(t + 1, w).astype(F32)
        d = win / cnt - z[:, cols]
        y = _dot(d.astype(BF16), wg_ref[gi]) * sc_ref[:, cols]
        o_ref[:, cols] = y.astype(BF16)
    zbuf[0:hist, :] = zbuf[tm:rows, :]


def _pool(x, g, w_in, w_grp, scale, *, tm=1024):
    B, S, D = x.shape
    PW = w_in.shape[1]
    hist, span = 0, 1
    for w in POOL_WINDOWS:
        hist, span = _pool_stage_start(hist, span), w
    assert tm >= hist
    tok = lambda w: pl.BlockSpec((None, tm, w), lambda b, i: (b, i, 0))
    return pl.pallas_call(
        functools.partial(_pool_kernel, hist=hist),
        grid=(B, S // tm),
        in_specs=[tok(D), _const_spec((1, D)), _const_spec(w_in.shape), _const_spec(w_grp.shape),
                  _const_spec((1, PW))],
        out_specs=tok(PW),
        out_shape=jax.ShapeDtypeStruct((B, S, PW), BF16),
        scratch_shapes=[pltpu.VMEM((hist + tm, PW), F32)] * 3,
        compiler_params=pltpu.CompilerParams(
            dimension_semantics=("arbitrary", "arbitrary"), vmem_limit_bytes=VMEM_LIMIT),
        name="pool",
    )(x, g, w_in, w_grp, scale)


def _post_kernel(x_ref, u_ref, wo_ref, g_ref, wgu_ref, wd_ref, fg_ref, o_ref, *, hidden, final):
    x1 = x_ref[...] + _dot(u_ref[...], wo_ref[...])
    xn = _rmsnorm(x1, g_ref[...]).astype(BF16)
    gate = _dot(xn, wgu_ref[:, :hidden])
    up = _dot(xn, wgu_ref[:, hidden:])
    a = (jax.nn.silu(gate) * up).astype(BF16)
    x2 = x1 + _dot(a, wd_ref[...])
    if final:
        x2 = _rmsnorm(x2, fg_ref[...])
    o_ref[...] = x2


def _post(x2, u2, w_out, g, w_gu_all, w_down_all, fg, *, layer, final, tm=512):
    N, D = x2.shape
    hidden = w_down_all.shape[1]
    tok = lambda w: pl.BlockSpec((tm, w), lambda i: (i, 0))
    of_layer = lambda a: pl.BlockSpec((None,) + a.shape[1:], lambda i: (layer, 0, 0), pipeline_mode=pl.Buffered(1))
    return pl.pallas_call(
        functools.partial(_post_kernel, hidden=hidden, final=final),
        grid=(N // tm,),
        in_specs=[tok(D), tok(u2.shape[1]), _const_spec(w_out.shape), _const_spec((1, D)),
                  of_layer(w_gu_all), of_layer(w_down_all), _const_spec((1, D))],
        out_specs=tok(D),
        out_shape=jax.ShapeDtypeStruct((N, D), F32),
        compiler_params=pltpu.CompilerParams(dimension_semantics=("parallel",), vmem_limit_bytes=VMEM_LIMIT),
        name="post",
    )(x2, u2, w_out, g, w_gu_all, w_down_all, fg)


def _rope_table(positions):
    inv_freq = ROPE_THETA ** (-jnp.arange(0, ROPE_DIM, 2, dtype=F32) / ROPE_DIM)
    ang = positions.astype(F32)[..., None] * inv_freq
    pad = jnp.zeros(ang.shape[:-1] + (NSA_HEAD_DIM - ROPE_DIM,), F32)
    return jnp.tile(jnp.concatenate([jnp.cos(ang), jnp.sin(ang), pad], axis=-1), (1, 1, LANES // NSA_HEAD_DIM))


def _permute_nsa_w_in(w):
    H, G, dh = NSA_HEADS, NSA_KV_GROUPS, NSA_HEAD_DIM
    base = H * dh
    seg = lambda k: w[:, base + k * G * dh: base + (k + 1) * G * dh]
    kc, vc, ks, vs, kw, vw = (seg(k) for k in range(6))
    gl = w[:, base + 6 * G * dh:]
    inter = lambda a, b: jnp.concatenate(
        [a.reshape(-1, G, 1, dh), b.reshape(-1, G, 1, dh)], axis=2).reshape(-1, 2 * G * dh)
    glp = jnp.pad(gl, ((0, 0), (0, LANES - gl.shape[1])))
    return jnp.concatenate([w[:, :base], inter(ks, vs), inter(kw, vw), inter(kc, vc), glp], axis=1).astype(BF16)


def _overlap_matrix_t(n_chunks, n_sel):
    n = np.arange(n_chunks)[None, :]
    m = np.arange(LANES)[:, None]
    start, end = n * CMP_STRIDE, n * CMP_STRIDE + CMP_BLOCK - 1
    ov = (start <= m * SEL_BLOCK + SEL_BLOCK - 1) & (end >= m * SEL_BLOCK) & (m < n_sel)
    return jnp.asarray(ov, BF16)


def _cmp_weights(k_w1, v_w1):
    dh = NSA_HEAD_DIM
    kw = k_w1.reshape(CMP_BLOCK, dh, -1)
    vw = v_w1.reshape(CMP_BLOCK, dh, -1)
    z = jnp.zeros_like(kw)
    w = jnp.concatenate([jnp.concatenate([kw, z], axis=2), jnp.concatenate([z, vw], axis=2)], axis=1).astype(BF16)
    w = w.reshape(CMP_BLOCK // 2, 2 * w.shape[1], w.shape[2])
    return w[:CMP_STRIDE // 2], w[CMP_STRIDE // 2:]


def _nsa_layer(x, rot_tab, g, w_in, pos_k, k_w1, k_w2, pos_v, v_w1, v_w2):
    B, S, D = x.shape
    dh = NSA_HEAD_DIM
    assert CMP_BLOCK == 2 * CMP_STRIDE
    q, ks, vst, kw, vwt, kcv, gates = _nsa_in(x, g.reshape(1, D), _permute_nsa_w_in(w_in), rot_tab)
    pos = jnp.concatenate([pos_k, pos_v], axis=1)
    wa, wb = _cmp_weights(k_w1, v_w1)
    pad_r = lambda w2: jnp.pad(w2, ((0, 0), (0, LANES - dh))).astype(BF16)
    pad_l = lambda w2: jnp.pad(w2, ((0, 0), (LANES - dh, 0))).astype(BF16)
    kc, vct = _nsa_cmp(kcv, pos[:CMP_STRIDE], pos[CMP_STRIDE:], wa, wb, pad_r(k_w2), pad_l(v_w2))
    ovt = _overlap_matrix_t(S // CMP_STRIDE, S // SEL_BLOCK)
    return _nsa_attn(q, kc, vct, ks, vst, kw, vwt, gates, ovt)


def kernel(x, positions, mix_norm, ffn_norm, final_norm, nsa_w_in, nsa_cmp_pos_k, nsa_cmp_k_w1, nsa_cmp_k_w2, nsa_cmp_pos_v, nsa_cmp_v_w1, nsa_cmp_v_w2, nsa_w_out, sgu_w_in, sgu_ln_g, sgu_ln_b, sgu_w_s, sgu_b_s, sgu_w_out, pool_w_in, pool_w_grp, pool_scale, pool_w_out, ffn_w_gate_up, ffn_w_down):
    B, S, D = x.shape
    depth = mix_norm.shape[0]
    rot_tab = _rope_table(positions)
    row = lambda v: v.reshape(1, -1)
    w_gu_all, w_down_all = ffn_w_gate_up.astype(BF16), ffn_w_down.astype(BF16)
    for i in range(depth):
        kind, j = i % N_MIXERS, i // N_MIXERS
        if kind == 0:
            u = _nsa_layer(x, rot_tab, mix_norm[i], nsa_w_in[j], nsa_cmp_pos_k[j], nsa_cmp_k_w1[j], nsa_cmp_k_w2[j],
                           nsa_cmp_pos_v[j], nsa_cmp_v_w1[j], nsa_cmp_v_w2[j])
            w_out = nsa_w_out[j]
        elif kind == 1:
            gd = sgu_w_in.shape[-1] // 2 // SGU_GROUPS
            bs_b = jnp.broadcast_to(sgu_b_s[j][:, :, None], sgu_b_s[j].shape + (gd,))
            u = _sgu(x.reshape(B * S, D), row(mix_norm[i]), sgu_w_in[j].astype(BF16), row(sgu_ln_g[j]),
                     row(sgu_ln_b[j]), sgu_w_s[j], bs_b)
            w_out = sgu_w_out[j]
        else:
            u = _pool(x, row(mix_norm[i]), pool_w_in[j].astype(BF16), pool_w_grp[j].astype(BF16),
                      row(pool_scale[j]))
            w_out = pool_w_out[j]
        x = _post(x.reshape(B * S, D), u.reshape(B * S, -1), w_out.astype(BF16), row(ffn_norm[i]),
                  w_gu_all, w_down_all, row(final_norm), layer=i, final=(i == depth - 1)).reshape(B, S, D)
    return x
```
